```python
import jax
import jax.numpy as jnp
from jax import lax
import numpy as np

D_MODEL = 4096
BATCH = 2
SEQ = 4096
DEPTH = 1

HEAD_DIM = 128
N_HEADS = D_MODEL // HEAD_DIM
N_NSA_HEADS = N_HEADS // 2
N_NSA_KV = 4
NSA_REP = N_NSA_HEADS // N_NSA_KV
N_DIL_HEADS = N_HEADS - N_NSA_HEADS
CMP_BLOCK = 32
CMP_STRIDE = 16
CMP_HIDDEN = 256
SLC_BLOCK = 64
SLC_TOP_N = 16
SLC_Q_BLOCK = 32
FORCED_SCORE = 1.0e4
WIN_SIZE = 512
DIL_PATTERNS = ((128, 1), (512, 4), (2048, 16))
BAND_BLOCK = 128
N_EXPERTS = 32
TOP_K = 4
D_EXPERT = 1536
SWIGLU_ALPHA = 1.702
SWIGLU_LIMIT = 7.0
RMS_EPS = 1e-6
NEG_INF = -1e30

Q_NSA_DIM = N_NSA_HEADS * HEAD_DIM
KV_NSA_DIM = N_NSA_KV * HEAD_DIM
GATE_DIM = N_NSA_HEADS * 3
DIL_DIM = N_DIL_HEADS * HEAD_DIM
IN_SPLITS = (Q_NSA_DIM,) + (KV_NSA_DIM,) * 6 + (GATE_DIM, DIL_DIM, DIL_DIM, DIL_DIM)
D_IN = sum(IN_SPLITS)

kernel_name = 'hybrid_nsa_dilated_moe_block'


def _rms_norm(x, g):
    xf = x.astype(jnp.float32)
    y = xf * lax.rsqrt(jnp.mean(xf * xf, axis=-1, keepdims=True) + RMS_EPS)
    return (y * g.astype(jnp.float32)).astype(x.dtype)


def _alibi_slopes():
    i = jnp.arange(1, N_HEADS + 1, dtype=jnp.float32)
    return jnp.exp2(-8.0 * i / N_HEADS)


def _banded_attention(q, k, v, slopes, max_steps, step):
    L, dh = q.shape[-2], q.shape[-1]
    nb = -(-L // BAND_BLOCK)
    pad = nb * BAND_BLOCK - L
    n_prev = -(-max_steps // BAND_BLOCK)

    def pad_seq(a):
        return jnp.pad(a, [(0, 0)] * (a.ndim - 2) + [(0, pad), (0, 0)])

    def windows(a):
        ab = pad_seq(a).reshape(a.shape[:-2] + (nb, BAND_BLOCK, dh))
        ab = jnp.pad(ab, [(0, 0)] * (ab.ndim - 3) + [(n_prev, 0), (0, 0), (0, 0)])
        return jnp.concatenate([ab[..., j:j + nb, :, :] for j in range(n_prev + 1)], axis=-2)

    qb = pad_seq(q).reshape(q.shape[:-2] + (nb, BAND_BLOCK, dh))
    kw, vw = windows(k), windows(v)
    s = jnp.einsum('...rnqd,...nkd->...rnqk', qb, kw).astype(jnp.float32) * (dh ** -0.5)
    tq = jnp.arange(nb * BAND_BLOCK).reshape(nb, BAND_BLOCK)
    tk = (jnp.arange(nb)[:, None] - n_prev) * BAND_BLOCK + jnp.arange((n_prev + 1) * BAND_BLOCK)[None, :]
    dist = tq[:, :, None] - tk[:, None, :]
    valid = (dist >= 0) & (dist <= max_steps) & (tk[:, None, :] >= 0)
    s = s - slopes[..., None, None, None] * (dist * step).astype(jnp.float32)
    s = jnp.where(valid, s, NEG_INF)
    m = jnp.max(s, axis=-1, keepdims=True)
    p = jnp.exp(s - m)
    l = jnp.sum(p, axis=-1, keepdims=True)
    o = jnp.einsum('...rnqk,...nkd->...rnqd', (p / l).astype(v.dtype), vw)
    lse = (m + jnp.log(l))[..., 0]
    o = o.reshape(o.shape[:-3] + (nb * BAND_BLOCK, dh))[..., :L, :]
    lse = lse.reshape(lse.shape[:-2] + (nb * BAND_BLOCK,))[..., :L]
    return o, lse


def _compress(kv, pos_emb, w1, w2):
    B, S, G, dh = kv.shape
    r = CMP_BLOCK // CMP_STRIDE
    n_chunks = S // CMP_STRIDE
    n_cmp = n_chunks - r + 1
    ch = kv.reshape(B, n_chunks, CMP_STRIDE, G, dh)
    blocks = jnp.concatenate([ch[:, j:j + n_cmp] for j in range(r)], axis=2)
    blocks = blocks + pos_emb[None, None, :, None, :]
    flat = blocks.transpose(0, 1, 3, 2, 4).reshape(B, n_cmp, G, CMP_BLOCK * dh)
    out = jax.nn.silu(flat @ w1) @ w2
    return out.transpose(0, 2, 1, 3)


def _nsa_mixer(q, k_cmp, v_cmp, k_slc, v_slc, k_win, v_win, gate_logits, slopes,
               pe_k, w_k1, w_k2, pe_v, w_v1, w_v2, qn, kn_cmp, kn_slc, kn_win):
    B, S = q.shape[0], q.shape[1]
    G, R, dh = N_NSA_KV, NSA_REP, HEAD_DIM
    scale = dh ** -0.5
    t = jnp.arange(S)
    qg = _rms_norm(q.reshape(B, S, G, R, dh), qn).transpose(0, 2, 3, 1, 4)

    kc = _rms_norm(_compress(k_cmp.reshape(B, S, G, dh), pe_k, w_k1, w_k2), kn_cmp)
    vc = _compress(v_cmp.reshape(B, S, G, dh), pe_v, w_v1, w_v2)
    n_cmp = kc.shape[2]
    blk_end = jnp.arange(n_cmp) * CMP_STRIDE + CMP_BLOCK - 1
    dist_c = t[:, None] - blk_end[None, :]
    valid_c = dist_c >= 0
    sc = jnp.einsum('bgrtd,bgnd->bgrtn', qg, kc).astype(jnp.float32) * scale
    sc = sc - slopes[..., None, None] * dist_c.astype(jnp.float32)
    sc = jnp.where(valid_c, sc, NEG_INF)
    mc = jnp.max(sc, axis=-1, keepdims=True)
    pc = jnp.where(valid_c, jnp.exp(sc - mc), 0.0)
    p_cmp = pc / jnp.maximum(jnp.sum(pc, axis=-1, keepdims=True), 1e-30)
    o_cmp = jnp.einsum('bgrtn,bgnd->bgrtd', p_cmp.astype(vc.dtype), vc)

    n_slc = S // SLC_BLOCK
    ci = np.arange(n_cmp)[:, None] * CMP_STRIDE
    sj = np.arange(n_slc)[None, :] * SLC_BLOCK
    overlap = ((ci < sj + SLC_BLOCK) & (ci + CMP_BLOCK > sj)).astype(np.float32)
    imp = jnp.einsum('bgrtn,nj->bgtj', p_cmp, jnp.asarray(overlap))
    jb = jnp.arange(n_slc)[None, :]
    cur = (t // SLC_BLOCK)[:, None]
    causal = jb * SLC_BLOCK <= t[:, None]
    forced = (jb == 0) | (jb == cur) | (jb == cur - 1)
    score = jnp.where(forced, FORCED_SCORE, jnp.where(causal, imp, -1.0))
    n_top = min(SLC_TOP_N, n_slc)
    _, idx = lax.top_k(score, n_top)

    ks = _rms_norm(k_slc.reshape(B, S, G, dh), kn_slc).transpose(0, 2, 1, 3)
    kb = ks.reshape(B, G, n_slc, SLC_BLOCK, dh)
    vb = v_slc.reshape(B, S, G, dh).transpose(0, 2, 1, 3).reshape(B, G, n_slc, SLC_BLOCK, dh)
    nq = S // SLC_Q_BLOCK
    q_c = qg.reshape(B, G, R, nq, SLC_Q_BLOCK, dh).transpose(3, 0, 1, 2, 4, 5)
    idx_c = idx.reshape(B, G, nq, SLC_Q_BLOCK, n_top).transpose(2, 0, 1, 3, 4)
    t_c = t.reshape(nq, SLC_Q_BLOCK)
    bi = jnp.arange(B)[:, None, None, None]
    gi = jnp.arange(G)[None, :, None, None]
    offs = jnp.arange(SLC_BLOCK)

    def attend_selected(args):
        qc, ic, tc = args
        kg = kb[bi, gi, ic]
        vg = vb[bi, gi, ic]
        s = jnp.einsum('bgrcd,bgcnkd->bgrcnk', qc, kg).astype(jnp.float32) * scale
        dist = tc[None, None, :, None, None] - (ic[..., None] * SLC_BLOCK + offs)
        s = s - slopes[None, :, :, None, None, None] * dist[:, :, None].astype(jnp.float32)
        s = jnp.where(dist[:, :, None] >= 0, s, NEG_INF)
        p = jax.nn.softmax(s.reshape(B, G, R, SLC_Q_BLOCK, -1), axis=-1)
        return jnp.einsum('bgrcm,bgcmd->bgrcd', p.astype(vg.dtype), vg.reshape(B, G, SLC_Q_BLOCK, -1, dh))

    o_slc = lax.map(attend_selected, (q_c, idx_c, t_c))
    o_slc = o_slc.transpose(1, 2, 3, 0, 4, 5).reshape(B, G, R, S, dh)

    kw = _rms_norm(k_win.reshape(B, S, G, dh), kn_win).transpose(0, 2, 1, 3)
    vw = v_win.reshape(B, S, G, dh).transpose(0, 2, 1, 3)
    o_win, _ = _banded_attention(qg, kw, vw, slopes, WIN_SIZE - 1, 1)

    g = jax.nn.sigmoid(gate_logits.reshape(B, S, G, R, 3).astype(jnp.float32))
    g = g.transpose(0, 2, 3, 1, 4).astype(q.dtype)
    o = g[..., 0:1] * o_cmp + g[..., 1:2] * o_slc + g[..., 2:3] * o_win
    return o.transpose(0, 3, 1, 2, 4).reshape(B, S, Q_NSA_DIM)


def _dilated_mixer(q, k, v, slopes, qn, kn):
    B, S = q.shape[0], q.shape[1]
    H, dh = N_DIL_HEADS, HEAD_DIM
    qh = _rms_norm(q.reshape(B, S, H, dh), qn).transpose(0, 2, 1, 3)
    kh = _rms_norm(k.reshape(B, S, H, dh), kn).transpose(0, 2, 1, 3)
    vh = v.reshape(B, S, H, dh).transpose(0, 2, 1, 3)
    outs, lses = [], []
    for window, dil in DIL_PATTERNS:
        L = S // dil
        qd = qh.reshape(B, H, L, dil, dh).transpose(0, 3, 1, 2, 4)[:, :, :, None]
        kd = kh.reshape(B, H, L, dil, dh).transpose(0, 3, 1, 2, 4)
        vd = vh.reshape(B, H, L, dil, dh).transpose(0, 3, 1, 2, 4)
        o, lse = _banded_attention(qd, kd, vd, slopes[:, None], window // dil, dil)
        outs.append(o[:, :, :, 0].transpose(0, 2, 3, 1, 4).reshape(B, H, S, dh))
        lses.append(lse[:, :, :, 0].transpose(0, 2, 3, 1).reshape(B, H, S))
    wts = jax.nn.softmax(jnp.stack(lses), axis=0)
    o = jnp.sum(wts[..., None] * jnp.stack(outs).astype(jnp.float32), axis=0)
    return o.astype(q.dtype).transpose(0, 2, 1, 3).reshape(B, S, DIL_DIM)


def _moe(h, w_router, b_router, w_gate_up, b_gate_up, w_down, b_down):
    B, S, D = h.shape
    ht = h.reshape(B * S, D)
    logits = (ht @ w_router + b_router).astype(jnp.float32)
    top_vals, top_idx = lax.top_k(logits, TOP_K)
    top_w = jax.nn.softmax(top_vals, axis=-1)
    gates = jnp.sum(jax.nn.one_hot(top_idx, N_EXPERTS, dtype=jnp.float32) * top_w[..., None], axis=1)
    y = jnp.zeros((B * S, D), jnp.float32)
    for e in range(N_EXPERTS):
        gu = ht @ w_gate_up[e] + b_gate_up[e]
        gate = jnp.minimum(gu[:, :D_EXPERT], SWIGLU_LIMIT)
        up = jnp.clip(gu[:, D_EXPERT:], -SWIGLU_LIMIT, SWIGLU_LIMIT)
        act = (up + 1.0) * (gate * jax.nn.sigmoid(SWIGLU_ALPHA * gate))
        y = y + gates[:, e:e + 1] * (act @ w_down[e] + b_down[e]).astype(jnp.float32)
    return y.astype(h.dtype).reshape(B, S, D)


def setup_inputs(seed: int = 0) -> dict:
    key = jax.random.key(seed)
    ks = jax.random.split(key, 26)
    f32 = jnp.float32
    dh = HEAD_DIM

    def nrm(k, shape, scale):
        return jax.random.normal(k, shape, f32) * scale

    def gain(k, shape):
        return 1.0 + 0.02 * jax.random.normal(k, shape, f32)

    return {
        'x': nrm(ks[0], (BATCH, SEQ, D_MODEL), 1.0),
        'attn_norm': gain(ks[1], (DEPTH, D_MODEL)),
        'w_in': nrm(ks[2], (DEPTH, D_MODEL, D_IN), D_MODEL ** -0.5),
        'cmp_pos_k': nrm(ks[3], (DEPTH, CMP_BLOCK, dh), 0.5),
        'w_cmp_k1': nrm(ks[4], (DEPTH, CMP_BLOCK * dh, CMP_HIDDEN), (CMP_BLOCK * dh) ** -0.5),
        'w_cmp_k2': nrm(ks[5], (DEPTH, CMP_HIDDEN, dh), CMP_HIDDEN ** -0.5),
        'cmp_pos_v': nrm(ks[6], (DEPTH, CMP_BLOCK, dh), 0.5),
        'w_cmp_v1': nrm(ks[7], (DEPTH, CMP_BLOCK * dh, CMP_HIDDEN), (CMP_BLOCK * dh) ** -0.5),
        'w_cmp_v2': nrm(ks[8], (DEPTH, CMP_HIDDEN, dh), CMP_HIDDEN ** -0.5),
        'q_norm_nsa': gain(ks[9], (DEPTH, dh)),
        'k_norm_cmp': gain(ks[10], (DEPTH, dh)),
        'k_norm_slc': gain(ks[11], (DEPTH, dh)),
        'k_norm_win': gain(ks[12], (DEPTH, dh)),
        'q_norm_dil': gain(ks[13], (DEPTH, dh)),
        'k_norm_dil': gain(ks[14], (DEPTH, dh)),
        'out_norm_nsa': gain(ks[15], (DEPTH, Q_NSA_DIM)),
        'out_norm_dil': gain(ks[16], (DEPTH, DIL_DIM)),
        'w_out': nrm(ks[17], (DEPTH, Q_NSA_DIM + DIL_DIM, D_MODEL), (Q_NSA_DIM + DIL_DIM) ** -0.5),
        'ffn_norm': gain(ks[18], (DEPTH, D_MODEL)),
        'w_router': nrm(ks[19], (DEPTH, D_MODEL, N_EXPERTS), D_MODEL ** -0.5),
        'b_router': nrm(ks[20], (DEPTH, N_EXPERTS), 0.01),
        'w_gate_up': nrm(ks[21], (DEPTH, N_EXPERTS, D_MODEL, 2 * D_EXPERT), D_MODEL ** -0.5),
        'b_gate_up': nrm(ks[22], (DEPTH, N_EXPERTS, 2 * D_EXPERT), 0.01),
        'w_down': nrm(ks[23], (DEPTH, N_EXPERTS, D_EXPERT, D_MODEL), D_EXPERT ** -0.5),
        'b_down': nrm(ks[24], (DEPTH, N_EXPERTS, D_MODEL), 0.01),
    }


def reference(x, attn_norm, w_in, cmp_pos_k, w_cmp_k1, w_cmp_k2, cmp_pos_v, w_cmp_v1, w_cmp_v2,
              q_norm_nsa, k_norm_cmp, k_norm_slc, k_norm_win, q_norm_dil, k_norm_dil,
              out_norm_nsa, out_norm_dil, w_out, ffn_norm, w_router, b_router,
              w_gate_up, b_gate_up, w_down, b_down):
    B, S, _ = x.shape
    slopes = _alibi_slopes()
    slopes_nsa = slopes[0::2].reshape(N_NSA_KV, NSA_REP)
    slopes_dil = slopes[1::2]
    split_points = np.cumsum(IN_SPLITS)[:-1].tolist()
    for layer in range(DEPTH):
        h = _rms_norm(x, attn_norm[layer])
        proj = h @ w_in[layer]
        (q_n, k_c, v_c, k_s, v_s, k_w, v_w, g_n, q_d, k_d, v_d) = jnp.split(proj, split_points, axis=-1)
        o_nsa = _nsa_mixer(q_n, k_c, v_c, k_s, v_s, k_w, v_w, g_n, slopes_nsa,
                           cmp_pos_k[layer], w_cmp_k1[layer], w_cmp_k2[layer],
                           cmp_pos_v[layer], w_cmp_v1[layer], w_cmp_v2[layer],
                           q_norm_nsa[layer], k_norm_cmp[layer], k_norm_slc[layer], k_norm_win[layer])
        o_dil = _dilated_mixer(q_d, k_d, v_d, slopes_dil, q_norm_dil[layer], k_norm_dil[layer])
        mixed = jnp.concatenate([_rms_norm(o_nsa, out_norm_nsa[layer]),
                                 _rms_norm(o_dil, out_norm_dil[layer])], axis=-1)
        x = x + mixed @ w_out[layer]
        x = x + _moe(_rms_norm(x, ffn_norm[layer]), w_router[layer], b_router[layer],
                     w_gate_up[layer], b_gate_up[layer], w_down[layer], b_down[layer])
    return x
```

```python
import functools

import jax
import jax.numpy as jnp
from jax import lax
from jax.experimental import pallas as pl
from jax.experimental.pallas import tpu as pltpu

F32, BF16, I32, U32 = jnp.float32, jnp.bfloat16, jnp.int32, jnp.uint32

HEAD_DIM = 128
N_NSA_HEADS = 16
N_NSA_KV = 4
NSA_REP = N_NSA_HEADS // N_NSA_KV
N_DIL_HEADS = 16
CMP_BLOCK = 32
CMP_STRIDE = 16
CMP_HIDDEN = 256
SLC_BLOCK = 64
SLC_TOP_N = 16
FORCED_SCORE = 1.0e4
WIN_SIZE = 512
DIL_PATTERNS = ((128, 1), (512, 4), (2048, 16))
N_EXPERTS = 32
TOP_K = 4
D_EXPERT = 1536
SWIGLU_ALPHA = 1.702
SWIGLU_LIMIT = 7.0
RMS_EPS = 1e-6
NEG_INF = -1e30
ATTN_SCALE = HEAD_DIM ** -0.5

Q_NSA_DIM = N_NSA_HEADS * HEAD_DIM
KV_NSA_DIM = N_NSA_KV * HEAD_DIM
GATE_DIM = N_NSA_HEADS * 3
DIL_DIM = N_DIL_HEADS * HEAD_DIM

LANES = 128
VMEM_LIMIT = 56 * 1024 * 1024

CB_Q_NSA = 0
CB_K_CMP = CB_Q_NSA + N_NSA_HEADS
CB_V_CMP = CB_K_CMP + N_NSA_KV
CB_K_SLC = CB_V_CMP + N_NSA_KV
CB_V_SLC = CB_K_SLC + N_NSA_KV
CB_K_WIN = CB_V_SLC + N_NSA_KV
CB_V_WIN = CB_K_WIN + N_NSA_KV
CB_Q_DIL = CB_V_WIN + N_NSA_KV
CB_K_DIL = CB_Q_DIL + N_DIL_HEADS
CB_V_DIL = CB_K_DIL + N_DIL_HEADS
CB_GATE = CB_V_DIL + N_DIL_HEADS
PROJ_TN = 512
N_PROJ = -(-(CB_GATE + 1) * LANES // PROJ_TN) * PROJ_TN

PROJ_TM = 512
NSA_TQ = 128
DIL_TQ = 512
ATT_TK = 256
ROUTER_TM = 256
RANK_TM = 512
POS_TM = 1024
MOE_TM = 1024
MOE_SUB = 256
MOE_TF = 256
MOE_TN = 512
DISPATCH_TM = 256
COMBINE_TM = 128


def _dot(a, b, **kw):
    return jnp.dot(a, b, preferred_element_type=F32, **kw)


def _dot_nt(a, b):
    return lax.dot_general(a, b, (((1,), (1,)), ((), ())), preferred_element_type=F32)


def _rms_rows(x, gain):
    ms = jnp.mean(x * x, axis=-1, keepdims=True)
    return x * lax.rsqrt(ms + RMS_EPS) * gain


def _sigmoid(x):
    return 1.0 / (1.0 + jnp.exp(-x))


def _params(*sem):
    return pltpu.CompilerParams(dimension_semantics=sem, vmem_limit_bytes=VMEM_LIMIT)


def _in_proj_kernel(x_ref, g_ref, w_ref, cg_ref, cf_ref, o_ref, h_scr):
    @pl.when(pl.program_id(1) == 0)
    def _():
        h_scr[...] = _rms_rows(x_ref[...], g_ref[...]).astype(BF16)

    acc = _dot(h_scr[...], w_ref[...])
    for c in range(PROJ_TN // LANES):
        cs = slice(c * LANES, (c + 1) * LANES)
        a = acc[:, cs]
        r = lax.rsqrt(jnp.mean(a * a, axis=-1, keepdims=True) + RMS_EPS)
        y = a * jnp.where(cf_ref[:, cs] > 0, r, 1.0) * cg_ref[:, cs]
        o_ref[:, cs] = y.astype(o_ref.dtype)


def _in_proj(x2, gain, w, col_gain, col_flag):
    t, d = x2.shape
    n = w.shape[1]
    return pl.pallas_call(
        _in_proj_kernel,
        out_shape=jax.ShapeDtypeStruct((t, n), BF16),
        grid=(t // PROJ_TM, n // PROJ_TN),
        in_specs=[
            pl.BlockSpec((PROJ_TM, d), lambda i, j: (i, 0)),
            pl.BlockSpec((1, d), lambda i, j: (0, 0)),
            pl.BlockSpec((d, PROJ_TN), lambda i, j: (0, j)),
            pl.BlockSpec((1, PROJ_TN), lambda i, j: (0, j)),
            pl.BlockSpec((1, PROJ_TN), lambda i, j: (0, j)),
        ],
        out_specs=pl.BlockSpec((PROJ_TM, PROJ_TN), lambda i, j: (i, j)),
        scratch_shapes=[pltpu.VMEM((PROJ_TM, d), BF16)],
        compiler_params=_params("parallel", "arbitrary"),
        name="in_proj",
    )(x2, gain, w, col_gain, col_flag)


def _compress_kernel(a_ref, pe_ref, w1_ref, w2_ref, kn_ref, o_ref):
    a = (a_ref[...].astype(F32) + pe_ref[...]).astype(BF16)
    hid = _dot(a, w1_ref[...])
    hid = hid * _sigmoid(hid)
    out = _dot(hid.astype(BF16), w2_ref[...])
    normed = _rms_rows(out, kn_ref[...])
    o_ref[...] = jnp.where(pl.program_id(0) == 0, normed, out).astype(o_ref.dtype)


def _compress(blocks, pe, w1, w2, kn):
    two, bg, ncp, flat = blocks.shape
    return pl.pallas_call(
        _compress_kernel,
        out_shape=jax.ShapeDtypeStruct((two, bg, ncp, HEAD_DIM), BF16),
        grid=(two, bg),
        in_specs=[
            pl.BlockSpec((None, None, ncp, flat), lambda s, i: (s, i, 0, 0)),
            pl.BlockSpec((None, 1, flat), lambda s, i: (s, 0, 0)),
            pl.BlockSpec((None, flat, CMP_HIDDEN), lambda s, i: (s, 0, 0)),
            pl.BlockSpec((None, CMP_HIDDEN, HEAD_DIM), lambda s, i: (s, 0, 0)),
            pl.BlockSpec((1, HEAD_DIM), lambda s, i: (0, 0)),
        ],
        out_specs=pl.BlockSpec((None, None, ncp, HEAD_DIM), lambda s, i: (s, i, 0, 0)),
        compiler_params=_params("arbitrary", "arbitrary"),
        name="nsa_compress",
    )(blocks, pe, w1, w2, kn)


def _flash_step(r, q, k, v, valid, bias, weight, m_scr, l_scr, acc_scr):
    s = _dot_nt(q, k) * ATTN_SCALE - bias
    s = jnp.where(valid, s, NEG_INF)
    m_old = m_scr[r]
    m_new = jnp.maximum(m_old, jnp.max(s, axis=-1, keepdims=True))
    alpha = jnp.exp(m_old - m_new)
    e = jnp.exp(s - m_new)
    p = jnp.where(valid, e, 0.0) if weight is None else weight * e
    l_scr[r] = alpha * l_scr[r] + jnp.sum(p, axis=-1, keepdims=True)
    acc_scr[r] = alpha * acc_scr[r] + _dot(p.astype(BF16), v)
    m_scr[r] = m_new


def _flash_init(m_scr, l_scr, acc_scr):
    m_scr[...] = jnp.full(m_scr.shape, NEG_INF, F32)
    l_scr[...] = jnp.zeros(l_scr.shape, F32)
    acc_scr[...] = jnp.zeros(acc_scr.shape, F32)


def _nsa_kernel(sl_ref, q_ref, kc_ref, vc_ref, ks_ref, vs_ref, kw_ref, vw_ref, gt_ref, o_ref,
                m_scr, l_scr, acc_scr, ob_scr, *, n_slc):
    g = pl.program_id(1)
    qi = pl.program_id(2)
    t0 = qi * NSA_TQ
    row = lax.broadcasted_iota(I32, (NSA_TQ, 1), 0) + t0
    slopes = [sl_ref[g * NSA_REP + r] for r in range(NSA_REP)]
    qs = [q_ref[:, r * HEAD_DIM:(r + 1) * HEAD_DIM] for r in range(NSA_REP)]

    ncp = kc_ref.shape[0]
    blk_end = lax.broadcasted_iota(I32, (NSA_TQ, ncp), 1) * CMP_STRIDE + (CMP_BLOCK - 1)
    dist_c = row - blk_end
    valid_c = dist_c >= 0
    dist_cf = dist_c.astype(F32)
    kc = kc_ref[...]
    vc = vc_ref[...]
    psum = jnp.zeros((NSA_TQ, ncp), F32)
    for r in range(NSA_REP):
        s = _dot_nt(qs[r], kc) * ATTN_SCALE - slopes[r] * dist_cf
        s = jnp.where(valid_c, s, NEG_INF)
        mc = jnp.max(s, axis=-1, keepdims=True)
        pc = jnp.where(valid_c, jnp.exp(s - mc), 0.0)
        p = pc / jnp.maximum(jnp.sum(pc, axis=-1, keepdims=True), 1e-30)
        psum = psum + p
        ob_scr[0, r] = _dot(p.astype(BF16), vc)

    per_slc = SLC_BLOCK // CMP_STRIDE
    back = CMP_BLOCK // CMP_STRIDE - 1
    nl = lax.broadcasted_iota(I32, (ncp, LANES), 0)
    jl = lax.broadcasted_iota(I32, (ncp, LANES), 1)
    overlap = jnp.where((nl >= per_slc * jl - back) & (nl < per_slc * (jl + 1)), 1.0, 0.0)
    imp = _dot(psum, overlap, precision=lax.Precision.HIGHEST)
    j = lax.broadcasted_iota(I32, (NSA_TQ, LANES), 1)
    jf = j.astype(F32)
    cur = row // SLC_BLOCK
    forced = (j == 0) | (j == cur) | (j == cur - 1)
    score = jnp.where(forced, FORCED_SCORE, jnp.where(j * SLC_BLOCK <= row, imp, -1.0))
    score = jnp.where(j < n_slc, score, -3.0)
    sel = jnp.zeros((NSA_TQ, LANES), F32)
    for _ in range(min(SLC_TOP_N, n_slc)):
        mx = jnp.max(score, axis=-1, keepdims=True)
        first = jnp.min(jnp.where(score == mx, jf, float(LANES)), axis=-1, keepdims=True)
        pick = jf == first
        sel = jnp.where(pick, 1.0, sel)
        score = jnp.where(pick, -4.0, score)
    sel_b = sel.astype(BF16)

    col = lax.broadcasted_iota(I32, (NSA_TQ, ATT_TK), 1)
    blk_of_col = lax.broadcasted_iota(I32, (LANES, ATT_TK), 1) // SLC_BLOCK
    blk_row = lax.broadcasted_iota(I32, (LANES, ATT_TK), 0)
    last_kt = t0 // ATT_TK

    _flash_init(m_scr, l_scr, acc_scr)

    def slc_body(kt, carry):
        k0 = pl.multiple_of(kt * ATT_TK, ATT_TK)
        k = ks_ref[pl.ds(k0, ATT_TK), :]
        v = vs_ref[pl.ds(k0, ATT_TK), :]
        expand = jnp.where(blk_row == kt * (ATT_TK // SLC_BLOCK) + blk_of_col, 1.0, 0.0).astype(BF16)
        chosen = _dot(sel_b, expand)
        dist = row - (k0 + col)
        valid = (chosen > 0.5) & (dist >= 0)
        distf = dist.astype(F32)
        for r in range(NSA_REP):
            _flash_step(r, qs[r], k, v, valid, slopes[r] * distf, None, m_scr, l_scr, acc_scr)
        return carry

    lax.fori_loop(0, last_kt + 1, slc_body, 0)
    for r in range(NSA_REP):
        ob_scr[1, r] = acc_scr[r] / l_scr[r]

    _flash_init(m_scr, l_scr, acc_scr)

    def win_body(kt, carry):
        k0 = pl.multiple_of(kt * ATT_TK, ATT_TK)
        k = kw_ref[pl.ds(k0, ATT_TK), :]
        v = vw_ref[pl.ds(k0, ATT_TK), :]
        dist = row - (k0 + col)
        valid = (dist >= 0) & (dist <= WIN_SIZE - 1)
        distf = dist.astype(F32)
        for r in range(NSA_REP):
            _flash_step(r, qs[r], k, v, valid, slopes[r] * distf, None, m_scr, l_scr, acc_scr)
        return carry

    first_kt = jnp.maximum(t0 - (WIN_SIZE - 1), 0) // ATT_TK
    lax.fori_loop(first_kt, last_kt + 1, win_body, 0)

    gate = _sigmoid(gt_ref[...].astype(F32))
    for r in range(NSA_REP):
        o_win = acc_scr[r] / l_scr[r]
        o = (gate[:, 3 * r:3 * r + 1] * ob_scr[0, r] + gate[:, 3 * r + 1:3 * r + 2] * ob_scr[1, r]
             + gate[:, 3 * r + 2:3 * r + 3] * o_win)
        o_ref[:, r * HEAD_DIM:(r + 1) * HEAD_DIM] = o.astype(o_ref.dtype)


def _nsa_mixer(slopes, proj, cmp_kv, gates_t, batch, seq):
    t = proj.shape[0]
    nq = seq // NSA_TQ
    ncp = cmp_kv.shape[2]
    n_slc = seq // SLC_BLOCK
    assert n_slc <= LANES and seq % ATT_TK == 0

    def seq_spec(cb):
        return pl.BlockSpec((seq, HEAD_DIM), lambda b, g, i: (b, cb + g))

    def cmp_spec(which):
        return pl.BlockSpec((None, None, ncp, HEAD_DIM), lambda b, g, i: (which, b * N_NSA_KV + g, 0, 0))

    qo_spec = pl.BlockSpec((NSA_TQ, NSA_REP * HEAD_DIM), lambda b, g, i: (b * nq + i, g))
    return pl.pallas_call(
        functools.partial(_nsa_kernel, n_slc=n_slc),
        out_shape=jax.ShapeDtypeStruct((t, Q_NSA_DIM), BF16),
        grid=(batch, N_NSA_KV, nq),
        in_specs=[
            pl.BlockSpec(memory_space=pltpu.SMEM),
            qo_spec, cmp_spec(0), cmp_spec(1),
            seq_spec(CB_K_SLC), seq_spec(CB_V_SLC), seq_spec(CB_K_WIN), seq_spec(CB_V_WIN),
            pl.BlockSpec((None, NSA_TQ, 3 * NSA_REP), lambda b, g, i: (g, b * nq + i, 0)),
        ],
        out_specs=qo_spec,
        scratch_shapes=[
            pltpu.VMEM((NSA_REP, NSA_TQ, 1), F32),
            pltpu.VMEM((NSA_REP, NSA_TQ, 1), F32),
            pltpu.VMEM((NSA_REP, NSA_TQ, HEAD_DIM), F32),
            pltpu.VMEM((2, NSA_REP, NSA_TQ, HEAD_DIM), F32),
        ],
        compiler_params=_params("arbitrary", "arbitrary", "arbitrary"),
        name="nsa_mixer",
    )(slopes, proj, cmp_kv, cmp_kv, proj, proj, proj, proj, gates_t)


def _dil_kernel(sl_ref, q_ref, k_ref, v_ref, o_ref, m_scr, l_scr, acc_scr):
    slope = sl_ref[pl.program_id(1)]
    t0 = pl.program_id(2) * DIL_TQ
    n_rb = DIL_TQ // LANES
    rows = lax.broadcasted_iota(I32, (LANES, 1), 0)
    col = lax.broadcasted_iota(I32, (LANES, ATT_TK), 1)
    _flash_init(m_scr, l_scr, acc_scr)

    def body(kt, carry):
        k0 = pl.multiple_of(kt * ATT_TK, ATT_TK)
        k = k_ref[pl.ds(k0, ATT_TK), :]
        v = v_ref[pl.ds(k0, ATT_TK), :]
        for rb in range(n_rb):
            dist = (rows + (t0 + rb * LANES)) - (k0 + col)
            nonneg = dist >= 0
            mult = jnp.zeros((LANES, ATT_TK), F32)
            for window, dil in DIL_PATTERNS:
                hit = nonneg & (dist <= window) & ((dist & (dil - 1)) == 0)
                mult = mult + jnp.where(hit, 1.0, 0.0)
            q = q_ref[rb * LANES:(rb + 1) * LANES, :]
            _flash_step(rb, q, k, v, mult > 0.0, slope * dist.astype(F32), mult, m_scr, l_scr, acc_scr)
        return carry

    reach = max(w for w, _ in DIL_PATTERNS)
    first_kt = jnp.maximum(t0 - reach, 0) // ATT_TK
    lax.fori_loop(first_kt, (t0 + DIL_TQ - 1) // ATT_TK + 1, body, 0)
    for rb in range(n_rb):
        o_ref[rb * LANES:(rb + 1) * LANES, :] = (acc_scr[rb] / l_scr[rb]).astype(o_ref.dtype)


def _dil_mixer(slopes, proj, batch, seq):
    t = proj.shape[0]
    nq = seq // DIL_TQ
    n_rb = DIL_TQ // LANES
    assert all(d & (d - 1) == 0 for _, d in DIL_PATTERNS)
    return pl.pallas_call(
        _dil_kernel,
        out_shape=jax.ShapeDtypeStruct((t, DIL_DIM), BF16),
        grid=(batch, N_DIL_HEADS, nq),
        in_specs=[
            pl.BlockSpec(memory_space=pltpu.SMEM),
            pl.BlockSpec((DIL_TQ, HEAD_DIM), lambda b, h, i: (b * nq + i, CB_Q_DIL + h)),
            pl.BlockSpec((seq, HEAD_DIM), lambda b, h, i: (b, CB_K_DIL + h)),
            pl.BlockSpec((seq, HEAD_DIM), lambda b, h, i: (b, CB_V_DIL + h)),
        ],
        out_specs=pl.BlockSpec((DIL_TQ, HEAD_DIM), lambda b, h, i: (b * nq + i, h)),
        scratch_shapes=[
            pltpu.VMEM((n_rb, LANES, 1), F32),
            pltpu.VMEM((n_rb, LANES, 1), F32),
            pltpu.VMEM((n_rb, LANES, HEAD_DIM), F32),
        ],
        compiler_params=_params("arbitrary", "arbitrary", "arbitrary"),
        name="dil_mixer",
    )(slopes, proj, proj, proj)


def _out_proj_kernel(a_ref, b_ref, ga_ref, gb_ref, w_ref, res_ref, o_ref, h_scr):
    @pl.when(pl.program_id(1) == 0)
    def _():
        na = a_ref.shape[1]
        h_scr[:, :na] = _rms_rows(a_ref[...].astype(F32), ga_ref[...]).astype(BF16)
        h_scr[:, na:] = _rms_rows(b_ref[...].astype(F32), gb_ref[...]).astype(BF16)

    o_ref[...] = res_ref[...] + _dot(h_scr[...], w_ref[...])


def _out_proj(o_nsa, o_dil, g_nsa, g_dil, w, resid):
    t, d = resid.shape
    na, nb = o_nsa.shape[1], o_dil.shape[1]
    return pl.pallas_call(
        _out_proj_kernel,
        out_shape=jax.ShapeDtypeStruct((t, d), F32),
        grid=(t // PROJ_TM, d // PROJ_TN),
        in_specs=[
            pl.BlockSpec((PROJ_TM, na), lambda i, j: (i, 0)),
            pl.BlockSpec((PROJ_TM, nb), lambda i, j: (i, 0)),
            pl.BlockSpec((1, na), lambda i, j: (0, 0)),
            pl.BlockSpec((1, nb), lambda i, j: (0, 0)),
            pl.BlockSpec((na + nb, PROJ_TN), lambda i, j: (0, j)),
            pl.BlockSpec((PROJ_TM, PROJ_TN), lambda i, j: (i, j)),
        ],
        out_specs=pl.BlockSpec((PROJ_TM, PROJ_TN), lambda i, j: (i, j)),
        scratch_shapes=[pltpu.VMEM((PROJ_TM, na + nb), BF16)],
        compiler_params=_params("parallel", "arbitrary"),
        name="out_proj",
    )(o_nsa, o_dil, g_nsa, g_dil, w, resid)


def _router_kernel(x_ref, g_ref, wr_ref, br_ref, hp_ref, idx_ref, tw_ref):
    h = _rms_rows(x_ref[...], g_ref[...])
    logits = _dot(h, wr_ref[...], precision=lax.Precision.HIGHEST) + br_ref[...]
    lane = lax.broadcasted_iota(I32, logits.shape, 1)
    lanef = lane.astype(F32)
    work = jnp.where(lane < N_EXPERTS, logits, -jnp.inf)
    vals, ids = [], []
    for _ in range(TOP_K):
        mx = jnp.max(work, axis=-1, keepdims=True)
        first = jnp.min(jnp.where(work == mx, lanef, float(LANES)), axis=-1, keepdims=True)
        vals.append(mx)
        ids.append(first)
        work = jnp.where(lanef == first, -jnp.inf, work)
    es = [jnp.exp(v - vals[0]) for v in vals]
    den = functools.reduce(lambda a, b: a + b, es)
    idx_out = jnp.zeros(logits.shape, F32)
    tw_out = jnp.zeros(logits.shape, F32)
    for k in range(TOP_K):
        idx_out = jnp.where(lane == k, ids[k], idx_out)
        tw_out = jnp.where(lane == k, es[k] / den, tw_out)
    idx_ref[...] = idx_out.astype(I32)
    tw_ref[...] = tw_out
    half = h.shape[1] // 2
    lo = lax.bitcast_convert_type(h[:, :half].astype(BF16).astype(F32), U32)
    hi = lax.bitcast_convert_type(h[:, half:].astype(BF16).astype(F32), U32)
    hp_ref[...] = (lo >> 16) | (hi & jnp.uint32(0xFFFF0000))


def _router(x1, gain, wr, br):
    t, d = x1.shape
    return pl.pallas_call(
        _router_kernel,
        out_shape=(jax.ShapeDtypeStruct((t, d // 2), U32),
                   jax.ShapeDtypeStruct((t, LANES), I32),
                   jax.ShapeDtypeStruct((t, LANES), F32)),
        grid=(t // ROUTER_TM,),
        in_specs=[
            pl.BlockSpec((ROUTER_TM, d), lambda i: (i, 0)),
            pl.BlockSpec((1, d), lambda i: (0, 0)),
            pl.BlockSpec((d, LANES), lambda i: (0, 0)),
            pl.BlockSpec((1, LANES), lambda i: (0, 0)),
        ],
        out_specs=(pl.BlockSpec((ROUTER_TM, d // 2), lambda i: (i, 0)),
                   pl.BlockSpec((ROUTER_TM, LANES), lambda i: (i, 0)),
                   pl.BlockSpec((ROUTER_TM, LANES), lambda i: (i, 0))),
        compiler_params=_params("parallel"),
        name="moe_router",
    )(x1, gain, wr, br)


def _rank_kernel(idx_ref, rank_ref, cnt_ref, carry_scr):
    @pl.when(pl.program_id(0) == 0)
    def _():
        carry_scr[...] = jnp.zeros(carry_scr.shape, F32)

    idx = idx_ref[...]
    lane = lax.broadcasted_iota(I32, idx.shape, 1)
    hits = [lane == idx[:, k:k + 1] for k in range(TOP_K)]
    onehot = functools.reduce(lambda a, b: a + b, [jnp.where(h, 1.0, 0.0) for h in hits])
    ri = lax.broadcasted_iota(I32, (RANK_TM, RANK_TM), 0)
    ci = lax.broadcasted_iota(I32, (RANK_TM, RANK_TM), 1)
    before = jnp.where(ci < ri, 1.0, 0.0).astype(BF16)
    rank = _dot(before, onehot.astype(BF16)) + carry_scr[0:1, :]
    out = jnp.zeros(idx.shape, F32)
    for k in range(TOP_K):
        mine = jnp.sum(jnp.where(hits[k], rank, 0.0), axis=-1, keepdims=True)
        out = jnp.where(lane == k, mine, out)
    rank_ref[...] = out.astype(I32)
    carry = carry_scr[...] + jnp.sum(onehot, axis=0, keepdims=True)
    carry_scr[...] = carry
    cnt_ref[...] = carry


def _rank(idx):
    t = idx.shape[0]
    return pl.pallas_call(
        _rank_kernel,
        out_shape=(jax.ShapeDtypeStruct((t, LANES), I32), jax.ShapeDtypeStruct((8, LANES), F32)),
        grid=(t // RANK_TM,),
        in_specs=[pl.BlockSpec((RANK_TM, LANES), lambda i: (i, 0))],
        out_specs=(pl.BlockSpec((RANK_TM, LANES), lambda i: (i, 0)),
                   pl.BlockSpec((8, LANES), lambda i: (0, 0))),
        scratch_shapes=[pltpu.VMEM((8, LANES), F32)],
        compiler_params=_params("arbitrary"),
        name="moe_rank",
    )(idx)


def _pos_kernel(cnt_ref, idx_ref, rank_ref, pos_ref, vt_ref):
    cnt = cnt_ref[0:1, :]
    lane1 = lax.broadcasted_iota(I32, (1, LANES), 1)
    tiles = jnp.where(lane1 < N_EXPERTS, jnp.floor((cnt + (MOE_TM - 1)) * (1.0 / MOE_TM)), 0.0)
    cum_excl = jnp.zeros((1, LANES), F32)
    for e in range(N_EXPERTS):
        cum_excl = cum_excl + jnp.where(lane1 > e, tiles[:, e:e + 1], 0.0)
    cum_incl = cum_excl + tiles
    start = cum_excl * MOE_TM

    idx = idx_ref[...]
    rank = rank_ref[...].astype(F32)
    lane = lax.broadcasted_iota(I32, idx.shape, 1)
    pos = jnp.zeros(idx.shape, F32)
    for k in range(TOP_K):
        base = jnp.sum(jnp.where(lane == idx[:, k:k + 1], start, 0.0), axis=-1, keepdims=True)
        pos = jnp.where(lane == k, base + rank[:, k:k + 1], pos)
    pos_ref[...] = pos.astype(I32)

    nv = vt_ref.shape[0]
    vl = lax.broadcasted_iota(I32, (nv, LANES), 1)
    v = lax.broadcasted_iota(I32, (nv, LANES), 0).astype(F32)
    total = jnp.sum(tiles, axis=-1, keepdims=True)
    vv = jnp.minimum(v, total - 1.0)
    done = jnp.where((cum_incl <= vv) & (vl < N_EXPERTS), 1.0, 0.0)
    e_v = jnp.minimum(jnp.sum(done, axis=-1, keepdims=True), float(N_EXPERTS - 1))
    mine = vl.astype(F32) == e_v
    first_tile = jnp.sum(jnp.where(mine, cum_excl, 0.0), axis=-1, keepdims=True)
    e_cnt = jnp.sum(jnp.where(mine, cnt, 0.0), axis=-1, keepdims=True)
    vv1 = vv[:, 0:1]
    left = jnp.clip(e_cnt - (vv1 - first_tile) * MOE_TM, 0.0, float(MOE_TM))
    valid = v[:, 0:1] < total
    nrows = jnp.where(valid, jnp.ceil(left * (1.0 / MOE_SUB)) * MOE_SUB, 0.0)
    table = jnp.where(vl == 0, e_v, jnp.where(vl == 1, vv1, jnp.where(vl == 2, nrows,
                      jnp.where(vl == 3, jnp.where(valid, 1.0, 0.0), 0.0))))
    vt_ref[...] = table.astype(I32)


def _positions(cnt, idx, rank, n_visits):
    t = idx.shape[0]
    nvp = -(-n_visits // 8) * 8
    return pl.pallas_call(
        _pos_kernel,
        out_shape=(jax.ShapeDtypeStruct((t, LANES), I32), jax.ShapeDtypeStruct((nvp, LANES), I32)),
        grid=(t // POS_TM,),
        in_specs=[pl.BlockSpec((8, LANES), lambda i: (0, 0)),
                  pl.BlockSpec((POS_TM, LANES), lambda i: (i, 0)),
                  pl.BlockSpec((POS_TM, LANES), lambda i: (i, 0))],
        out_specs=(pl.BlockSpec((POS_TM, LANES), lambda i: (i, 0)),
                   pl.BlockSpec((nvp, LANES), lambda i: (0, 0))),
        compiler_params=_params("arbitrary"),
        name="moe_positions",
    )(cnt, idx, rank)


def _dispatch_kernel(pos_ref, hp_ref, xs_in_ref, xs_ref, sem):
    del xs_in_ref
    base = pl.program_id(0) * DISPATCH_TM

    def row_copy(r, dst):
        return pltpu.make_async_copy(hp_ref.at[pl.ds(r, 1)], xs_ref.at[pl.ds(dst, 1)], sem)

    def issue(r, carry):
        for k in range(TOP_K):
            row_copy(r, pos_ref[(base + r) * TOP_K + k]).start()
        return carry

    def drain(r, carry):
        for k in range(TOP_K):
            row_copy(r, pos_ref[(base + r) * TOP_K + k]).wait()
        return carry

    lax.fori_loop(0, DISPATCH_TM, issue, 0)
    lax.fori_loop(0, DISPATCH_TM, drain, 0)


def _dispatch(pos_flat, hp, xs_init):
    t, half = hp.shape
    return pl.pallas_call(
        _dispatch_kernel,
        out_shape=jax.ShapeDtypeStruct(xs_init.shape, U32),
        grid_spec=pltpu.PrefetchScalarGridSpec(
            num_scalar_prefetch=1,
            grid=(t // DISPATCH_TM,),
            in_specs=[pl.BlockSpec((DISPATCH_TM, half), lambda i, pos: (i, 0)),
                      pl.BlockSpec(memory_space=pl.ANY)],
            out_specs=pl.BlockSpec(memory_space=pl.ANY),
            scratch_shapes=[pltpu.SemaphoreType.DMA],
        ),
        input_output_aliases={2: 0},
        compiler_params=_params("arbitrary"),
        name="moe_dispatch",
    )(pos_flat, hp, xs_init)


def _moe_up_kernel(vt_ref, xs_ref, wg_ref, wu_ref, bg_ref, bu_ref, h_ref, xb_scr, wg_scr, wu_scr):
    v = pl.program_id(0)
    nrows = vt_ref[v * 4 + 2]
    half = xs_ref.shape[1]

    def sub_blocks(live_fn, dead_fn=None):
        for sb in range(MOE_TM // MOE_SUB):
            rs = slice(sb * MOE_SUB, (sb + 1) * MOE_SUB)
            pl.when(sb * MOE_SUB < nrows)(functools.partial(live_fn, rs))
            if dead_fn is not None:
                pl.when(sb * MOE_SUB >= nrows)(functools.partial(dead_fn, rs))

    @pl.when(pl.program_id(1) == 0)
    def _():
        def unpack(rs):
            w = xs_ref[rs, :]
            lo = lax.bitcast_convert_type(w << 16, F32)
            hi = lax.bitcast_convert_type(w & jnp.uint32(0xFFFF0000), F32)
            xb_scr[rs, :half] = lo.astype(BF16)
            xb_scr[rs, half:] = hi.astype(BF16)
        sub_blocks(unpack)

    @pl.when(nrows > 0)
    def _():
        wg_scr[...] = wg_ref[...].astype(BF16)
        wu_scr[...] = wu_ref[...].astype(BF16)

    def compute(rs):
        x = xb_scr[rs, :]
        gate = jnp.minimum(_dot(x, wg_scr[...]) + bg_ref[...], SWIGLU_LIMIT)
        up = jnp.clip(_dot(x, wu_scr[...]) + bu_ref[...], -SWIGLU_LIMIT, SWIGLU_LIMIT)
        act = (up + 1.0) * (gate * _sigmoid(SWIGLU_ALPHA * gate))
        h_ref[rs, :] = act.astype(h_ref.dtype)

    def clear(rs):
        h_ref[rs, :] = jnp.zeros((MOE_SUB, h_ref.shape[1]), h_ref.dtype)

    sub_blocks(compute, clear)


def _moe_up(vt, xs, w_gate_up, b_gate_up, n_visits):
    half = xs.shape[1]
    d = w_gate_up.shape[1]
    nf = D_EXPERT // MOE_TF

    def fsel(v, f, vt):
        return jnp.where(vt[v * 4 + 3] > 0, f, nf - 1)

    return pl.pallas_call(
        _moe_up_kernel,
        out_shape=jax.ShapeDtypeStruct((n_visits * MOE_TM, D_EXPERT), BF16),
        grid_spec=pltpu.PrefetchScalarGridSpec(
            num_scalar_prefetch=1,
            grid=(n_visits, nf),
            in_specs=[
                pl.BlockSpec((MOE_TM, half), lambda v, f, vt: (vt[v * 4 + 1], 0)),
                pl.BlockSpec((None, d, MOE_TF), lambda v, f, vt: (vt[v * 4], 0, fsel(v, f, vt))),
                pl.BlockSpec((None, d, MOE_TF), lambda v, f, vt: (vt[v * 4], 0, nf + fsel(v, f, vt))),
                pl.BlockSpec((None, 1, MOE_TF), lambda v, f, vt: (vt[v * 4], 0, fsel(v, f, vt))),
                pl.BlockSpec((None, 1, MOE_TF), lambda v, f, vt: (vt[v * 4], 0, nf + fsel(v, f, vt))),
            ],
            out_specs=pl.BlockSpec((MOE_TM, MOE_TF), lambda v, f, vt: (v, f)),
            scratch_shapes=[pltpu.VMEM((MOE_TM, d), BF16),
                            pltpu.VMEM((d, MOE_TF), BF16),
                            pltpu.VMEM((d, MOE_TF), BF16)],
        ),
        compiler_params=_params("arbitrary", "arbitrary"),
        name="moe_up",
    )(vt, xs, w_gate_up, w_gate_up, b_gate_up, b_gate_up)


def _moe_down_kernel(vt_ref, h_ref, wd_ref, bd_ref, y_ref, wd_scr):
    nrows = vt_ref[pl.program_id(0) * 4 + 2]

    @pl.when(nrows > 0)
    def _():
        wd_scr[...] = wd_ref[...].astype(BF16)

    for sb in range(MOE_TM // MOE_SUB):
        rs = slice(sb * MOE_SUB, (sb + 1) * MOE_SUB)

        @pl.when(sb * MOE_SUB < nrows)
        def _():
            y_ref[rs, :] = _dot(h_ref[rs, :], wd_scr[...]) + bd_ref[...]

        @pl.when(sb * MOE_SUB >= nrows)
        def _():
            y_ref[rs, :] = jnp.zeros((MOE_SUB, y_ref.shape[1]), y_ref.dtype)


def _moe_down(vt, h, w_down, b_down, n_visits):
    d = w_down.shape[2]
    nj = d // MOE_TN

    def jsel(v, j, vt):
        return jnp.where(vt[v * 4 + 3] > 0, j, nj - 1)

    return pl.pallas_call(
        _moe_down_kernel,
        out_shape=jax.ShapeDtypeStruct((n_visits * MOE_TM, d), F32),
        grid_spec=pltpu.PrefetchScalarGridSpec(
            num_scalar_prefetch=1,
            grid=(n_visits, nj),
            in_specs=[
                pl.BlockSpec((MOE_TM, D_EXPERT), lambda v, j, vt: (vt[v * 4 + 1], 0)),
                pl.BlockSpec((None, D_EXPERT, MOE_TN), lambda v, j, vt: (vt[v * 4], 0, jsel(v, j, vt))),
                pl.BlockSpec((None, 1, MOE_TN), lambda v, j, vt: (vt[v * 4], 0, jsel(v, j, vt))),
            ],
            out_specs=pl.BlockSpec((MOE_TM, MOE_TN), lambda v, j, vt: (v, j)),
            scratch_shapes=[pltpu.VMEM((D_EXPERT, MOE_TN), BF16)],
        ),
        compiler_params=_params("arbitrary", "arbitrary"),
        name="moe_down",
    )(vt, h, w_down, b_down)


def _combine_kernel(pos_ref, ys_ref, x_ref, tw_ref, o_ref, buf, sem):
    base = pl.program_id(0) * COMBINE_TM

    def row_copy(r, k):
        src = pos_ref[(base + r) * TOP_K + k]
        return pltpu.make_async_copy(ys_ref.at[pl.ds(src, 1)], buf.at[k, pl.ds(r, 1)], sem)

    def issue(r, carry):
        for k in range(TOP_K):
            row_copy(r, k).start()
        return carry

    def drain(r, carry):
        for k in range(TOP_K):
            row_copy(r, k).wait()
        return carry

    lax.fori_loop(0, COMBINE_TM, issue, 0)
    lax.fori_loop(0, COMBINE_TM, drain, 0)
    tw = tw_ref[...]
    acc = x_ref[...]
    for k in range(TOP_K):
        acc = acc + tw[:, k:k + 1] * buf[k]
    o_ref[...] = acc


def _combine(pos_flat, ys, x1, tw):
    t, d = x1.shape
    return pl.pallas_call(
        _combine_kernel,
        out_shape=jax.ShapeDtypeStruct((t, d), F32),
        grid_spec=pltpu.PrefetchScalarGridSpec(
            num_scalar_prefetch=1,
            grid=(t // COMBINE_TM,),
            in_specs=[pl.BlockSpec(memory_space=pl.ANY),
                      pl.BlockSpec((COMBINE_TM, d), lambda i, pos: (i, 0)),
                      pl.BlockSpec((COMBINE_TM, LANES), lambda i, pos: (i, 0))],
            out_specs=pl.BlockSpec((COMBINE_TM, d), lambda i, pos: (i, 0)),
            scratch_shapes=[pltpu.VMEM((TOP_K, COMBINE_TM, d), F32), pltpu.SemaphoreType.DMA],
        ),
        compiler_params=_params("arbitrary"),
        name="moe_combine",
    )(pos_flat, ys, x1, tw)


def _alibi_slopes():
    n = N_NSA_HEADS + N_DIL_HEADS
    i = jnp.arange(1, n + 1, dtype=F32)
    return jnp.exp2(-8.0 * i / n)


def _attention_block(x2, batch, seq, attn_norm, w_in, pe_k, w_k1, w_k2, pe_v, w_v1, w_v2,
                     qn_nsa, kn_cmp, kn_slc, kn_win, qn_dil, kn_dil, on_nsa, on_dil, w_out):
    d = x2.shape[1]
    a_dim = Q_NSA_DIM + 6 * KV_NSA_DIM
    d_off = a_dim + GATE_DIM
    n_pad = N_PROJ - (a_dim + 3 * DIL_DIM + GATE_DIM)
    w = jnp.concatenate([w_in[:, :a_dim], w_in[:, d_off:], w_in[:, a_dim:d_off],
                         jnp.zeros((d, n_pad), w_in.dtype)], axis=1).astype(BF16)
    ones_kv = jnp.ones((KV_NSA_DIM,), F32)
    ones_dil = jnp.ones((DIL_DIM,), F32)
    tail = jnp.ones((GATE_DIM + n_pad,), F32)
    col_gain = jnp.concatenate([
        jnp.tile(qn_nsa, N_NSA_HEADS), ones_kv, ones_kv, jnp.tile(kn_slc, N_NSA_KV), ones_kv,
        jnp.tile(kn_win, N_NSA_KV), ones_kv, jnp.tile(qn_dil, N_DIL_HEADS), jnp.tile(kn_dil, N_DIL_HEADS),
        ones_dil, tail])[None, :]
    col_flag = jnp.concatenate([
        jnp.ones((Q_NSA_DIM,), F32), 0 * ones_kv, 0 * ones_kv, ones_kv, 0 * ones_kv, ones_kv, 0 * ones_kv,
        ones_dil, ones_dil, 0 * ones_dil, 0 * tail])[None, :]
    proj = _in_proj(x2, attn_norm[None, :], w, col_gain, col_flag)

    n_chunks = seq // CMP_STRIDE
    kv = proj[:, CB_K_CMP * LANES:(CB_V_CMP + N_NSA_KV) * LANES]
    kv = kv.reshape(batch, n_chunks, CMP_STRIDE, 2, N_NSA_KV, HEAD_DIM).transpose(3, 0, 4, 1, 2, 5)
    chunks = kv.reshape(2, batch * N_NSA_KV, n_chunks, CMP_STRIDE * HEAD_DIM)
    assert CMP_BLOCK == 2 * CMP_STRIDE
    blocks = jnp.concatenate([chunks[:, :, :-1], chunks[:, :, 1:]], axis=-1)
    blocks = jnp.pad(blocks, ((0, 0), (0, 0), (0, 1), (0, 0)))
    pe = jnp.stack([pe_k.reshape(1, -1), pe_v.reshape(1, -1)])
    w1 = jnp.stack([w_k1, w_v1]).astype(BF16)
    w2 = jnp.stack([w_k2, w_v2]).astype(BF16)
    cmp_kv = _compress(blocks, pe, w1, w2, kn_cmp[None, :])

    slopes = _alibi_slopes()
    gates = proj[:, CB_GATE * LANES:CB_GATE * LANES + GATE_DIM]
    gates_t = gates.reshape(-1, N_NSA_KV, 3 * NSA_REP).transpose(1, 0, 2)
    o_nsa = _nsa_mixer(slopes[0::2], proj, cmp_kv, gates_t, batch, seq)
    o_dil = _dil_mixer(slopes[1::2], proj, batch, seq)
    return _out_proj(o_nsa, o_dil, on_nsa[None, :], on_dil[None, :], w_out.astype(BF16), x2)


def _moe_block(x1, ffn_norm, w_router, b_router, w_gate_up, b_gate_up, w_down, b_down):
    t, d = x1.shape
    wr = jnp.pad(w_router, ((0, 0), (0, LANES - N_EXPERTS)))
    br = jnp.pad(b_router, (0, LANES - N_EXPERTS))[None, :]
    hp, idx, tw = _router(x1, ffn_norm[None, :], wr, br)
    rank, cnt = _rank(idx)
    n_visits = N_EXPERTS + (t * TOP_K) // MOE_TM
    pos, table = _positions(cnt, idx, rank, n_visits)
    pos_flat = pos[:, :TOP_K].reshape(-1)
    vt = table[:n_visits, :4].reshape(-1)
    xs = _dispatch(pos_flat, hp, jnp.zeros((n_visits * MOE_TM, d // 2), U32))
    h = _moe_up(vt, xs, w_gate_up, b_gate_up[:, None, :], n_visits)
    ys = _moe_down(vt, h, w_down, b_down[:, None, :], n_visits)
    return _combine(pos_flat, ys, x1, tw)


def kernel(x, attn_norm, w_in, cmp_pos_k, w_cmp_k1, w_cmp_k2, cmp_pos_v, w_cmp_v1, w_cmp_v2, q_norm_nsa, k_norm_cmp, k_norm_slc, k_norm_win, q_norm_dil, k_norm_dil, out_norm_nsa, out_norm_dil, w_out, ffn_norm, w_router, b_router, w_gate_up, b_gate_up, w_down, b_down):
    batch, seq, d = x.shape
    x2 = x.reshape(batch * seq, d)
    for layer in range(attn_norm.shape[0]):
        x2 = _attention_block(
            x2, batch, seq, attn_norm[layer], w_in[layer], cmp_pos_k[layer], w_cmp_k1[layer],
            w_cmp_k2[layer], cmp_pos_v[layer], w_cmp_v1[layer], w_cmp_v2[layer], q_norm_nsa[layer],
            k_norm_cmp[layer], k_norm_slc[layer], k_norm_win[layer], q_norm_dil[layer],
            k_norm_dil[layer], out_norm_nsa[layer], out_norm_dil[layer], w_out[layer])
        x2 = _moe_block(x2, ffn_norm[layer], w_router[layer], b_router[layer], w_gate_up[layer],
                        b_gate_up[layer], w_down[layer], b_down[layer])
    return x2.reshape(batch, seq, d)
```

```python
import functools

import numpy as np
import jax
import jax.numpy as jnp
from jax import lax
from jax.experimental import pallas as pl
from jax.experimental.pallas import tpu as pltpu

F32, BF16, I32 = jnp.float32, jnp.bfloat16, jnp.int32

HEAD_DIM = 128
N_NSA_HEADS = 16
N_NSA_KV = 4
NSA_REP = N_NSA_HEADS // N_NSA_KV
N_DIL_HEADS = 16
CMP_BLOCK = 32
CMP_STRIDE = 16
CMP_HIDDEN = 256
SLC_BLOCK = 64
SLC_TOP_N = 16
FORCED_SCORE = 1.0e4
WIN_SIZE = 512
DIL_PATTERNS = ((128, 1), (512, 4), (2048, 16))
N_EXPERTS = 32
TOP_K = 4
D_EXPERT = 1536
SWIGLU_ALPHA = 1.702
SWIGLU_LIMIT = 7.0
RMS_EPS = 1e-6
NEG_INF = -1e30
ATTN_SCALE = HEAD_DIM ** -0.5
LOG2E = 1.4426950408889634

Q_NSA_DIM = N_NSA_HEADS * HEAD_DIM
KV_NSA_DIM = N_NSA_KV * HEAD_DIM
GATE_DIM = N_NSA_HEADS * 3
DIL_DIM = N_DIL_HEADS * HEAD_DIM

LANES = 128
SUBLANES = 8
VMEM_LIMIT = 56 * 1024 * 1024

CB_Q_NSA = 0
CB_K_CMP = CB_Q_NSA + N_NSA_HEADS
CB_V_CMP = CB_K_CMP + N_NSA_KV
CB_K_SLC = CB_V_CMP + N_NSA_KV
CB_V_SLC = CB_K_SLC + N_NSA_KV
CB_K_WIN = CB_V_SLC + N_NSA_KV
CB_V_WIN = CB_K_WIN + N_NSA_KV
CB_Q_DIL = CB_V_WIN + N_NSA_KV
CB_K_DIL = CB_Q_DIL + N_DIL_HEADS
CB_V_DIL = CB_K_DIL + N_DIL_HEADS
CB_GATE = CB_V_DIL + N_DIL_HEADS
PROJ_TN = 512
N_PROJ = -(-(CB_GATE + 1) * LANES // PROJ_TN) * PROJ_TN

PROJ_TM = 512
NSA_TQ = 128
DIL_TQ = 512
ATT_TK = 256
ROUTER_TM = 256
RANK_TM = 512
POS_TM = 1024
MOE_TM = 1024
MOE_SUB = 256
MOE_TF = 256
MOE_TN = 512
DISPATCH_TM = 256
COMBINE_TM = 128
VT_W = 8

POS_SHIFT = 8
POS_RADIX = 1 << POS_SHIFT
ALIBI_PIECES = 3
ALIBI_COLS = 2 * ALIBI_PIECES
SEL_COL0 = 64
MASK_BIG = 2.0 ** 100


def _dot(a, b, **kw):
    return jnp.dot(a, b, preferred_element_type=F32, **kw)


def _dot_nt(a, b, **kw):
    return lax.dot_general(a, b, (((1,), (1,)), ((), ())), preferred_element_type=F32, **kw)


def _rms_rows(x, gain):
    ms = jnp.mean(x * x, axis=-1, keepdims=True)
    return x * lax.rsqrt(ms + RMS_EPS) * gain


def _sigmoid(x):
    return 1.0 / (1.0 + jnp.exp(-x))


def _params(*sem):
    return pltpu.CompilerParams(dimension_semantics=sem, vmem_limit_bytes=VMEM_LIMIT)


def _in_proj_kernel(x_ref, g_ref, w_ref, cg_ref, cf_ref, o_ref, h_scr):
    @pl.when(pl.program_id(1) == 0)
    def _():
        h_scr[...] = _rms_rows(x_ref[...], g_ref[...]).astype(BF16)

    acc = _dot(h_scr[...], w_ref[...])
    for c in range(PROJ_TN // LANES):
        cs = slice(c * LANES, (c + 1) * LANES)
        a = acc[:, cs]
        r = lax.rsqrt(jnp.mean(a * a, axis=-1, keepdims=True) + RMS_EPS)
        y = a * jnp.where(cf_ref[:, cs] > 0, r, 1.0) * cg_ref[:, cs]
        o_ref[:, cs] = y.astype(o_ref.dtype)


def _in_proj(x2, gain, w, col_gain, col_flag):
    t, d = x2.shape
    n = w.shape[1]
    return pl.pallas_call(
        _in_proj_kernel,
        out_shape=jax.ShapeDtypeStruct((t, n), BF16),
        grid=(t // PROJ_TM, n // PROJ_TN),
        in_specs=[
            pl.BlockSpec((PROJ_TM, d), lambda i, j: (i, 0)),
            pl.BlockSpec((1, d), lambda i, j: (0, 0)),
            pl.BlockSpec((d, PROJ_TN), lambda i, j: (0, j)),
            pl.BlockSpec((1, PROJ_TN), lambda i, j: (0, j)),
            pl.BlockSpec((1, PROJ_TN), lambda i, j: (0, j)),
        ],
        out_specs=pl.BlockSpec((PROJ_TM, PROJ_TN), lambda i, j: (i, j)),
        scratch_shapes=[pltpu.VMEM((PROJ_TM, d), BF16)],
        compiler_params=_params("parallel", "arbitrary"),
        name="in_proj",
    )(x2, gain, w, col_gain, col_flag)


def _compress_kernel(a_ref, pe_ref, w1_ref, w2_ref, kn_ref, o_ref):
    a = (a_ref[...].astype(F32) + pe_ref[...]).astype(BF16)
    hid = _dot(a, w1_ref[...])
    hid = hid * _sigmoid(hid)
    out = _dot(hid.astype(BF16), w2_ref[...])
    normed = _rms_rows(out, kn_ref[...])
    o_ref[...] = jnp.where(pl.program_id(0) == 0, normed, out).astype(o_ref.dtype)


def _compress(blocks, pe, w1, w2, kn):
    two, bg, ncp, flat = blocks.shape
    return pl.pallas_call(
        _compress_kernel,
        out_shape=jax.ShapeDtypeStruct((two, bg, ncp, HEAD_DIM), BF16),
        grid=(two, bg),
        in_specs=[
            pl.BlockSpec((None, None, ncp, flat), lambda s, i: (s, i, 0, 0)),
            pl.BlockSpec((None, 1, flat), lambda s, i: (s, 0, 0)),
            pl.BlockSpec((None, flat, CMP_HIDDEN), lambda s, i: (s, 0, 0)),
            pl.BlockSpec((None, CMP_HIDDEN, HEAD_DIM), lambda s, i: (s, 0, 0)),
            pl.BlockSpec((1, HEAD_DIM), lambda s, i: (0, 0)),
        ],
        out_specs=pl.BlockSpec((None, None, ncp, HEAD_DIM), lambda s, i: (s, i, 0, 0)),
        compiler_params=_params("arbitrary", "arbitrary"),
        name="nsa_compress",
    )(blocks, pe, w1, w2, kn)


def _split_position(v):
    hi = np.floor_divide(v, POS_RADIX)
    return hi, v - POS_RADIX * hi


def _rel_position_table(seq):
    hi, lo = _split_position(np.arange(2 * seq) - seq)
    tab = np.zeros((2 * seq, LANES), np.float32)
    tab[:, 0:ALIBI_COLS:2] = hi[:, None]
    tab[:, 1:ALIBI_COLS:2] = lo[:, None]
    return jnp.asarray(tab, BF16)


def _block_onehot_table(seq):
    tab = np.zeros((seq, LANES), np.float32)
    tab[np.arange(seq), SEL_COL0 + np.arange(seq) // SLC_BLOCK] = 1.0
    return jnp.asarray(tab, BF16)


def _alibi_rows(slopes):
    rest = slopes.astype(F32) * LOG2E
    cols = []
    for _ in range(ALIBI_PIECES):
        piece = rest.astype(BF16).astype(F32)
        rest = rest - piece
        cols += [piece * POS_RADIX, piece]
    rows = jnp.concatenate([jnp.stack(cols, axis=1),
                            jnp.zeros((slopes.shape[0], LANES - ALIBI_COLS), F32)], axis=1)
    return jnp.broadcast_to(rows[:, None, :], (slopes.shape[0], SUBLANES, LANES))


def _tile_distance(u):
    return LANES * u + np.arange(LANES)[:, None] - np.arange(ATT_TK)[None, :]


def _mask_table(n, valid_fn):
    tabs = [np.where(valid_fn(_tile_distance(u)), 0.0, -MASK_BIG) for u in range(n)]
    return jnp.asarray(np.stack(tabs), F32)


def _dilation_table():
    reach = max(w for w, _ in DIL_PATTERNS)
    n = (reach + ATT_TK - 1) // LANES + 1
    tabs = []
    for u in range(n):
        d = _tile_distance(u)
        mult = sum(((d >= 0) & (d <= w) & (d % dil == 0)).astype(np.float32) for w, dil in DIL_PATTERNS)
        tabs.append(np.where(mult > 0, np.log2(np.maximum(mult, 1.0)), -MASK_BIG))
    tabs.append(np.full((LANES, ATT_TK), -MASK_BIG))
    return jnp.asarray(np.stack(tabs), F32)


def _softmax_tile(s_scr, p_scr, m_scr, l_scr, acc_scr, v, bias_for_block):
    for rb in range(s_scr.shape[0] // LANES):
        rs = slice(rb * LANES, (rb + 1) * LANES)
        s = s_scr[rs, :]
        bias = bias_for_block(rb)
        if bias is not None:
            s = s + bias
        m_old = m_scr[rs, :]
        m_new = jnp.maximum(m_old, jnp.max(s, axis=-1, keepdims=True))
        alpha = jnp.exp2(m_old - m_new)
        p = [jnp.exp2(s[:, c * LANES:(c + 1) * LANES] - m_new) for c in range(ATT_TK // LANES)]
        total = functools.reduce(lambda a, b: a + b, p)
        l_scr[rs, :] = alpha * l_scr[rs, :] + jnp.sum(total, axis=-1, keepdims=True)
        acc_scr[rs, :] = alpha * acc_scr[rs, :]
        m_scr[rs, :] = m_new
        for c in range(ATT_TK // LANES):
            p_scr[rs, c * LANES:(c + 1) * LANES] = p[c].astype(BF16)
    acc_scr[...] += _dot(p_scr[...], v)


def _softmax_init(m_scr, l_scr, acc_scr):
    m_scr[...] = jnp.full(m_scr.shape, NEG_INF, F32)
    l_scr[...] = jnp.zeros(l_scr.shape, F32)
    acc_scr[...] = jnp.zeros(acc_scr.shape, F32)


def _nsa_kernel(q_ref, qx_ref, kc_ref, vc_ref, ks_ref, vs_ref, kw_ref, vw_ref, rel_ref, oh_ref,
                wtab_ref, ctab_ref, gt_ref, o_ref,
                qa_scr, s_scr, p_scr, m_scr, l_scr, acc_scr, ob_scr, sc_scr, *, seq):
    t0 = pl.program_id(2) * NSA_TQ
    n_slc = seq // SLC_BLOCK
    heads = [slice(r * NSA_TQ, (r + 1) * NSA_TQ) for r in range(NSA_REP)]
    alibi = [jnp.broadcast_to(qx_ref[r, 0:1, :], (NSA_TQ, LANES)) for r in range(NSA_REP)]
    for r in range(NSA_REP):
        qa_scr[heads[r], :HEAD_DIM] = q_ref[:, r * HEAD_DIM:(r + 1) * HEAD_DIM]
        qa_scr[heads[r], HEAD_DIM:] = alibi[r].astype(BF16)

    ncp = kc_ref.shape[0]
    n_i = lax.broadcasted_iota(I32, (ncp, LANES), 0)
    ln = lax.broadcasted_iota(I32, (ncp, LANES), 1)
    rel_end = n_i * CMP_STRIDE + (CMP_BLOCK - 1) - t0
    digits = jnp.where((ln & 1) == 0, rel_end >> POS_SHIFT, rel_end & (POS_RADIX - 1))
    feat = jnp.where(ln < ALIBI_COLS, digits, 0).astype(F32).astype(BF16)
    kc_aug = jnp.concatenate([kc_ref[...], feat], axis=1)
    row = lax.broadcasted_iota(I32, (NSA_TQ, 1), 0) + t0
    blk_end = lax.broadcasted_iota(I32, (NSA_TQ, ncp), 1) * CMP_STRIDE + (CMP_BLOCK - 1)
    valid_c = row >= blk_end
    vc = vc_ref[...]
    psum = jnp.zeros((NSA_TQ, ncp), F32)
    for r in range(NSA_REP):
        s = jnp.where(valid_c, _dot_nt(qa_scr[heads[r], :], kc_aug), NEG_INF)
        mc = jnp.max(s, axis=-1, keepdims=True)
        pc = jnp.where(valid_c, jnp.exp2(s - mc), 0.0)
        p = pc / jnp.maximum(jnp.sum(pc, axis=-1, keepdims=True), 1e-30)
        psum = psum + p
        ob_scr[0, heads[r], :] = _dot(p.astype(BF16), vc)

    per_slc = SLC_BLOCK // CMP_STRIDE
    back = CMP_BLOCK // CMP_STRIDE - 1
    jb = lax.broadcasted_iota(I32, (LANES, ncp), 0)
    nb = lax.broadcasted_iota(I32, (LANES, ncp), 1)
    overlap_t = jnp.where((nb >= per_slc * jb - back) & (nb < per_slc * (jb + 1)), 1.0, 0.0)
    imp_t = _dot_nt(overlap_t, psum, precision=lax.Precision.HIGHEST)
    j = lax.broadcasted_iota(I32, (LANES, NSA_TQ), 0)
    tq = lax.broadcasted_iota(I32, (LANES, NSA_TQ), 1) + t0
    cur = tq // SLC_BLOCK
    forced = (j == 0) | (j == cur) | (j == cur - 1)
    sc_scr[...] = jnp.where(forced, FORCED_SCORE, jnp.where(j * SLC_BLOCK <= tq, imp_t, -1.0))

    n_slab = n_slc // SUBLANES
    slabs = [sc_scr[SUBLANES * v:SUBLANES * (v + 1), :] for v in range(n_slab)]
    ranks = [jnp.zeros((SUBLANES, NSA_TQ), F32) for _ in range(n_slab)]
    sub = lax.broadcasted_iota(I32, (SUBLANES, NSA_TQ), 0)
    for i in range(n_slc):
        rival = jnp.broadcast_to(sc_scr[i:i + 1, :], (SUBLANES, NSA_TQ))
        for v in range(n_slab):
            wins_ties = jnp.where(rival >= slabs[v], 1.0, 0.0)
            loses_ties = jnp.where(rival > slabs[v], 1.0, 0.0)
            if SUBLANES * v > i:
                ranks[v] = ranks[v] + wins_ties
            elif SUBLANES * (v + 1) <= i:
                ranks[v] = ranks[v] + loses_ties
            else:
                ranks[v] = ranks[v] + jnp.where(sub + SUBLANES * v > i, wins_ties, loses_ties)
    n_top = min(SLC_TOP_N, n_slc)
    pen_rows = [jnp.zeros((SEL_COL0, NSA_TQ), F32)]
    pen_rows += [jnp.where(rk < n_top, 0.0, -MASK_BIG) for rk in ranks]
    if SEL_COL0 + n_slc < LANES:
        pen_rows.append(jnp.zeros((LANES - SEL_COL0 - n_slc, NSA_TQ), F32))
    penalty = jnp.concatenate(pen_rows, axis=0).T
    for r in range(NSA_REP):
        qa_scr[heads[r], HEAD_DIM:] = (alibi[r] + penalty).astype(BF16)

    def key_tile(kt, k_ref, with_blocks):
        k0 = pl.multiple_of(kt * ATT_TK, ATT_TK)
        r0 = pl.multiple_of(k0 - t0 + seq, LANES)
        extra = rel_ref[pl.ds(r0, ATT_TK), :]
        if with_blocks:
            extra = extra + oh_ref[pl.ds(k0, ATT_TK), :]
        s_scr[...] = _dot_nt(qa_scr[...], jnp.concatenate([k_ref[pl.ds(k0, ATT_TK), :], extra], axis=1))
        return k0

    _softmax_init(m_scr, l_scr, acc_scr)
    last_kt = t0 // ATT_TK

    def slc_body(kt, carry):
        k0 = key_tile(kt, ks_ref, True)
        _softmax_tile(s_scr, p_scr, m_scr, l_scr, acc_scr, vs_ref[pl.ds(k0, ATT_TK), :], lambda rb: None)
        return carry

    lax.fori_loop(0, last_kt, slc_body, 0)
    k0 = key_tile(last_kt, ks_ref, True)
    causal = ctab_ref[(t0 - k0) // LANES]
    _softmax_tile(s_scr, p_scr, m_scr, l_scr, acc_scr, vs_ref[pl.ds(k0, ATT_TK), :], lambda rb: causal)
    ob_scr[1] = acc_scr[...] / l_scr[...]

    _softmax_init(m_scr, l_scr, acc_scr)

    def win_body(kt, carry):
        k0 = key_tile(kt, kw_ref, False)
        window = wtab_ref[(t0 - k0) // LANES]
        _softmax_tile(s_scr, p_scr, m_scr, l_scr, acc_scr, vw_ref[pl.ds(k0, ATT_TK), :], lambda rb: window)
        return carry

    first_kt = jnp.maximum(t0 - (WIN_SIZE - 1), 0) // ATT_TK
    lax.fori_loop(first_kt, last_kt + 1, win_body, 0)

    gate = _sigmoid(gt_ref[...].astype(F32))
    for r in range(NSA_REP):
        o_win = acc_scr[heads[r], :] / l_scr[heads[r], :]
        o = (gate[:, 3 * r:3 * r + 1] * ob_scr[0, heads[r], :] + gate[:, 3 * r + 1:3 * r + 2] * ob_scr[1, heads[r], :]
             + gate[:, 3 * r + 2:3 * r + 3] * o_win)
        o_ref[:, r * HEAD_DIM:(r + 1) * HEAD_DIM] = o.astype(o_ref.dtype)


def _nsa_mixer(slopes, proj, cmp_kv, gates_t, rel_tab, batch, seq):
    t = proj.shape[0]
    nq = seq // NSA_TQ
    ncp = cmp_kv.shape[2]
    n_slc = seq // SLC_BLOCK
    assert SEL_COL0 + n_slc <= LANES and n_slc % SUBLANES == 0 and seq % ATT_TK == 0
    rows = NSA_REP * NSA_TQ
    win_tab = _mask_table((WIN_SIZE - 1 + ATT_TK - 1) // LANES + 1, lambda d: (d >= 0) & (d <= WIN_SIZE - 1))
    causal_tab = _mask_table(ATT_TK // LANES, lambda d: d >= 0)

    def seq_spec(cb):
        return pl.BlockSpec((seq, HEAD_DIM), lambda b, g, i: (b, cb + g))

    def cmp_spec(which):
        return pl.BlockSpec((None, None, ncp, HEAD_DIM), lambda b, g, i: (which, b * N_NSA_KV + g, 0, 0))

    def whole(a):
        return pl.BlockSpec(a.shape, lambda b, g, i: (0,) * a.ndim)

    qo_spec = pl.BlockSpec((NSA_TQ, NSA_REP * HEAD_DIM), lambda b, g, i: (b * nq + i, g))
    onehot = _block_onehot_table(seq)
    qx = _alibi_rows(slopes)
    return pl.pallas_call(
        functools.partial(_nsa_kernel, seq=seq),
        out_shape=jax.ShapeDtypeStruct((t, Q_NSA_DIM), BF16),
        grid=(batch, N_NSA_KV, nq),
        in_specs=[
            qo_spec,
            pl.BlockSpec((NSA_REP, SUBLANES, LANES), lambda b, g, i: (g, 0, 0)),
            cmp_spec(0), cmp_spec(1),
            seq_spec(CB_K_SLC), seq_spec(CB_V_SLC), seq_spec(CB_K_WIN), seq_spec(CB_V_WIN),
            whole(rel_tab), whole(onehot), whole(win_tab), whole(causal_tab),
            pl.BlockSpec((None, NSA_TQ, 3 * NSA_REP), lambda b, g, i: (g, b * nq + i, 0)),
        ],
        out_specs=qo_spec,
        scratch_shapes=[
            pltpu.VMEM((rows, HEAD_DIM + LANES), BF16),
            pltpu.VMEM((rows, ATT_TK), F32),
            pltpu.VMEM((rows, ATT_TK), BF16),
            pltpu.VMEM((rows, LANES), F32),
            pltpu.VMEM((rows, LANES), F32),
            pltpu.VMEM((rows, HEAD_DIM), F32),
            pltpu.VMEM((2, rows, HEAD_DIM), F32),
            pltpu.VMEM((LANES, NSA_TQ), F32),
        ],
        compiler_params=_params("arbitrary", "arbitrary", "arbitrary"),
        name="nsa_mixer",
    )(proj, qx, cmp_kv, cmp_kv, proj, proj, proj, proj, rel_tab, onehot, win_tab, causal_tab, gates_t)


def _dil_kernel(q_ref, qx_ref, k_ref, v_ref, rel_ref, mtab_ref, o_ref,
                qa_scr, s_scr, p_scr, m_scr, l_scr, acc_scr, *, seq):
    t0 = pl.program_id(2) * DIL_TQ
    qa_scr[:, :HEAD_DIM] = q_ref[...]
    qa_scr[:, HEAD_DIM:] = jnp.broadcast_to(qx_ref[0:1, :], (DIL_TQ, LANES)).astype(BF16)
    _softmax_init(m_scr, l_scr, acc_scr)
    masked = mtab_ref.shape[0] - 1

    def body(kt, carry):
        k0 = pl.multiple_of(kt * ATT_TK, ATT_TK)
        r0 = pl.multiple_of(k0 - t0 + seq, LANES)
        k_aug = jnp.concatenate([k_ref[pl.ds(k0, ATT_TK), :], rel_ref[pl.ds(r0, ATT_TK), :]], axis=1)
        s_scr[...] = _dot_nt(qa_scr[...], k_aug)

        def multiplicity(rb):
            u = (t0 + rb * LANES - k0) // LANES
            return mtab_ref[jnp.where((u < 0) | (u >= masked), masked, u)]

        _softmax_tile(s_scr, p_scr, m_scr, l_scr, acc_scr, v_ref[pl.ds(k0, ATT_TK), :], multiplicity)
        return carry

    reach = max(w for w, _ in DIL_PATTERNS)
    first_kt = jnp.maximum(t0 - reach, 0) // ATT_TK
    lax.fori_loop(first_kt, (t0 + DIL_TQ - 1) // ATT_TK + 1, body, 0)
    o_ref[...] = (acc_scr[...] / l_scr[...]).astype(o_ref.dtype)


def _dil_mixer(slopes, proj, rel_tab, batch, seq):
    t = proj.shape[0]
    nq = seq // DIL_TQ
    mult_tab = _dilation_table()
    qx = _alibi_rows(slopes)

    def whole(a):
        return pl.BlockSpec(a.shape, lambda b, h, i: (0,) * a.ndim)

    return pl.pallas_call(
        functools.partial(_dil_kernel, seq=seq),
        out_shape=jax.ShapeDtypeStruct((t, DIL_DIM), BF16),
        grid=(batch, N_DIL_HEADS, nq),
        in_specs=[
            pl.BlockSpec((DIL_TQ, HEAD_DIM), lambda b, h, i: (b * nq + i, CB_Q_DIL + h)),
            pl.BlockSpec((None, SUBLANES, LANES), lambda b, h, i: (h, 0, 0)),
            pl.BlockSpec((seq, HEAD_DIM), lambda b, h, i: (b, CB_K_DIL + h)),
            pl.BlockSpec((seq, HEAD_DIM), lambda b, h, i: (b, CB_V_DIL + h)),
            whole(rel_tab), whole(mult_tab),
        ],
        out_specs=pl.BlockSpec((DIL_TQ, HEAD_DIM), lambda b, h, i: (b * nq + i, h)),
        scratch_shapes=[
            pltpu.VMEM((DIL_TQ, HEAD_DIM + LANES), BF16),
            pltpu.VMEM((DIL_TQ, ATT_TK), F32),
            pltpu.VMEM((DIL_TQ, ATT_TK), BF16),
            pltpu.VMEM((DIL_TQ, LANES), F32),
            pltpu.VMEM((DIL_TQ, LANES), F32),
            pltpu.VMEM((DIL_TQ, HEAD_DIM), F32),
        ],
        compiler_params=_params("arbitrary", "arbitrary", "arbitrary"),
        name="dil_mixer",
    )(proj, qx, proj, proj, rel_tab, mult_tab)


def _out_proj_kernel(a_ref, b_ref, ga_ref, gb_ref, w_ref, res_ref, o_ref, h_scr):
    @pl.when(pl.program_id(1) == 0)
    def _():
        na = a_ref.shape[1]
        h_scr[:, :na] = _rms_rows(a_ref[...].astype(F32), ga_ref[...]).astype(BF16)
        h_scr[:, na:] = _rms_rows(b_ref[...].astype(F32), gb_ref[...]).astype(BF16)

    o_ref[...] = res_ref[...] + _dot(h_scr[...], w_ref[...])


def _out_proj(o_nsa, o_dil, g_nsa, g_dil, w, resid):
    t, d = resid.shape
    na, nb = o_nsa.shape[1], o_dil.shape[1]
    return pl.pallas_call(
        _out_proj_kernel,
        out_shape=jax.ShapeDtypeStruct((t, d), F32),
        grid=(t // PROJ_TM, d // PROJ_TN),
        in_specs=[
            pl.BlockSpec((PROJ_TM, na), lambda i, j: (i, 0)),
            pl.BlockSpec((PROJ_TM, nb), lambda i, j: (i, 0)),
            pl.BlockSpec((1, na), lambda i, j: (0, 0)),
            pl.BlockSpec((1, nb), lambda i, j: (0, 0)),
            pl.BlockSpec((na + nb, PROJ_TN), lambda i, j: (0, j)),
            pl.BlockSpec((PROJ_TM, PROJ_TN), lambda i, j: (i, j)),
        ],
        out_specs=pl.BlockSpec((PROJ_TM, PROJ_TN), lambda i, j: (i, j)),
        scratch_shapes=[pltpu.VMEM((PROJ_TM, na + nb), BF16)],
        compiler_params=_params("parallel", "arbitrary"),
        name="out_proj",
    )(o_nsa, o_dil, g_nsa, g_dil, w, resid)


def _router_kernel(x_ref, g_ref, wr_ref, br_ref, hn_ref, idx_ref, tw_ref):
    h = _rms_rows(x_ref[...], g_ref[...])
    hn_ref[...] = h
    logits = _dot(h, wr_ref[...], precision=lax.Precision.HIGHEST) + br_ref[...]
    lane = lax.broadcasted_iota(I32, logits.shape, 1)
    lanef = lane.astype(F32)
    work = jnp.where(lane < N_EXPERTS, logits, -jnp.inf)
    vals, ids = [], []
    for _ in range(TOP_K):
        mx = jnp.max(work, axis=-1, keepdims=True)
        first = jnp.min(jnp.where(work == mx, lanef, float(LANES)), axis=-1, keepdims=True)
        vals.append(mx)
        ids.append(first)
        work = jnp.where(lanef == first, -jnp.inf, work)
    es = [jnp.exp(v - vals[0]) for v in vals]
    den = functools.reduce(lambda a, b: a + b, es)
    idx_out = jnp.zeros(logits.shape, F32)
    tw_out = jnp.zeros(logits.shape, F32)
    for k in range(TOP_K):
        idx_out = jnp.where(lane == k, ids[k], idx_out)
        tw_out = jnp.where(lane == k, es[k] / den, tw_out)
    idx_ref[...] = idx_out.astype(I32)
    tw_ref[...] = tw_out


def _router(x1, gain, wr, br):
    t, d = x1.shape
    return pl.pallas_call(
        _router_kernel,
        out_shape=(jax.ShapeDtypeStruct((t, d), F32),
                   jax.ShapeDtypeStruct((t, LANES), I32),
                   jax.ShapeDtypeStruct((t, LANES), F32)),
        grid=(t // ROUTER_TM,),
        in_specs=[
            pl.BlockSpec((ROUTER_TM, d), lambda i: (i, 0)),
            pl.BlockSpec((1, d), lambda i: (0, 0)),
            pl.BlockSpec((d, LANES), lambda i: (0, 0)),
            pl.BlockSpec((1, LANES), lambda i: (0, 0)),
        ],
        out_specs=(pl.BlockSpec((ROUTER_TM, d), lambda i: (i, 0)),
                   pl.BlockSpec((ROUTER_TM, LANES), lambda i: (i, 0)),
                   pl.BlockSpec((ROUTER_TM, LANES), lambda i: (i, 0))),
        compiler_params=_params("parallel"),
        name="moe_router",
    )(x1, gain, wr, br)


def _rank_kernel(idx_ref, rank_ref, cnt_ref, carry_scr):
    @pl.when(pl.program_id(0) == 0)
    def _():
        carry_scr[...] = jnp.zeros(carry_scr.shape, F32)

    idx = idx_ref[...]
    lane = lax.broadcasted_iota(I32, idx.shape, 1)
    hits = [lane == idx[:, k:k + 1] for k in range(TOP_K)]
    onehot = functools.reduce(lambda a, b: a + b, [jnp.where(h, 1.0, 0.0) for h in hits])
    ri = lax.broadcasted_iota(I32, (RANK_TM, RANK_TM), 0)
    ci = lax.broadcasted_iota(I32, (RANK_TM, RANK_TM), 1)
    before = jnp.where(ci < ri, 1.0, 0.0).astype(BF16)
    rank = _dot(before, onehot.astype(BF16)) + carry_scr[0:1, :]
    out = jnp.zeros(idx.shape, F32)
    for k in range(TOP_K):
        mine = jnp.sum(jnp.where(hits[k], rank, 0.0), axis=-1, keepdims=True)
        out = jnp.where(lane == k, mine, out)
    rank_ref[...] = out.astype(I32)
    carry = carry_scr[...] + jnp.sum(onehot, axis=0, keepdims=True)
    carry_scr[...] = carry
    cnt_ref[...] = carry


def _rank(idx):
    t = idx.shape[0]
    return pl.pallas_call(
        _rank_kernel,
        out_shape=(jax.ShapeDtypeStruct((t, LANES), I32), jax.ShapeDtypeStruct((SUBLANES, LANES), F32)),
        grid=(t // RANK_TM,),
        in_specs=[pl.BlockSpec((RANK_TM, LANES), lambda i: (i, 0))],
        out_specs=(pl.BlockSpec((RANK_TM, LANES), lambda i: (i, 0)),
                   pl.BlockSpec((SUBLANES, LANES), lambda i: (0, 0))),
        scratch_shapes=[pltpu.VMEM((SUBLANES, LANES), F32)],
        compiler_params=_params("arbitrary"),
        name="moe_rank",
    )(idx)


def _pos_kernel(cnt_ref, idx_ref, rank_ref, pos_ref, vt_ref):
    cnt = cnt_ref[0:1, :]
    lane1 = lax.broadcasted_iota(I32, (1, LANES), 1)
    tiles = jnp.where(lane1 < N_EXPERTS, jnp.floor((cnt + (MOE_TM - 1)) * (1.0 / MOE_TM)), 0.0)
    tiles_before = jnp.zeros((1, LANES), F32)
    rows_before = jnp.zeros((1, LANES), F32)
    for e in range(N_EXPERTS):
        tiles_before = tiles_before + jnp.where(lane1 > e, tiles[:, e:e + 1], 0.0)
        rows_before = rows_before + jnp.where(lane1 > e, cnt[:, e:e + 1], 0.0)
    tiles_through = tiles_before + tiles

    idx = idx_ref[...]
    rank = rank_ref[...].astype(F32)
    lane = lax.broadcasted_iota(I32, idx.shape, 1)
    out = jnp.zeros(idx.shape, F32)
    for k in range(TOP_K):
        mine = lane == idx[:, k:k + 1]
        row0 = jnp.sum(jnp.where(mine, tiles_before * MOE_TM, 0.0), axis=-1, keepdims=True)
        slot0 = jnp.sum(jnp.where(mine, rows_before, 0.0), axis=-1, keepdims=True)
        out = jnp.where(lane == k, row0 + rank[:, k:k + 1], out)
        out = jnp.where(lane == TOP_K + k, slot0 + rank[:, k:k + 1], out)
    pos_ref[...] = out.astype(I32)

    nv = vt_ref.shape[0]
    vl = lax.broadcasted_iota(I32, (nv, LANES), 1)
    v = lax.broadcasted_iota(I32, (nv, LANES), 0).astype(F32)
    total = jnp.sum(tiles, axis=-1, keepdims=True)
    vv = jnp.minimum(v, total - 1.0)
    done = jnp.where((tiles_through <= vv) & (vl < N_EXPERTS), 1.0, 0.0)
    e_v = jnp.minimum(jnp.sum(done, axis=-1, keepdims=True), float(N_EXPERTS - 1))
    mine = vl.astype(F32) == e_v
    first_tile = jnp.sum(jnp.where(mine, tiles_before, 0.0), axis=-1, keepdims=True)
    e_cnt = jnp.sum(jnp.where(mine, cnt, 0.0), axis=-1, keepdims=True)
    e_slot = jnp.sum(jnp.where(mine, rows_before, 0.0), axis=-1, keepdims=True)
    vv1 = vv[:, 0:1]
    valid = v[:, 0:1] < total
    offset = (vv1 - first_tile) * MOE_TM
    held = jnp.where(valid, jnp.clip(e_cnt - offset, 0.0, float(MOE_TM)), 0.0)
    nrows = jnp.ceil(held * (1.0 / MOE_SUB)) * MOE_SUB
    cols = [e_v, vv1, nrows, jnp.where(valid, 1.0, 0.0), e_slot + offset, held]
    table = jnp.zeros((nv, LANES), F32)
    for c, val in enumerate(cols):
        table = jnp.where(vl == c, val, table)
    vt_ref[...] = table.astype(I32)


def _positions(cnt, idx, rank, n_visits):
    t = idx.shape[0]
    nvp = -(-n_visits // SUBLANES) * SUBLANES
    return pl.pallas_call(
        _pos_kernel,
        out_shape=(jax.ShapeDtypeStruct((t, LANES), I32), jax.ShapeDtypeStruct((nvp, LANES), I32)),
        grid=(t // POS_TM,),
        in_specs=[pl.BlockSpec((SUBLANES, LANES), lambda i: (0, 0)),
                  pl.BlockSpec((POS_TM, LANES), lambda i: (i, 0)),
                  pl.BlockSpec((POS_TM, LANES), lambda i: (i, 0))],
        out_specs=(pl.BlockSpec((POS_TM, LANES), lambda i: (i, 0)),
                   pl.BlockSpec((nvp, LANES), lambda i: (0, 0))),
        compiler_params=_params("arbitrary"),
        name="moe_positions",
    )(cnt, idx, rank)


def _invert_kernel(slot_ref, tok_ref):
    def body(i, carry):
        tok_ref[slot_ref[i]] = i // TOP_K
        return carry

    lax.fori_loop(0, slot_ref.shape[0], body, 0)


def _invert(slot_flat):
    return pl.pallas_call(
        _invert_kernel,
        out_shape=jax.ShapeDtypeStruct(slot_flat.shape, I32),
        in_specs=[pl.BlockSpec(memory_space=pltpu.SMEM)],
        out_specs=pl.BlockSpec(memory_space=pltpu.SMEM),
        name="moe_invert",
    )(slot_flat)


def _dispatch_kernel(vt_ref, tok_ref, hn_ref, xs_ref, buf, sems):
    i = pl.program_id(0)
    per_visit = MOE_TM // DISPATCH_TM

    def held_rows(step):
        v = step // per_visit
        part = (step % per_visit) * DISPATCH_TM
        return jnp.clip(vt_ref[v * VT_W + 5] - part, 0, DISPATCH_TM), vt_ref[v * VT_W + 4] + part

    def row_copy(src_row, r, slot):
        return pltpu.make_async_copy(hn_ref.at[pl.ds(src_row, 1)], buf.at[slot, pl.ds(r, 1)], sems.at[slot])

    def start(step, slot):
        n, base = held_rows(step)
        lax.fori_loop(0, n, lambda r, c: (row_copy(tok_ref[base + r], r, slot).start(), c)[1], 0)

    def wait(step, slot):
        n, _ = held_rows(step)
        lax.fori_loop(0, n, lambda r, c: (row_copy(0, r, slot).wait(), c)[1], 0)

    @pl.when(i == 0)
    def _():
        buf[...] = jnp.zeros(buf.shape, buf.dtype)
        start(0, 0)

    @pl.when(i + 1 < pl.num_programs(0))
    def _():
        start(i + 1, (i + 1) % 2)

    wait(i, i % 2)
    n, _ = held_rows(i)
    rowid = lax.broadcasted_iota(I32, (DISPATCH_TM, 1), 0)
    xs_ref[...] = jnp.where(rowid < n, buf[i % 2], 0.0).astype(xs_ref.dtype)


def _dispatch(vt, tok, hn, n_visits):
    d = hn.shape[1]
    return pl.pallas_call(
        _dispatch_kernel,
        out_shape=jax.ShapeDtypeStruct((n_visits * MOE_TM, d), BF16),
        grid_spec=pltpu.PrefetchScalarGridSpec(
            num_scalar_prefetch=2,
            grid=(n_visits * MOE_TM // DISPATCH_TM,),
            in_specs=[pl.BlockSpec(memory_space=pl.ANY)],
            out_specs=pl.BlockSpec((DISPATCH_TM, d), lambda i, vt, tok: (i, 0)),
            scratch_shapes=[pltpu.VMEM((2, DISPATCH_TM, d), F32), pltpu.SemaphoreType.DMA((2,))],
        ),
        compiler_params=_params("arbitrary"),
        name="moe_dispatch",
    )(vt, tok, hn)


def _moe_up_kernel(vt_ref, xs_ref, wg_ref, wu_ref, bg_ref, bu_ref, h_ref, wg_scr, wu_scr):
    nrows = vt_ref[pl.program_id(0) * VT_W + 2]

    @pl.when(nrows > 0)
    def _():
        wg_scr[...] = wg_ref[...].astype(BF16)
        wu_scr[...] = wu_ref[...].astype(BF16)

    for sb in range(MOE_TM // MOE_SUB):
        rs = slice(sb * MOE_SUB, (sb + 1) * MOE_SUB)

        @pl.when(sb * MOE_SUB < nrows)
        def _():
            x = xs_ref[rs, :]
            gate = jnp.minimum(_dot(x, wg_scr[...]) + bg_ref[...], SWIGLU_LIMIT)
            up = jnp.clip(_dot(x, wu_scr[...]) + bu_ref[...], -SWIGLU_LIMIT, SWIGLU_LIMIT)
            act = (up + 1.0) * (gate * _sigmoid(SWIGLU_ALPHA * gate))
            h_ref[rs, :] = act.astype(h_ref.dtype)

        @pl.when(sb * MOE_SUB >= nrows)
        def _():
            h_ref[rs, :] = jnp.zeros((MOE_SUB, h_ref.shape[1]), h_ref.dtype)


def _moe_up(vt, xs, w_gate_up, b_gate_up, n_visits):
    d = w_gate_up.shape[1]
    nf = D_EXPERT // MOE_TF

    def fsel(v, f, vt):
        return jnp.where(vt[v * VT_W + 3] > 0, f, nf - 1)

    return pl.pallas_call(
        _moe_up_kernel,
        out_shape=jax.ShapeDtypeStruct((n_visits * MOE_TM, D_EXPERT), BF16),
        grid_spec=pltpu.PrefetchScalarGridSpec(
            num_scalar_prefetch=1,
            grid=(n_visits, nf),
            in_specs=[
                pl.BlockSpec((MOE_TM, d), lambda v, f, vt: (vt[v * VT_W + 1], 0)),
                pl.BlockSpec((None, d, MOE_TF), lambda v, f, vt: (vt[v * VT_W], 0, fsel(v, f, vt))),
                pl.BlockSpec((None, d, MOE_TF), lambda v, f, vt: (vt[v * VT_W], 0, nf + fsel(v, f, vt))),
                pl.BlockSpec((None, 1, MOE_TF), lambda v, f, vt: (vt[v * VT_W], 0, fsel(v, f, vt))),
                pl.BlockSpec((None, 1, MOE_TF), lambda v, f, vt: (vt[v * VT_W], 0, nf + fsel(v, f, vt))),
            ],
            out_specs=pl.BlockSpec((MOE_TM, MOE_TF), lambda v, f, vt: (v, f)),
            scratch_shapes=[pltpu.VMEM((d, MOE_TF), BF16), pltpu.VMEM((d, MOE_TF), BF16)],
        ),
        compiler_params=_params("arbitrary", "arbitrary"),
        name="moe_up",
    )(vt, xs, w_gate_up, w_gate_up, b_gate_up, b_gate_up)


def _moe_down_kernel(vt_ref, h_ref, wd_ref, bd_ref, y_ref, wd_scr):
    nrows = vt_ref[pl.program_id(0) * VT_W + 2]

    @pl.when(nrows > 0)
    def _():
        wd_scr[...] = wd_ref[...].astype(BF16)

    for sb in range(MOE_TM // MOE_SUB):
        rs = slice(sb * MOE_SUB, (sb + 1) * MOE_SUB)

        @pl.when(sb * MOE_SUB < nrows)
        def _():
            y_ref[rs, :] = _dot(h_ref[rs, :], wd_scr[...]) + bd_ref[...]

        @pl.when(sb * MOE_SUB >= nrows)
        def _():
            y_ref[rs, :] = jnp.zeros((MOE_SUB, y_ref.shape[1]), y_ref.dtype)


def _moe_down(vt, h, w_down, b_down, n_visits):
    d = w_down.shape[2]
    nj = d // MOE_TN

    def jsel(v, j, vt):
        return jnp.where(vt[v * VT_W + 3] > 0, j, nj - 1)

    return pl.pallas_call(
        _moe_down_kernel,
        out_shape=jax.ShapeDtypeStruct((n_visits * MOE_TM, d), F32),
        grid_spec=pltpu.PrefetchScalarGridSpec(
            num_scalar_prefetch=1,
            grid=(n_visits, nj),
            in_specs=[
                pl.BlockSpec((MOE_TM, D_EXPERT), lambda v, j, vt: (vt[v * VT_W + 1], 0)),
                pl.BlockSpec((None, D_EXPERT, MOE_TN), lambda v, j, vt: (vt[v * VT_W], 0, jsel(v, j, vt))),
                pl.BlockSpec((None, 1, MOE_TN), lambda v, j, vt: (vt[v * VT_W], 0, jsel(v, j, vt))),
            ],
            out_specs=pl.BlockSpec((MOE_TM, MOE_TN), lambda v, j, vt: (v, j)),
            scratch_shapes=[pltpu.VMEM((D_EXPERT, MOE_TN), BF16)],
        ),
        compiler_params=_params("arbitrary", "arbitrary"),
        name="moe_down",
    )(vt, h, w_down, b_down)


def _combine_kernel(pos_ref, ys_ref, x_ref, tw_ref, o_ref, buf, sem):
    base = pl.program_id(0) * COMBINE_TM

    def row_copy(r, k):
        src = pos_ref[(base + r) * TOP_K + k]
        return pltpu.make_async_copy(ys_ref.at[pl.ds(src, 1)], buf.at[k, pl.ds(r, 1)], sem)

    def issue(r, carry):
        for k in range(TOP_K):
            row_copy(r, k).start()
        return carry

    def drain(r, carry):
        for k in range(TOP_K):
            row_copy(r, k).wait()
        return carry

    lax.fori_loop(0, COMBINE_TM, issue, 0)
    lax.fori_loop(0, COMBINE_TM, drain, 0)
    tw = tw_ref[...]
    acc = x_ref[...]
    for k in range(TOP_K):
        acc = acc + tw[:, k:k + 1] * buf[k]
    o_ref[...] = acc


def _combine(pos_flat, ys, x1, tw):
    t, d = x1.shape
    return pl.pallas_call(
        _combine_kernel,
        out_shape=jax.ShapeDtypeStruct((t, d), F32),
        grid_spec=pltpu.PrefetchScalarGridSpec(
            num_scalar_prefetch=1,
            grid=(t // COMBINE_TM,),
            in_specs=[pl.BlockSpec(memory_space=pl.ANY),
                      pl.BlockSpec((COMBINE_TM, d), lambda i, pos: (i, 0)),
                      pl.BlockSpec((COMBINE_TM, LANES), lambda i, pos: (i, 0))],
            out_specs=pl.BlockSpec((COMBINE_TM, d), lambda i, pos: (i, 0)),
            scratch_shapes=[pltpu.VMEM((TOP_K, COMBINE_TM, d), F32), pltpu.SemaphoreType.DMA],
        ),
        compiler_params=_params("arbitrary"),
        name="moe_combine",
    )(pos_flat, ys, x1, tw)


def _alibi_slopes():
    n = N_NSA_HEADS + N_DIL_HEADS
    i = jnp.arange(1, n + 1, dtype=F32)
    return jnp.exp2(-8.0 * i / n)


def _attention_block(x2, batch, seq, attn_norm, w_in, pe_k, w_k1, w_k2, pe_v, w_v1, w_v2,
                     qn_nsa, kn_cmp, kn_slc, kn_win, qn_dil, kn_dil, on_nsa, on_dil, w_out):
    d = x2.shape[1]
    a_dim = Q_NSA_DIM + 6 * KV_NSA_DIM
    d_off = a_dim + GATE_DIM
    n_pad = N_PROJ - (a_dim + 3 * DIL_DIM + GATE_DIM)
    w = jnp.concatenate([w_in[:, :a_dim], w_in[:, d_off:], w_in[:, a_dim:d_off],
                         jnp.zeros((d, n_pad), w_in.dtype)], axis=1).astype(BF16)
    ones_kv = jnp.ones((KV_NSA_DIM,), F32)
    ones_dil = jnp.ones((DIL_DIM,), F32)
    tail = jnp.ones((GATE_DIM + n_pad,), F32)
    q_scale = ATTN_SCALE * LOG2E
    col_gain = jnp.concatenate([
        jnp.tile(qn_nsa, N_NSA_HEADS) * q_scale, ones_kv, ones_kv, jnp.tile(kn_slc, N_NSA_KV), ones_kv,
        jnp.tile(kn_win, N_NSA_KV), ones_kv, jnp.tile(qn_dil, N_DIL_HEADS) * q_scale,
        jnp.tile(kn_dil, N_DIL_HEADS), ones_dil, tail])[None, :]
    col_flag = jnp.concatenate([
        jnp.ones((Q_NSA_DIM,), F32), 0 * ones_kv, 0 * ones_kv, ones_kv, 0 * ones_kv, ones_kv, 0 * ones_kv,
        ones_dil, ones_dil, 0 * ones_dil, 0 * tail])[None, :]
    proj = _in_proj(x2, attn_norm[None, :], w, col_gain, col_flag)

    n_chunks = seq // CMP_STRIDE
    kv = proj[:, CB_K_CMP * LANES:(CB_V_CMP + N_NSA_KV) * LANES]
    kv = kv.reshape(batch, n_chunks, CMP_STRIDE, 2, N_NSA_KV, HEAD_DIM).transpose(3, 0, 4, 1, 2, 5)
    chunks = kv.reshape(2, batch * N_NSA_KV, n_chunks, CMP_STRIDE * HEAD_DIM)
    assert CMP_BLOCK == 2 * CMP_STRIDE
    blocks = jnp.concatenate([chunks[:, :, :-1], chunks[:, :, 1:]], axis=-1)
    blocks = jnp.pad(blocks, ((0, 0), (0, 0), (0, 1), (0, 0)))
    pe = jnp.stack([pe_k.reshape(1, -1), pe_v.reshape(1, -1)])
    w1 = jnp.stack([w_k1, w_v1]).astype(BF16)
    w2 = jnp.stack([w_k2, w_v2]).astype(BF16)
    cmp_kv = _compress(blocks, pe, w1, w2, kn_cmp[None, :])

    slopes = _alibi_slopes()
    rel_tab = _rel_position_table(seq)
    gates = proj[:, CB_GATE * LANES:CB_GATE * LANES + GATE_DIM]
    gates_t = gates.reshape(-1, N_NSA_KV, 3 * NSA_REP).transpose(1, 0, 2)
    o_nsa = _nsa_mixer(slopes[0::2], proj, cmp_kv, gates_t, rel_tab, batch, seq)
    o_dil = _dil_mixer(slopes[1::2], proj, rel_tab, batch, seq)
    return _out_proj(o_nsa, o_dil, on_nsa[None, :], on_dil[None, :], w_out.astype(BF16), x2)


def _moe_block(x1, ffn_norm, w_router, b_router, w_gate_up, b_gate_up, w_down, b_down):
    t, d = x1.shape
    wr = jnp.pad(w_router, ((0, 0), (0, LANES - N_EXPERTS)))
    br = jnp.pad(b_router, (0, LANES - N_EXPERTS))[None, :]
    hn, idx, tw = _router(x1, ffn_norm[None, :], wr, br)
    rank, cnt = _rank(idx)
    n_visits = N_EXPERTS + (t * TOP_K) // MOE_TM
    pos, table = _positions(cnt, idx, rank, n_visits)
    pos_flat = pos[:, :TOP_K].reshape(-1)
    tok = _invert(pos[:, TOP_K:2 * TOP_K].reshape(-1))
    vt = table[:n_visits, :VT_W].reshape(-1)
    xs = _dispatch(vt, tok, hn, n_visits)
    h = _moe_up(vt, xs, w_gate_up, b_gate_up[:, None, :], n_visits)
    ys = _moe_down(vt, h, w_down, b_down[:, None, :], n_visits)
    return _combine(pos_flat, ys, x1, tw)


def kernel(x, attn_norm, w_in, cmp_pos_k, w_cmp_k1, w_cmp_k2, cmp_pos_v, w_cmp_v1, w_cmp_v2, q_norm_nsa, k_norm_cmp, k_norm_slc, k_norm_win, q_norm_dil, k_norm_dil, out_norm_nsa, out_norm_dil, w_out, ffn_norm, w_router, b_router, w_gate_up, b_gate_up, w_down, b_down):
    batch, seq, d = x.shape
    x2 = x.reshape(batch * seq, d)
    for layer in range(attn_norm.shape[0]):
        x2 = _attention_block(
            x2, batch, seq, attn_norm[layer], w_in[layer], cmp_pos_k[layer], w_cmp_k1[layer],
            w_cmp_k2[layer], cmp_pos_v[layer], w_cmp_v1[layer], w_cmp_v2[layer], q_norm_nsa[layer],
            k_norm_cmp[layer], k_norm_slc[layer], k_norm_win[layer], q_norm_dil[layer],
            k_norm_dil[layer], out_norm_nsa[layer], out_norm_dil[layer], w_out[layer])
        x2 = _moe_block(x2, ffn_norm[layer], w_router[layer], b_router[layer], w_gate_up[layer],
                        b_gate_up[layer], w_down[layer], b_down[layer])
    return x2.reshape(batch, seq, d)
```

```python
import functools

import numpy as np
import jax
import jax.numpy as jnp
from jax import lax
from jax.experimental import pallas as pl
from jax.experimental.pallas import tpu as pltpu

F32, BF16, I32 = jnp.float32, jnp.bfloat16, jnp.int32

HEAD_DIM = 128
N_NSA_HEADS = 16
N_NSA_KV = 4
NSA_REP = N_NSA_HEADS // N_NSA_KV
N_DIL_HEADS = 16
CMP_BLOCK = 32
CMP_STRIDE = 16
CMP_HIDDEN = 256
SLC_BLOCK = 64
SLC_TOP_N = 16
FORCED_SCORE = 1.0e4
WIN_SIZE = 512
DIL_PATTERNS = ((128, 1), (512, 4), (2048, 16))
N_EXPERTS = 32
TOP_K = 4
D_EXPERT = 1536
SWIGLU_ALPHA = 1.702
SWIGLU_LIMIT = 7.0
RMS_EPS = 1e-6
NEG_INF = -1e30
ATTN_SCALE = HEAD_DIM ** -0.5
LOG2E = 1.4426950408889634

Q_NSA_DIM = N_NSA_HEADS * HEAD_DIM
KV_NSA_DIM = N_NSA_KV * HEAD_DIM
GATE_DIM = N_NSA_HEADS * 3
DIL_DIM = N_DIL_HEADS * HEAD_DIM

LANES = 128
SUBLANES = 8
VMEM_LIMIT = 56 * 1024 * 1024

CB_Q_NSA = 0
CB_K_CMP = CB_Q_NSA + N_NSA_HEADS
CB_V_CMP = CB_K_CMP + N_NSA_KV
CB_K_SLC = CB_V_CMP + N_NSA_KV
CB_V_SLC = CB_K_SLC + N_NSA_KV
CB_K_WIN = CB_V_SLC + N_NSA_KV
CB_V_WIN = CB_K_WIN + N_NSA_KV
CB_Q_DIL = CB_V_WIN + N_NSA_KV
CB_K_DIL = CB_Q_DIL + N_DIL_HEADS
CB_V_DIL = CB_K_DIL + N_DIL_HEADS
CB_GATE = CB_V_DIL + N_DIL_HEADS
PROJ_TN = 512
N_PROJ = -(-(CB_GATE + 1) * LANES // PROJ_TN) * PROJ_TN

PROJ_TM = 512
NSA_TQ = 128
DIL_TQ = 512
ATT_TK = 256
ROUTER_TM = 256
RANK_TM = 512
POS_TM = 1024
MOE_TM = 1024
MOE_SUB = 256
MOE_TF = 256
MOE_TN = 1024
INVERT_UNROLL = 16
GATHER_UNROLL = 8
DISPATCH_TM = 256
COMBINE_TM = 128
VT_W = 8

POS_SHIFT = 8
POS_RADIX = 1 << POS_SHIFT
ALIBI_PIECES = 3
ALIBI_COLS = 2 * ALIBI_PIECES
SEL_COL0 = 64
MASK_BIG = 2.0 ** 100


def _dot(a, b, **kw):
    return jnp.dot(a, b, preferred_element_type=F32, **kw)


def _dot_nt(a, b, **kw):
    return lax.dot_general(a, b, (((1,), (1,)), ((), ())), preferred_element_type=F32, **kw)


def _rms_rows(x, gain):
    ms = jnp.mean(x * x, axis=-1, keepdims=True)
    return x * lax.rsqrt(ms + RMS_EPS) * gain


def _sigmoid(x):
    return 1.0 / (1.0 + jnp.exp(-x))


def _params(*sem):
    return pltpu.CompilerParams(dimension_semantics=sem, vmem_limit_bytes=VMEM_LIMIT)


def _in_proj_kernel(x_ref, g_ref, w_ref, cg_ref, cf_ref, o_ref, h_scr):
    @pl.when(pl.program_id(1) == 0)
    def _():
        h_scr[...] = _rms_rows(x_ref[...], g_ref[...]).astype(BF16)

    acc = _dot(h_scr[...], w_ref[...])
    for c in range(PROJ_TN // LANES):
        cs = slice(c * LANES, (c + 1) * LANES)
        a = acc[:, cs]
        r = lax.rsqrt(jnp.mean(a * a, axis=-1, keepdims=True) + RMS_EPS)
        y = a * jnp.where(cf_ref[:, cs] > 0, r, 1.0) * cg_ref[:, cs]
        o_ref[:, cs] = y.astype(o_ref.dtype)


def _in_proj(x2, gain, w, col_gain, col_flag):
    t, d = x2.shape
    n = w.shape[1]
    return pl.pallas_call(
        _in_proj_kernel,
        out_shape=jax.ShapeDtypeStruct((t, n), BF16),
        grid=(t // PROJ_TM, n // PROJ_TN),
        in_specs=[
            pl.BlockSpec((PROJ_TM, d), lambda i, j: (i, 0)),
            pl.BlockSpec((1, d), lambda i, j: (0, 0)),
            pl.BlockSpec((d, PROJ_TN), lambda i, j: (0, j)),
            pl.BlockSpec((1, PROJ_TN), lambda i, j: (0, j)),
            pl.BlockSpec((1, PROJ_TN), lambda i, j: (0, j)),
        ],
        out_specs=pl.BlockSpec((PROJ_TM, PROJ_TN), lambda i, j: (i, j)),
        scratch_shapes=[pltpu.VMEM((PROJ_TM, d), BF16)],
        compiler_params=_params("parallel", "arbitrary"),
        name="in_proj",
    )(x2, gain, w, col_gain, col_flag)


def _compress_kernel(a_ref, pe_ref, w1_ref, w2_ref, kn_ref, o_ref):
    a = (a_ref[...].astype(F32) + pe_ref[...]).astype(BF16)
    hid = _dot(a, w1_ref[...])
    hid = hid * _sigmoid(hid)
    out = _dot(hid.astype(BF16), w2_ref[...])
    normed = _rms_rows(out, kn_ref[...])
    o_ref[...] = jnp.where(pl.program_id(0) == 0, normed, out).astype(o_ref.dtype)


def _compress(blocks, pe, w1, w2, kn):
    two, bg, ncp, flat = blocks.shape
    return pl.pallas_call(
        _compress_kernel,
        out_shape=jax.ShapeDtypeStruct((two, bg, ncp, HEAD_DIM), BF16),
        grid=(two, bg),
        in_specs=[
            pl.BlockSpec((None, None, ncp, flat), lambda s, i: (s, i, 0, 0)),
            pl.BlockSpec((None, 1, flat), lambda s, i: (s, 0, 0)),
            pl.BlockSpec((None, flat, CMP_HIDDEN), lambda s, i: (s, 0, 0)),
            pl.BlockSpec((None, CMP_HIDDEN, HEAD_DIM), lambda s, i: (s, 0, 0)),
            pl.BlockSpec((1, HEAD_DIM), lambda s, i: (0, 0)),
        ],
        out_specs=pl.BlockSpec((None, None, ncp, HEAD_DIM), lambda s, i: (s, i, 0, 0)),
        compiler_params=_params("arbitrary", "arbitrary"),
        name="nsa_compress",
    )(blocks, pe, w1, w2, kn)


def _split_position(v):
    hi = np.floor_divide(v, POS_RADIX)
    return hi, v - POS_RADIX * hi


def _rel_position_table(seq):
    hi, lo = _split_position(np.arange(2 * seq) - seq)
    tab = np.zeros((2 * seq, LANES), np.float32)
    tab[:, 0:ALIBI_COLS:2] = hi[:, None]
    tab[:, 1:ALIBI_COLS:2] = lo[:, None]
    return jnp.asarray(tab, BF16)


def _block_onehot_table(seq):
    tab = np.zeros((seq, LANES), np.float32)
    tab[np.arange(seq), SEL_COL0 + np.arange(seq) // SLC_BLOCK] = 1.0
    return jnp.asarray(tab, BF16)


def _alibi_rows(slopes):
    rest = slopes.astype(F32) * LOG2E
    cols = []
    for _ in range(ALIBI_PIECES):
        piece = rest.astype(BF16).astype(F32)
        rest = rest - piece
        cols += [piece * POS_RADIX, piece]
    rows = jnp.concatenate([jnp.stack(cols, axis=1),
                            jnp.zeros((slopes.shape[0], LANES - ALIBI_COLS), F32)], axis=1)
    return jnp.broadcast_to(rows[:, None, :], (slopes.shape[0], SUBLANES, LANES))


def _tile_distance(u):
    return LANES * u + np.arange(LANES)[:, None] - np.arange(ATT_TK)[None, :]


def _mask_table(n, valid_fn):
    tabs = [np.where(valid_fn(_tile_distance(u)), 0.0, -MASK_BIG) for u in range(n)]
    return jnp.asarray(np.stack(tabs), F32)


def _dilation_table():
    reach = max(w for w, _ in DIL_PATTERNS)
    n = (reach + ATT_TK - 1) // LANES + 1
    tabs = []
    for u in range(n):
        d = _tile_distance(u)
        mult = sum(((d >= 0) & (d <= w) & (d % dil == 0)).astype(np.float32) for w, dil in DIL_PATTERNS)
        tabs.append(np.where(mult > 0, np.log2(np.maximum(mult, 1.0)), -MASK_BIG))
    tabs.append(np.full((LANES, ATT_TK), -MASK_BIG))
    return jnp.asarray(np.stack(tabs), F32)


def _softmax_tile(s_scr, p_scr, m_scr, l_scr, acc_scr, v, bias_for_block):
    for rb in range(s_scr.shape[0] // LANES):
        rs = slice(rb * LANES, (rb + 1) * LANES)
        s = s_scr[rs, :]
        bias = bias_for_block(rb)
        if bias is not None:
            s = s + bias
        m_old = m_scr[rs, :]
        m_new = jnp.maximum(m_old, jnp.max(s, axis=-1, keepdims=True))
        alpha = jnp.exp2(m_old - m_new)
        p = [jnp.exp2(s[:, c * LANES:(c + 1) * LANES] - m_new) for c in range(ATT_TK // LANES)]
        total = functools.reduce(lambda a, b: a + b, p)
        l_scr[rs, :] = alpha * l_scr[rs, :] + jnp.sum(total, axis=-1, keepdims=True)
        acc_scr[rs, :] = alpha * acc_scr[rs, :]
        m_scr[rs, :] = m_new
        for c in range(ATT_TK // LANES):
            p_scr[rs, c * LANES:(c + 1) * LANES] = p[c].astype(BF16)
    acc_scr[...] += _dot(p_scr[...], v)


def _softmax_init(m_scr, l_scr, acc_scr):
    m_scr[...] = jnp.full(m_scr.shape, NEG_INF, F32)
    l_scr[...] = jnp.zeros(l_scr.shape, F32)
    acc_scr[...] = jnp.zeros(acc_scr.shape, F32)


def _attend(first, end, qa_scr, s_bufs, p_scr, m_scr, l_scr, acc_scr, load_k_aug, load_v, bias_for_tile):
    s_a, s_b = s_bufs
    last = end - 1

    def scores(kt, s_buf):
        s_buf[...] = _dot_nt(qa_scr[...], load_k_aug(jnp.minimum(kt, last)))

    def absorb(kt, s_buf):
        _softmax_tile(s_buf, p_scr, m_scr, l_scr, acc_scr, load_v(kt), bias_for_tile(kt))

    scores(first, s_a)

    def pair(j, carry):
        a = first + 2 * j
        scores(a + 1, s_b)
        absorb(a, s_a)
        scores(a + 2, s_a)
        absorb(a + 1, s_b)
        return carry

    lax.fori_loop(0, (end - first) // 2, pair, 0)

    @pl.when((end - first) % 2 == 1)
    def _():
        absorb(last, s_a)


def _nsa_kernel(q_ref, qx_ref, kc_ref, vc_ref, ks_ref, vs_ref, kw_ref, vw_ref, rel_ref, oh_ref,
                wtab_ref, ctab_ref, gt_ref, o_ref,
                qa_scr, s_a_scr, s_b_scr, p_scr, m_scr, l_scr, acc_scr, ob_scr, sc_scr, *, seq):
    s_bufs = (s_a_scr, s_b_scr)
    t0 = pl.program_id(2) * NSA_TQ
    n_slc = seq // SLC_BLOCK
    heads = [slice(r * NSA_TQ, (r + 1) * NSA_TQ) for r in range(NSA_REP)]
    alibi = [jnp.broadcast_to(qx_ref[r, 0:1, :], (NSA_TQ, LANES)) for r in range(NSA_REP)]
    for r in range(NSA_REP):
        qa_scr[heads[r], :HEAD_DIM] = q_ref[:, r * HEAD_DIM:(r + 1) * HEAD_DIM]
        qa_scr[heads[r], HEAD_DIM:] = alibi[r].astype(BF16)

    ncp = kc_ref.shape[0]
    n_i = lax.broadcasted_iota(I32, (ncp, LANES), 0)
    ln = lax.broadcasted_iota(I32, (ncp, LANES), 1)
    rel_end = n_i * CMP_STRIDE + (CMP_BLOCK - 1) - t0
    digits = jnp.where((ln & 1) == 0, rel_end >> POS_SHIFT, rel_end & (POS_RADIX - 1))
    feat = jnp.where(ln < ALIBI_COLS, digits, 0).astype(F32).astype(BF16)
    kc_aug = jnp.concatenate([kc_ref[...], feat], axis=1)
    row = lax.broadcasted_iota(I32, (NSA_TQ, 1), 0) + t0
    blk_end = lax.broadcasted_iota(I32, (NSA_TQ, ncp), 1) * CMP_STRIDE + (CMP_BLOCK - 1)
    valid_c = row >= blk_end
    vc = vc_ref[...]
    psum = jnp.zeros((NSA_TQ, ncp), F32)
    for r in range(NSA_REP):
        s = jnp.where(valid_c, _dot_nt(qa_scr[heads[r], :], kc_aug), NEG_INF)
        mc = jnp.max(s, axis=-1, keepdims=True)
        pc = jnp.where(valid_c, jnp.exp2(s - mc), 0.0)
        p = pc / jnp.maximum(jnp.sum(pc, axis=-1, keepdims=True), 1e-30)
        psum = psum + p
        ob_scr[0, heads[r], :] = _dot(p.astype(BF16), vc)

    per_slc = SLC_BLOCK // CMP_STRIDE
    back = CMP_BLOCK // CMP_STRIDE - 1
    jb = lax.broadcasted_iota(I32, (LANES, ncp), 0)
    nb = lax.broadcasted_iota(I32, (LANES, ncp), 1)
    overlap_t = jnp.where((nb >= per_slc * jb - back) & (nb < per_slc * (jb + 1)), 1.0, 0.0)
    imp_t = _dot_nt(overlap_t, psum, precision=lax.Precision.HIGHEST)
    j = lax.broadcasted_iota(I32, (LANES, NSA_TQ), 0)
    tq = lax.broadcasted_iota(I32, (LANES, NSA_TQ), 1) + t0
    cur = tq // SLC_BLOCK
    forced = (j == 0) | (j == cur) | (j == cur - 1)
    sc_scr[...] = jnp.where(forced, FORCED_SCORE, jnp.where(j * SLC_BLOCK <= tq, imp_t, -1.0))

    n_slab = n_slc // SUBLANES
    slabs = [sc_scr[SUBLANES * v:SUBLANES * (v + 1), :] for v in range(n_slab)]
    ranks = [jnp.zeros((SUBLANES, NSA_TQ), F32) for _ in range(n_slab)]
    sub = lax.broadcasted_iota(I32, (SUBLANES, NSA_TQ), 0)
    for i in range(n_slc):
        rival = jnp.broadcast_to(sc_scr[i:i + 1, :], (SUBLANES, NSA_TQ))
        for v in range(n_slab):
            wins_ties = jnp.where(rival >= slabs[v], 1.0, 0.0)
            loses_ties = jnp.where(rival > slabs[v], 1.0, 0.0)
            if SUBLANES * v > i:
                ranks[v] = ranks[v] + wins_ties
            elif SUBLANES * (v + 1) <= i:
                ranks[v] = ranks[v] + loses_ties
            else:
                ranks[v] = ranks[v] + jnp.where(sub + SUBLANES * v > i, wins_ties, loses_ties)
    n_top = min(SLC_TOP_N, n_slc)
    pen_rows = [jnp.zeros((SEL_COL0, NSA_TQ), F32)]
    pen_rows += [jnp.where(rk < n_top, 0.0, -MASK_BIG) for rk in ranks]
    if SEL_COL0 + n_slc < LANES:
        pen_rows.append(jnp.zeros((LANES - SEL_COL0 - n_slc, NSA_TQ), F32))
    penalty = jnp.concatenate(pen_rows, axis=0).T
    for r in range(NSA_REP):
        qa_scr[heads[r], HEAD_DIM:] = (alibi[r] + penalty).astype(BF16)

    def tile_rows(kt):
        return pl.ds(pl.multiple_of(kt * ATT_TK, ATT_TK), ATT_TK)

    def k_aug(kt, k_ref, with_blocks):
        extra = rel_ref[pl.ds(pl.multiple_of(kt * ATT_TK - t0 + seq, LANES), ATT_TK), :]
        if with_blocks:
            extra = extra + oh_ref[tile_rows(kt), :]
        return jnp.concatenate([k_ref[tile_rows(kt), :], extra], axis=1)

    def shared_bias(tab_ref, index):
        return lambda kt: (lambda rb, bias=tab_ref[index(kt)]: bias)

    last_kt = t0 // ATT_TK
    tile_shift = lambda kt: (t0 - kt * ATT_TK) // LANES
    unmasked = ctab_ref.shape[0] - 1

    _softmax_init(m_scr, l_scr, acc_scr)
    _attend(0, last_kt + 1, qa_scr, s_bufs, p_scr, m_scr, l_scr, acc_scr,
            lambda kt: k_aug(kt, ks_ref, True), lambda kt: vs_ref[tile_rows(kt), :],
            shared_bias(ctab_ref, lambda kt: jnp.where(kt == last_kt, tile_shift(kt), unmasked)))
    ob_scr[1] = acc_scr[...] / l_scr[...]

    _softmax_init(m_scr, l_scr, acc_scr)
    first_kt = jnp.maximum(t0 - (WIN_SIZE - 1), 0) // ATT_TK
    _attend(first_kt, last_kt + 1, qa_scr, s_bufs, p_scr, m_scr, l_scr, acc_scr,
            lambda kt: k_aug(kt, kw_ref, False), lambda kt: vw_ref[tile_rows(kt), :],
            shared_bias(wtab_ref, tile_shift))

    gate = _sigmoid(gt_ref[...].astype(F32))
    for r in range(NSA_REP):
        o_win = acc_scr[heads[r], :] / l_scr[heads[r], :]
        o = (gate[:, 3 * r:3 * r + 1] * ob_scr[0, heads[r], :] + gate[:, 3 * r + 1:3 * r + 2] * ob_scr[1, heads[r], :]
             + gate[:, 3 * r + 2:3 * r + 3] * o_win)
        o_ref[:, r * HEAD_DIM:(r + 1) * HEAD_DIM] = o.astype(o_ref.dtype)


def _nsa_mixer(slopes, proj, cmp_kv, gates_t, rel_tab, batch, seq):
    t = proj.shape[0]
    nq = seq // NSA_TQ
    ncp = cmp_kv.shape[2]
    n_slc = seq // SLC_BLOCK
    assert SEL_COL0 + n_slc <= LANES and n_slc % SUBLANES == 0 and seq % ATT_TK == 0
    rows = NSA_REP * NSA_TQ
    win_tab = _mask_table((WIN_SIZE - 1 + ATT_TK - 1) // LANES + 1, lambda d: (d >= 0) & (d <= WIN_SIZE - 1))
    causal_tab = jnp.concatenate([_mask_table(ATT_TK // LANES, lambda d: d >= 0),
                                  jnp.zeros((1, LANES, ATT_TK), F32)])

    def seq_spec(cb):
        return pl.BlockSpec((seq, HEAD_DIM), lambda b, g, i: (b, cb + g))

    def cmp_spec(which):
        return pl.BlockSpec((None, None, ncp, HEAD_DIM), lambda b, g, i: (which, b * N_NSA_KV + g, 0, 0))

    def whole(a):
        return pl.BlockSpec(a.shape, lambda b, g, i: (0,) * a.ndim)

    qo_spec = pl.BlockSpec((NSA_TQ, NSA_REP * HEAD_DIM), lambda b, g, i: (b * nq + i, g))
    onehot = _block_onehot_table(seq)
    qx = _alibi_rows(slopes)
    return pl.pallas_call(
        functools.partial(_nsa_kernel, seq=seq),
        out_shape=jax.ShapeDtypeStruct((t, Q_NSA_DIM), BF16),
        grid=(batch, N_NSA_KV, nq),
        in_specs=[
            qo_spec,
            pl.BlockSpec((NSA_REP, SUBLANES, LANES), lambda b, g, i: (g, 0, 0)),
            cmp_spec(0), cmp_spec(1),
            seq_spec(CB_K_SLC), seq_spec(CB_V_SLC), seq_spec(CB_K_WIN), seq_spec(CB_V_WIN),
            whole(rel_tab), whole(onehot), whole(win_tab), whole(causal_tab),
            pl.BlockSpec((None, NSA_TQ, 3 * NSA_REP), lambda b, g, i: (g, b * nq + i, 0)),
        ],
        out_specs=qo_spec,
        scratch_shapes=[
            pltpu.VMEM((rows, HEAD_DIM + LANES), BF16),
            pltpu.VMEM((rows, ATT_TK), F32),
            pltpu.VMEM((rows, ATT_TK), F32),
            pltpu.VMEM((rows, ATT_TK), BF16),
            pltpu.VMEM((rows, LANES), F32),
            pltpu.VMEM((rows, LANES), F32),
            pltpu.VMEM((rows, HEAD_DIM), F32),
            pltpu.VMEM((2, rows, HEAD_DIM), F32),
            pltpu.VMEM((LANES, NSA_TQ), F32),
        ],
        compiler_params=_params("arbitrary", "arbitrary", "arbitrary"),
        name="nsa_mixer",
    )(proj, qx, cmp_kv, cmp_kv, proj, proj, proj, proj, rel_tab, onehot, win_tab, causal_tab, gates_t)


def _dil_kernel(q_ref, qx_ref, k_ref, v_ref, rel_ref, mtab_ref, o_ref,
                qa_scr, s_a_scr, s_b_scr, p_scr, m_scr, l_scr, acc_scr, *, seq):
    t0 = pl.program_id(2) * DIL_TQ
    qa_scr[:, :HEAD_DIM] = q_ref[...]
    qa_scr[:, HEAD_DIM:] = jnp.broadcast_to(qx_ref[0:1, :], (DIL_TQ, LANES)).astype(BF16)
    _softmax_init(m_scr, l_scr, acc_scr)
    masked = mtab_ref.shape[0] - 1

    def tile_rows(kt):
        return pl.ds(pl.multiple_of(kt * ATT_TK, ATT_TK), ATT_TK)

    def k_aug(kt):
        rel = rel_ref[pl.ds(pl.multiple_of(kt * ATT_TK - t0 + seq, LANES), ATT_TK), :]
        return jnp.concatenate([k_ref[tile_rows(kt), :], rel], axis=1)

    def multiplicity(kt):
        def for_block(rb):
            u = (t0 + rb * LANES - kt * ATT_TK) // LANES
            return mtab_ref[jnp.where((u < 0) | (u >= masked), masked, u)]
        return for_block

    reach = max(w for w, _ in DIL_PATTERNS)
    first_kt = jnp.maximum(t0 - reach, 0) // ATT_TK
    _attend(first_kt, (t0 + DIL_TQ - 1) // ATT_TK + 1, qa_scr, (s_a_scr, s_b_scr), p_scr, m_scr, l_scr, acc_scr,
            k_aug, lambda kt: v_ref[tile_rows(kt), :], multiplicity)
    o_ref[...] = (acc_scr[...] / l_scr[...]).astype(o_ref.dtype)


def _dil_mixer(slopes, proj, rel_tab, batch, seq):
    t = proj.shape[0]
    nq = seq // DIL_TQ
    mult_tab = _dilation_table()
    qx = _alibi_rows(slopes)

    def whole(a):
        return pl.BlockSpec(a.shape, lambda b, h, i: (0,) * a.ndim)

    return pl.pallas_call(
        functools.partial(_dil_kernel, seq=seq),
        out_shape=jax.ShapeDtypeStruct((t, DIL_DIM), BF16),
        grid=(batch, N_DIL_HEADS, nq),
        in_specs=[
            pl.BlockSpec((DIL_TQ, HEAD_DIM), lambda b, h, i: (b * nq + i, CB_Q_DIL + h)),
            pl.BlockSpec((None, SUBLANES, LANES), lambda b, h, i: (h, 0, 0)),
            pl.BlockSpec((seq, HEAD_DIM), lambda b, h, i: (b, CB_K_DIL + h)),
            pl.BlockSpec((seq, HEAD_DIM), lambda b, h, i: (b, CB_V_DIL + h)),
            whole(rel_tab), whole(mult_tab),
        ],
        out_specs=pl.BlockSpec((DIL_TQ, HEAD_DIM), lambda b, h, i: (b * nq + i, h)),
        scratch_shapes=[
            pltpu.VMEM((DIL_TQ, HEAD_DIM + LANES), BF16),
            pltpu.VMEM((DIL_TQ, ATT_TK), F32),
            pltpu.VMEM((DIL_TQ, ATT_TK), F32),
            pltpu.VMEM((DIL_TQ, ATT_TK), BF16),
            pltpu.VMEM((DIL_TQ, LANES), F32),
            pltpu.VMEM((DIL_TQ, LANES), F32),
            pltpu.VMEM((DIL_TQ, HEAD_DIM), F32),
        ],
        compiler_params=_params("arbitrary", "arbitrary", "arbitrary"),
        name="dil_mixer",
    )(proj, qx, proj, proj, rel_tab, mult_tab)


def _out_proj_kernel(a_ref, b_ref, ga_ref, gb_ref, w_ref, res_ref, o_ref, h_scr):
    @pl.when(pl.program_id(1) == 0)
    def _():
        na = a_ref.shape[1]
        h_scr[:, :na] = _rms_rows(a_ref[...].astype(F32), ga_ref[...]).astype(BF16)
        h_scr[:, na:] = _rms_rows(b_ref[...].astype(F32), gb_ref[...]).astype(BF16)

    o_ref[...] = res_ref[...] + _dot(h_scr[...], w_ref[...])


def _out_proj(o_nsa, o_dil, g_nsa, g_dil, w, resid):
    t, d = resid.shape
    na, nb = o_nsa.shape[1], o_dil.shape[1]
    return pl.pallas_call(
        _out_proj_kernel,
        out_shape=jax.ShapeDtypeStruct((t, d), F32),
        grid=(t // PROJ_TM, d // PROJ_TN),
        in_specs=[
            pl.BlockSpec((PROJ_TM, na), lambda i, j: (i, 0)),
            pl.BlockSpec((PROJ_TM, nb), lambda i, j: (i, 0)),
            pl.BlockSpec((1, na), lambda i, j: (0, 0)),
            pl.BlockSpec((1, nb), lambda i, j: (0, 0)),
            pl.BlockSpec((na + nb, PROJ_TN), lambda i, j: (0, j)),
            pl.BlockSpec((PROJ_TM, PROJ_TN), lambda i, j: (i, j)),
        ],
        out_specs=pl.BlockSpec((PROJ_TM, PROJ_TN), lambda i, j: (i, j)),
        scratch_shapes=[pltpu.VMEM((PROJ_TM, na + nb), BF16)],
        compiler_params=_params("parallel", "arbitrary"),
        name="out_proj",
    )(o_nsa, o_dil, g_nsa, g_dil, w, resid)


def _router_kernel(x_ref, g_ref, wr_ref, br_ref, hn_ref, idx_ref, tw_ref):
    h = _rms_rows(x_ref[...], g_ref[...])
    hn_ref[...] = h
    logits = _dot(h, wr_ref[...], precision=lax.Precision.HIGHEST) + br_ref[...]
    lane = lax.broadcasted_iota(I32, logits.shape, 1)
    lanef = lane.astype(F32)
    work = jnp.where(lane < N_EXPERTS, logits, -jnp.inf)
    vals, ids = [], []
    for _ in range(TOP_K):
        mx = jnp.max(work, axis=-1, keepdims=True)
        first = jnp.min(jnp.where(work == mx, lanef, float(LANES)), axis=-1, keepdims=True)
        vals.append(mx)
        ids.append(first)
        work = jnp.where(lanef == first, -jnp.inf, work)
    es = [jnp.exp(v - vals[0]) for v in vals]
    den = functools.reduce(lambda a, b: a + b, es)
    idx_out = jnp.zeros(logits.shape, F32)
    tw_out = jnp.zeros(logits.shape, F32)
    for k in range(TOP_K):
        idx_out = jnp.where(lane == k, ids[k], idx_out)
        tw_out = jnp.where(lane == k, es[k] / den, tw_out)
    idx_ref[...] = idx_out.astype(I32)
    tw_ref[...] = tw_out


def _router(x1, gain, wr, br):
    t, d = x1.shape
    return pl.pallas_call(
        _router_kernel,
        out_shape=(jax.ShapeDtypeStruct((t, d), F32),
                   jax.ShapeDtypeStruct((t, LANES), I32),
                   jax.ShapeDtypeStruct((t, LANES), F32)),
        grid=(t // ROUTER_TM,),
        in_specs=[
            pl.BlockSpec((ROUTER_TM, d), lambda i: (i, 0)),
            pl.BlockSpec((1, d), lambda i: (0, 0)),
            pl.BlockSpec((d, LANES), lambda i: (0, 0)),
            pl.BlockSpec((1, LANES), lambda i: (0, 0)),
        ],
        out_specs=(pl.BlockSpec((ROUTER_TM, d), lambda i: (i, 0)),
                   pl.BlockSpec((ROUTER_TM, LANES), lambda i: (i, 0)),
                   pl.BlockSpec((ROUTER_TM, LANES), lambda i: (i, 0))),
        compiler_params=_params("parallel"),
        name="moe_router",
    )(x1, gain, wr, br)


def _rank_kernel(idx_ref, rank_ref, cnt_ref, carry_scr):
    @pl.when(pl.program_id(0) == 0)
    def _():
        carry_scr[...] = jnp.zeros(carry_scr.shape, F32)

    idx = idx_ref[...]
    lane = lax.broadcasted_iota(I32, idx.shape, 1)
    hits = [lane == idx[:, k:k + 1] for k in range(TOP_K)]
    onehot = functools.reduce(lambda a, b: a + b, [jnp.where(h, 1.0, 0.0) for h in hits])
    ri = lax.broadcasted_iota(I32, (RANK_TM, RANK_TM), 0)
    ci = lax.broadcasted_iota(I32, (RANK_TM, RANK_TM), 1)
    before = jnp.where(ci < ri, 1.0, 0.0).astype(BF16)
    rank = _dot(before, onehot.astype(BF16)) + carry_scr[0:1, :]
    out = jnp.zeros(idx.shape, F32)
    for k in range(TOP_K):
        mine = jnp.sum(jnp.where(hits[k], rank, 0.0), axis=-1, keepdims=True)
        out = jnp.where(lane == k, mine, out)
    rank_ref[...] = out.astype(I32)
    carry = carry_scr[...] + jnp.sum(onehot, axis=0, keepdims=True)
    carry_scr[...] = carry
    cnt_ref[...] = carry


def _rank(idx):
    t = idx.shape[0]
    return pl.pallas_call(
        _rank_kernel,
        out_shape=(jax.ShapeDtypeStruct((t, LANES), I32), jax.ShapeDtypeStruct((SUBLANES, LANES), F32)),
        grid=(t // RANK_TM,),
        in_specs=[pl.BlockSpec((RANK_TM, LANES), lambda i: (i, 0))],
        out_specs=(pl.BlockSpec((RANK_TM, LANES), lambda i: (i, 0)),
                   pl.BlockSpec((SUBLANES, LANES), lambda i: (0, 0))),
        scratch_shapes=[pltpu.VMEM((SUBLANES, LANES), F32)],
        compiler_params=_params("arbitrary"),
        name="moe_rank",
    )(idx)


def _pos_kernel(cnt_ref, idx_ref, rank_ref, pos_ref, vt_ref):
    cnt = cnt_ref[0:1, :]
    lane1 = lax.broadcasted_iota(I32, (1, LANES), 1)
    tiles = jnp.where(lane1 < N_EXPERTS, jnp.floor((cnt + (MOE_TM - 1)) * (1.0 / MOE_TM)), 0.0)
    tiles_before = jnp.zeros((1, LANES), F32)
    rows_before = jnp.zeros((1, LANES), F32)
    for e in range(N_EXPERTS):
        tiles_before = tiles_before + jnp.where(lane1 > e, tiles[:, e:e + 1], 0.0)
        rows_before = rows_before + jnp.where(lane1 > e, cnt[:, e:e + 1], 0.0)
    tiles_through = tiles_before + tiles

    idx = idx_ref[...]
    rank = rank_ref[...].astype(F32)
    lane = lax.broadcasted_iota(I32, idx.shape, 1)
    out = jnp.zeros(idx.shape, F32)
    for k in range(TOP_K):
        mine = lane == idx[:, k:k + 1]
        row0 = jnp.sum(jnp.where(mine, tiles_before * MOE_TM, 0.0), axis=-1, keepdims=True)
        slot0 = jnp.sum(jnp.where(mine, rows_before, 0.0), axis=-1, keepdims=True)
        out = jnp.where(lane == k, row0 + rank[:, k:k + 1], out)
        out = jnp.where(lane == TOP_K + k, slot0 + rank[:, k:k + 1], out)
    pos_ref[...] = out.astype(I32)

    nv = vt_ref.shape[0]
    vl = lax.broadcasted_iota(I32, (nv, LANES), 1)
    v = lax.broadcasted_iota(I32, (nv, LANES), 0).astype(F32)
    total = jnp.sum(tiles, axis=-1, keepdims=True)
    vv = jnp.minimum(v, total - 1.0)
    done = jnp.where((tiles_through <= vv) & (vl < N_EXPERTS), 1.0, 0.0)
    e_v = jnp.minimum(jnp.sum(done, axis=-1, keepdims=True), float(N_EXPERTS - 1))
    mine = vl.astype(F32) == e_v
    first_tile = jnp.sum(jnp.where(mine, tiles_before, 0.0), axis=-1, keepdims=True)
    e_cnt = jnp.sum(jnp.where(mine, cnt, 0.0), axis=-1, keepdims=True)
    e_slot = jnp.sum(jnp.where(mine, rows_before, 0.0), axis=-1, keepdims=True)
    vv1 = vv[:, 0:1]
    valid = v[:, 0:1] < total
    offset = (vv1 - first_tile) * MOE_TM
    held = jnp.where(valid, jnp.clip(e_cnt - offset, 0.0, float(MOE_TM)), 0.0)
    nrows = jnp.ceil(held * (1.0 / MOE_SUB)) * MOE_SUB
    cols = [e_v, vv1, nrows, jnp.where(valid, 1.0, 0.0), e_slot + offset, held]
    table = jnp.zeros((nv, LANES), F32)
    for c, val in enumerate(cols):
        table = jnp.where(vl == c, val, table)
    vt_ref[...] = table.astype(I32)


def _positions(cnt, idx, rank, n_visits):
    t = idx.shape[0]
    nvp = -(-n_visits // SUBLANES) * SUBLANES
    return pl.pallas_call(
        _pos_kernel,
        out_shape=(jax.ShapeDtypeStruct((t, LANES), I32), jax.ShapeDtypeStruct((nvp, LANES), I32)),
        grid=(t // POS_TM,),
        in_specs=[pl.BlockSpec((SUBLANES, LANES), lambda i: (0, 0)),
                  pl.BlockSpec((POS_TM, LANES), lambda i: (i, 0)),
                  pl.BlockSpec((POS_TM, LANES), lambda i: (i, 0))],
        out_specs=(pl.BlockSpec((POS_TM, LANES), lambda i: (i, 0)),
                   pl.BlockSpec((nvp, LANES), lambda i: (0, 0))),
        compiler_params=_params("arbitrary"),
        name="moe_positions",
    )(cnt, idx, rank)


def _invert_kernel(slot_ref, tok_ref):
    def body(i, carry):
        tok_ref[slot_ref[i]] = i // TOP_K
        return carry

    lax.fori_loop(0, slot_ref.shape[0], body, 0, unroll=INVERT_UNROLL)


def _invert(slot_flat):
    return pl.pallas_call(
        _invert_kernel,
        out_shape=jax.ShapeDtypeStruct(slot_flat.shape, I32),
        in_specs=[pl.BlockSpec(memory_space=pltpu.SMEM)],
        out_specs=pl.BlockSpec(memory_space=pltpu.SMEM),
        name="moe_invert",
    )(slot_flat)


def _dispatch_kernel(vt_ref, tok_ref, hn_ref, xs_ref, buf, sems):
    i = pl.program_id(0)
    per_visit = MOE_TM // DISPATCH_TM

    def held_rows(step):
        v = step // per_visit
        part = (step % per_visit) * DISPATCH_TM
        return jnp.clip(vt_ref[v * VT_W + 5] - part, 0, DISPATCH_TM), vt_ref[v * VT_W + 4] + part

    def row_copy(src_row, r, slot):
        return pltpu.make_async_copy(hn_ref.at[pl.ds(src_row, 1)], buf.at[slot, pl.ds(r, 1)], sems.at[slot])

    def start(step, slot):
        n, base = held_rows(step)
        issue = lambda r, c: (row_copy(tok_ref[base + r], r, slot).start(), c)[1]

        @pl.when(n == DISPATCH_TM)
        def _():
            lax.fori_loop(0, DISPATCH_TM, issue, 0, unroll=GATHER_UNROLL)

        @pl.when(n < DISPATCH_TM)
        def _():
            lax.fori_loop(0, n, issue, 0)

    def wait(step, slot):
        n, _ = held_rows(step)

        @pl.when(n == DISPATCH_TM)
        def _():
            pltpu.make_async_copy(hn_ref.at[pl.ds(0, DISPATCH_TM)], buf.at[slot], sems.at[slot]).wait()

        @pl.when(n < DISPATCH_TM)
        def _():
            lax.fori_loop(0, n, lambda r, c: (row_copy(0, r, slot).wait(), c)[1], 0)

    @pl.when(i == 0)
    def _():
        buf[...] = jnp.zeros(buf.shape, buf.dtype)
        start(0, 0)

    @pl.when(i + 1 < pl.num_programs(0))
    def _():
        start(i + 1, (i + 1) % 2)

    wait(i, i % 2)
    n, _ = held_rows(i)
    rowid = lax.broadcasted_iota(I32, (DISPATCH_TM, 1), 0)
    xs_ref[...] = jnp.where(rowid < n, buf[i % 2], 0.0).astype(xs_ref.dtype)


def _dispatch(vt, tok, hn, n_visits):
    d = hn.shape[1]
    return pl.pallas_call(
        _dispatch_kernel,
        out_shape=jax.ShapeDtypeStruct((n_visits * MOE_TM, d), BF16),
        grid_spec=pltpu.PrefetchScalarGridSpec(
            num_scalar_prefetch=2,
            grid=(n_visits * MOE_TM // DISPATCH_TM,),
            in_specs=[pl.BlockSpec(memory_space=pl.ANY)],
            out_specs=pl.BlockSpec((DISPATCH_TM, d), lambda i, vt, tok: (i, 0)),
            scratch_shapes=[pltpu.VMEM((2, DISPATCH_TM, d), F32), pltpu.SemaphoreType.DMA((2,))],
        ),
        compiler_params=_params("arbitrary"),
        name="moe_dispatch",
    )(vt, tok, hn)


def _moe_up_kernel(vt_ref, xs_ref, wg_ref, wu_ref, bg_ref, bu_ref, h_ref, wg_scr, wu_scr):
    nrows = vt_ref[pl.program_id(0) * VT_W + 2]

    for sb in range(MOE_TM // MOE_SUB):
        rs = slice(sb * MOE_SUB, (sb + 1) * MOE_SUB)

        @pl.when(sb * MOE_SUB < nrows)
        def _(sb=sb, rs=rs):
            if sb == 0:
                wg_scr[...] = wg_ref[...].astype(BF16)
                wu_scr[...] = wu_ref[...].astype(BF16)
            x = xs_ref[rs, :]
            gate = jnp.minimum(_dot(x, wg_scr[...]) + bg_ref[...], SWIGLU_LIMIT)
            up = jnp.clip(_dot(x, wu_scr[...]) + bu_ref[...], -SWIGLU_LIMIT, SWIGLU_LIMIT)
            act = (up + 1.0) * (gate * _sigmoid(SWIGLU_ALPHA * gate))
            h_ref[rs, :] = act.astype(h_ref.dtype)

        @pl.when(sb * MOE_SUB >= nrows)
        def _():
            h_ref[rs, :] = jnp.zeros((MOE_SUB, h_ref.shape[1]), h_ref.dtype)


def _moe_up(vt, xs, w_gate_up, b_gate_up, n_visits):
    d = w_gate_up.shape[1]
    nf = D_EXPERT // MOE_TF

    def fsel(v, f, vt):
        return jnp.where(vt[v * VT_W + 3] > 0, f, nf - 1)

    return pl.pallas_call(
        _moe_up_kernel,
        out_shape=jax.ShapeDtypeStruct((n_visits * MOE_TM, D_EXPERT), BF16),
        grid_spec=pltpu.PrefetchScalarGridSpec(
            num_scalar_prefetch=1,
            grid=(n_visits, nf),
            in_specs=[
                pl.BlockSpec((MOE_TM, d), lambda v, f, vt: (vt[v * VT_W + 1], 0)),
                pl.BlockSpec((None, d, MOE_TF), lambda v, f, vt: (vt[v * VT_W], 0, fsel(v, f, vt))),
                pl.BlockSpec((None, d, MOE_TF), lambda v, f, vt: (vt[v * VT_W], 0, nf + fsel(v, f, vt))),
                pl.BlockSpec((None, 1, MOE_TF), lambda v, f, vt: (vt[v * VT_W], 0, fsel(v, f, vt))),
                pl.BlockSpec((None, 1, MOE_TF), lambda v, f, vt: (vt[v * VT_W], 0, nf + fsel(v, f, vt))),
            ],
            out_specs=pl.BlockSpec((MOE_TM, MOE_TF), lambda v, f, vt: (v, f)),
            scratch_shapes=[pltpu.VMEM((d, MOE_TF), BF16), pltpu.VMEM((d, MOE_TF), BF16)],
        ),
        compiler_params=_params("arbitrary", "arbitrary"),
        name="moe_up",
    )(vt, xs, w_gate_up, w_gate_up, b_gate_up, b_gate_up)


def _moe_down_kernel(vt_ref, h_ref, wd_ref, bd_ref, y_ref, wd_scr):
    nrows = vt_ref[pl.program_id(0) * VT_W + 2]

    for sb in range(MOE_TM // MOE_SUB):
        rs = slice(sb * MOE_SUB, (sb + 1) * MOE_SUB)

        @pl.when(sb * MOE_SUB < nrows)
        def _(sb=sb, rs=rs):
            if sb == 0:
                wd_scr[...] = wd_ref[...].astype(BF16)
            y_ref[rs, :] = _dot(h_ref[rs, :], wd_scr[...]) + bd_ref[...]

        @pl.when(sb * MOE_SUB >= nrows)
        def _():
            y_ref[rs, :] = jnp.zeros((MOE_SUB, y_ref.shape[1]), y_ref.dtype)


def _moe_down(vt, h, w_down, b_down, n_visits):
    d = w_down.shape[2]
    nj = d // MOE_TN

    def jsel(v, j, vt):
        return jnp.where(vt[v * VT_W + 3] > 0, j, nj - 1)

    return pl.pallas_call(
        _moe_down_kernel,
        out_shape=jax.ShapeDtypeStruct((n_visits * MOE_TM, d), F32),
        grid_spec=pltpu.PrefetchScalarGridSpec(
            num_scalar_prefetch=1,
            grid=(n_visits, nj),
            in_specs=[
                pl.BlockSpec((MOE_TM, D_EXPERT), lambda v, j, vt: (vt[v * VT_W + 1], 0)),
                pl.BlockSpec((None, D_EXPERT, MOE_TN), lambda v, j, vt: (vt[v * VT_W], 0, jsel(v, j, vt))),
                pl.BlockSpec((None, 1, MOE_TN), lambda v, j, vt: (vt[v * VT_W], 0, jsel(v, j, vt))),
            ],
            out_specs=pl.BlockSpec((MOE_TM, MOE_TN), lambda v, j, vt: (v, j)),
            scratch_shapes=[pltpu.VMEM((D_EXPERT, MOE_TN), BF16)],
        ),
        compiler_params=_params("arbitrary", "arbitrary"),
        name="moe_down",
    )(vt, h, w_down, b_down)


def _combine_kernel(pos_ref, ys_ref, x_ref, tw_ref, o_ref, buf, sems):
    i = pl.program_id(0)

    def start(step, slot):
        def issue(r, carry):
            for k in range(TOP_K):
                src = pos_ref[(step * COMBINE_TM + r) * TOP_K + k]
                pltpu.make_async_copy(ys_ref.at[pl.ds(src, 1)], buf.at[slot, k, pl.ds(r, 1)], sems.at[slot]).start()
            return carry
        lax.fori_loop(0, COMBINE_TM, issue, 0, unroll=GATHER_UNROLL)

    def wait(slot):
        for k in range(TOP_K):
            pltpu.make_async_copy(ys_ref.at[pl.ds(0, COMBINE_TM)], buf.at[slot, k], sems.at[slot]).wait()

    @pl.when(i == 0)
    def _():
        start(0, 0)

    @pl.when(i + 1 < pl.num_programs(0))
    def _():
        start(i + 1, (i + 1) % 2)

    wait(i % 2)
    tw = tw_ref[...]
    acc = x_ref[...]
    for k in range(TOP_K):
        acc = acc + tw[:, k:k + 1] * buf[i % 2, k]
    o_ref[...] = acc


def _combine(pos_flat, ys, x1, tw):
    t, d = x1.shape
    return pl.pallas_call(
        _combine_kernel,
        out_shape=jax.ShapeDtypeStruct((t, d), F32),
        grid_spec=pltpu.PrefetchScalarGridSpec(
            num_scalar_prefetch=1,
            grid=(t // COMBINE_TM,),
            in_specs=[pl.BlockSpec(memory_space=pl.ANY),
                      pl.BlockSpec((COMBINE_TM, d), lambda i, pos: (i, 0)),
                      pl.BlockSpec((COMBINE_TM, LANES), lambda i, pos: (i, 0))],
            out_specs=pl.BlockSpec((COMBINE_TM, d), lambda i, pos: (i, 0)),
            scratch_shapes=[pltpu.VMEM((2, TOP_K, COMBINE_TM, d), F32), pltpu.SemaphoreType.DMA((2,))],
        ),
        compiler_params=_params("arbitrary"),
        name="moe_combine",
    )(pos_flat, ys, x1, tw)


def _alibi_slopes():
    n = N_NSA_HEADS + N_DIL_HEADS
    i = jnp.arange(1, n + 1, dtype=F32)
    return jnp.exp2(-8.0 * i / n)


def _attention_block(x2, batch, seq, attn_norm, w_in, pe_k, w_k1, w_k2, pe_v, w_v1, w_v2,
                     qn_nsa, kn_cmp, kn_slc, kn_win, qn_dil, kn_dil, on_nsa, on_dil, w_out):
    d = x2.shape[1]
    a_dim = Q_NSA_DIM + 6 * KV_NSA_DIM
    d_off = a_dim + GATE_DIM
    n_pad = N_PROJ - (a_dim + 3 * DIL_DIM + GATE_DIM)
    w = jnp.concatenate([w_in[:, :a_dim], w_in[:, d_off:], w_in[:, a_dim:d_off],
                         jnp.zeros((d, n_pad), w_in.dtype)], axis=1).astype(BF16)
    ones_kv = jnp.ones((KV_NSA_DIM,), F32)
    ones_dil = jnp.ones((DIL_DIM,), F32)
    tail = jnp.ones((GATE_DIM + n_pad,), F32)
    q_scale = ATTN_SCALE * LOG2E
    col_gain = jnp.concatenate([
        jnp.tile(qn_nsa, N_NSA_HEADS) * q_scale, ones_kv, ones_kv, jnp.tile(kn_slc, N_NSA_KV), ones_kv,
        jnp.tile(kn_win, N_NSA_KV), ones_kv, jnp.tile(qn_dil, N_DIL_HEADS) * q_scale,
        jnp.tile(kn_dil, N_DIL_HEADS), ones_dil, tail])[None, :]
    col_flag = jnp.concatenate([
        jnp.ones((Q_NSA_DIM,), F32), 0 * ones_kv, 0 * ones_kv, ones_kv, 0 * ones_kv, ones_kv, 0 * ones_kv,
        ones_dil, ones_dil, 0 * ones_dil, 0 * tail])[None, :]
    proj = _in_proj(x2, attn_norm[None, :], w, col_gain, col_flag)

    n_chunks = seq // CMP_STRIDE
    kv = proj[:, CB_K_CMP * LANES:(CB_V_CMP + N_NSA_KV) * LANES]
    kv = kv.reshape(batch, n_chunks, CMP_STRIDE, 2, N_NSA_KV, HEAD_DIM).transpose(3, 0, 4, 1, 2, 5)
    chunks = kv.reshape(2, batch * N_NSA_KV, n_chunks, CMP_STRIDE * HEAD_DIM)
    assert CMP_BLOCK == 2 * CMP_STRIDE
    blocks = jnp.concatenate([chunks[:, :, :-1], chunks[:, :, 1:]], axis=-1)
    blocks = jnp.pad(blocks, ((0, 0), (0, 0), (0, 1), (0, 0)))
    pe = jnp.stack([pe_k.reshape(1, -1), pe_v.reshape(1, -1)])
    w1 = jnp.stack([w_k1, w_v1]).astype(BF16)
    w2 = jnp.stack([w_k2, w_v2]).astype(BF16)
    cmp_kv = _compress(blocks, pe, w1, w2, kn_cmp[None, :])

    slopes = _alibi_slopes()
    rel_tab = _rel_position_table(seq)
    gates = proj[:, CB_GATE * LANES:CB_GATE * LANES + GATE_DIM]
    gates_t = gates.reshape(-1, N_NSA_KV, 3 * NSA_REP).transpose(1, 0, 2)
    o_nsa = _nsa_mixer(slopes[0::2], proj, cmp_kv, gates_t, rel_tab, batch, seq)
    o_dil = _dil_mixer(slopes[1::2], proj, rel_tab, batch, seq)
    return _out_proj(o_nsa, o_dil, on_nsa[None, :], on_dil[None, :], w_out.astype(BF16), x2)


def _moe_block(x1, ffn_norm, w_router, b_router, w_gate_up, b_gate_up, w_down, b_down):
    t, d = x1.shape
    wr = jnp.pad(w_router, ((0, 0), (0, LANES - N_EXPERTS)))
    br = jnp.pad(b_router, (0, LANES - N_EXPERTS))[None, :]
    hn, idx, tw = _router(x1, ffn_norm[None, :], wr, br)
    rank, cnt = _rank(idx)
    n_visits = N_EXPERTS + (t * TOP_K) // MOE_TM
    pos, table = _positions(cnt, idx, rank, n_visits)
    pos_flat = pos[:, :TOP_K].reshape(-1)
    tok = _invert(pos[:, TOP_K:2 * TOP_K].reshape(-1))
    vt = table[:n_visits, :VT_W].reshape(-1)
    xs = _dispatch(vt, tok, hn, n_visits)
    h = _moe_up(vt, xs, w_gate_up, b_gate_up[:, None, :], n_visits)
    ys = _moe_down(vt, h, w_down, b_down[:, None, :], n_visits)
    return _combine(pos_flat, ys, x1, tw)


def kernel(x, attn_norm, w_in, cmp_pos_k, w_cmp_k1, w_cmp_k2, cmp_pos_v, w_cmp_v1, w_cmp_v2, q_norm_nsa, k_norm_cmp, k_norm_slc, k_norm_win, q_norm_dil, k_norm_dil, out_norm_nsa, out_norm_dil, w_out, ffn_norm, w_router, b_router, w_gate_up, b_gate_up, w_down, b_down):
    batch, seq, d = x.shape
    x2 = x.reshape(batch * seq, d)
    for layer in range(attn_norm.shape[0]):
        x2 = _attention_block(
            x2, batch, seq, attn_norm[layer], w_in[layer], cmp_pos_k[layer], w_cmp_k1[layer],
            w_cmp_k2[layer], cmp_pos_v[layer], w_cmp_v1[layer], w_cmp_v2[layer], q_norm_nsa[layer],
            k_norm_cmp[layer], k_norm_slc[layer], k_norm_win[layer], q_norm_dil[layer],
            k_norm_dil[layer], out_norm_nsa[layer], out_norm_dil[layer], w_out[layer])
        x2 = _moe_block(x2, ffn_norm[layer], w_router[layer], b_router[layer], w_gate_up[layer],
                        b_gate_up[layer], w_down[layer], b_down[layer])
    return x2.reshape(batch, seq, d)
```

```python
import functools

import numpy as np
import jax
import jax.numpy as jnp
from jax import lax
from jax.experimental import pallas as pl
from jax.experimental.pallas import tpu as pltpu

F32, BF16, I32 = jnp.float32, jnp.bfloat16, jnp.int32

HEAD_DIM = 128
N_NSA_HEADS = 16
N_NSA_KV = 4
NSA_REP = N_NSA_HEADS // N_NSA_KV
N_DIL_HEADS = 16
CMP_BLOCK = 32
CMP_STRIDE = 16
CMP_HIDDEN = 256
SLC_BLOCK = 64
SLC_TOP_N = 16
FORCED_SCORE = 1.0e4
WIN_SIZE = 512
DIL_PATTERNS = ((128, 1), (512, 4), (2048, 16))
N_EXPERTS = 32
TOP_K = 4
D_EXPERT = 1536
SWIGLU_ALPHA = 1.702
SWIGLU_LIMIT = 7.0
RMS_EPS = 1e-6
NEG_INF = -1e30
ATTN_SCALE = HEAD_DIM ** -0.5
LOG2E = 1.4426950408889634

Q_NSA_DIM = N_NSA_HEADS * HEAD_DIM
KV_NSA_DIM = N_NSA_KV * HEAD_DIM
GATE_DIM = N_NSA_HEADS * 3
DIL_DIM = N_DIL_HEADS * HEAD_DIM

LANES = 128
SUBLANES = 8
VMEM_LIMIT = 56 * 1024 * 1024

CB_Q_NSA = 0
CB_K_CMP = CB_Q_NSA + N_NSA_HEADS
CB_V_CMP = CB_K_CMP + N_NSA_KV
CB_K_SLC = CB_V_CMP + N_NSA_KV
CB_V_SLC = CB_K_SLC + N_NSA_KV
CB_K_WIN = CB_V_SLC + N_NSA_KV
CB_V_WIN = CB_K_WIN + N_NSA_KV
CB_Q_DIL = CB_V_WIN + N_NSA_KV
CB_K_DIL = CB_Q_DIL + N_DIL_HEADS
CB_V_DIL = CB_K_DIL + N_DIL_HEADS
CB_GATE = CB_V_DIL + N_DIL_HEADS
PROJ_TN = 512
N_PROJ = -(-(CB_GATE + 1) * LANES // PROJ_TN) * PROJ_TN

PROJ_TM = 512
NSA_TQ = 128
DIL_TQ = 512
ATT_TK = 256
ROUTER_TM = 256
RANK_TM = 512
POS_TM = 1024
MOE_TM = 1536
MOE_SUB = 256
MOE_TF = 256
MOE_TN = 1024
INVERT_UNROLL = 16
GATHER_UNROLL = 8
DISPATCH_TM = 256
COMBINE_TM = 128
VT_W = 8

POS_SHIFT = 8
POS_RADIX = 1 << POS_SHIFT
ALIBI_PIECES = 3
ALIBI_COLS = 2 * ALIBI_PIECES
SEL_COL0 = 64
MASK_BIG = 2.0 ** 100


def _dot(a, b, **kw):
    return jnp.dot(a, b, preferred_element_type=F32, **kw)


def _dot_nt(a, b, **kw):
    return lax.dot_general(a, b, (((1,), (1,)), ((), ())), preferred_element_type=F32, **kw)


def _rms_rows(x, gain):
    ms = jnp.mean(x * x, axis=-1, keepdims=True)
    return x * lax.rsqrt(ms + RMS_EPS) * gain


def _sigmoid(x):
    return 1.0 / (1.0 + jnp.exp(-x))


def _params(*sem):
    return pltpu.CompilerParams(dimension_semantics=sem, vmem_limit_bytes=VMEM_LIMIT)


def _in_proj_kernel(x_ref, g_ref, wa_ref, wb_ref, cg_ref, cf_ref, o_ref, h_scr, *, na_tiles):
    j = pl.program_id(1)

    @pl.when(j == 0)
    def _():
        h_scr[...] = _rms_rows(x_ref[...], g_ref[...]).astype(BF16)

    def tile(w_ref):
        acc = _dot(h_scr[...], w_ref[...])
        for c in range(PROJ_TN // LANES):
            cs = slice(c * LANES, (c + 1) * LANES)
            a = acc[:, cs]
            r = lax.rsqrt(jnp.mean(a * a, axis=-1, keepdims=True) + RMS_EPS)
            y = a * jnp.where(cf_ref[:, cs] > 0, r, 1.0) * cg_ref[:, cs]
            o_ref[:, cs] = y.astype(o_ref.dtype)

    pl.when(j < na_tiles)(functools.partial(tile, wa_ref))
    pl.when(j >= na_tiles)(functools.partial(tile, wb_ref))


def _in_proj(x2, gain, wa, wb, col_gain, col_flag):
    t, d = x2.shape
    na_tiles = wa.shape[1] // PROJ_TN
    n = wa.shape[1] + wb.shape[1]
    return pl.pallas_call(
        functools.partial(_in_proj_kernel, na_tiles=na_tiles),
        out_shape=jax.ShapeDtypeStruct((t, n), BF16),
        grid=(t // PROJ_TM, n // PROJ_TN),
        in_specs=[
            pl.BlockSpec((PROJ_TM, d), lambda i, j: (i, 0)),
            pl.BlockSpec((1, d), lambda i, j: (0, 0)),
            pl.BlockSpec((d, PROJ_TN), lambda i, j: (0, jnp.minimum(j, na_tiles - 1))),
            pl.BlockSpec((d, PROJ_TN), lambda i, j: (0, jnp.maximum(j - na_tiles, 0))),
            pl.BlockSpec((1, PROJ_TN), lambda i, j: (0, j)),
            pl.BlockSpec((1, PROJ_TN), lambda i, j: (0, j)),
        ],
        out_specs=pl.BlockSpec((PROJ_TM, PROJ_TN), lambda i, j: (i, j)),
        scratch_shapes=[pltpu.VMEM((PROJ_TM, d), BF16)],
        compiler_params=_params("parallel", "arbitrary"),
        name="in_proj",
    )(x2, gain, wa, wb, col_gain, col_flag)


def _compress_kernel(a_ref, pe_ref, w1_ref, w2_ref, kn_ref, o_ref):
    a = (a_ref[...].astype(F32) + pe_ref[...]).astype(BF16)
    hid = _dot(a, w1_ref[...])
    hid = hid * _sigmoid(hid)
    out = _dot(hid.astype(BF16), w2_ref[...])
    normed = _rms_rows(out, kn_ref[...])
    o_ref[...] = jnp.where(pl.program_id(0) == 0, normed, out).astype(o_ref.dtype)


def _compress(blocks, pe, w1, w2, kn):
    two, bg, ncp, flat = blocks.shape
    return pl.pallas_call(
        _compress_kernel,
        out_shape=jax.ShapeDtypeStruct((two, bg, ncp, HEAD_DIM), BF16),
        grid=(two, bg),
        in_specs=[
            pl.BlockSpec((None, None, ncp, flat), lambda s, i: (s, i, 0, 0)),
            pl.BlockSpec((None, 1, flat), lambda s, i: (s, 0, 0)),
            pl.BlockSpec((None, flat, CMP_HIDDEN), lambda s, i: (s, 0, 0)),
            pl.BlockSpec((None, CMP_HIDDEN, HEAD_DIM), lambda s, i: (s, 0, 0)),
            pl.BlockSpec((1, HEAD_DIM), lambda s, i: (0, 0)),
        ],
        out_specs=pl.BlockSpec((None, None, ncp, HEAD_DIM), lambda s, i: (s, i, 0, 0)),
        compiler_params=_params("arbitrary", "arbitrary"),
        name="nsa_compress",
    )(blocks, pe, w1, w2, kn)


def _split_position(v):
    hi = np.floor_divide(v, POS_RADIX)
    return hi, v - POS_RADIX * hi


def _rel_position_table(seq):
    hi, lo = _split_position(np.arange(2 * seq) - seq)
    tab = np.zeros((2 * seq, LANES), np.float32)
    tab[:, 0:ALIBI_COLS:2] = hi[:, None]
    tab[:, 1:ALIBI_COLS:2] = lo[:, None]
    return jnp.asarray(tab, BF16)


def _block_onehot_table(seq):
    tab = np.zeros((seq, LANES), np.float32)
    tab[np.arange(seq), SEL_COL0 + np.arange(seq) // SLC_BLOCK] = 1.0
    return jnp.asarray(tab, BF16)


def _alibi_rows(slopes):
    rest = slopes.astype(F32) * LOG2E
    cols = []
    for _ in range(ALIBI_PIECES):
        piece = rest.astype(BF16).astype(F32)
        rest = rest - piece
        cols += [piece * POS_RADIX, piece]
    rows = jnp.concatenate([jnp.stack(cols, axis=1),
                            jnp.zeros((slopes.shape[0], LANES - ALIBI_COLS), F32)], axis=1)
    return jnp.broadcast_to(rows[:, None, :], (slopes.shape[0], SUBLANES, LANES))


def _tile_distance(u):
    return LANES * u + np.arange(LANES)[:, None] - np.arange(ATT_TK)[None, :]


def _mask_table(n, valid_fn):
    tabs = [np.where(valid_fn(_tile_distance(u)), 0.0, -MASK_BIG) for u in range(n)]
    tabs += [np.zeros((LANES, ATT_TK)), np.full((LANES, ATT_TK), -MASK_BIG)]
    return jnp.asarray(np.stack(tabs), F32)


def _dilation_table():
    reach = max(w for w, _ in DIL_PATTERNS)
    n = (reach + ATT_TK - 1) // LANES + 1
    tabs = []
    for u in range(n):
        d = _tile_distance(u)
        mult = sum(((d >= 0) & (d <= w) & (d % dil == 0)).astype(np.float32) for w, dil in DIL_PATTERNS)
        tabs.append(np.where(mult > 0, np.log2(np.maximum(mult, 1.0)), -MASK_BIG))
    tabs.append(np.full((LANES, ATT_TK), -MASK_BIG))
    return jnp.asarray(np.stack(tabs), F32)


def _softmax_tile(s_buf, p_buf, m_scr, l_scr, acc_scr, bias_for_block):
    for rb in range(s_buf.shape[0] // LANES):
        rs = slice(rb * LANES, (rb + 1) * LANES)
        s = s_buf[rs, :] + bias_for_block(rb)
        m_old = m_scr[rs, :]
        m_new = jnp.maximum(m_old, jnp.max(s, axis=-1, keepdims=True))
        alpha = jnp.exp2(m_old - m_new)
        p = [jnp.exp2(s[:, c * LANES:(c + 1) * LANES] - m_new) for c in range(ATT_TK // LANES)]
        total = functools.reduce(lambda a, b: a + b, p)
        l_scr[rs, :] = alpha * l_scr[rs, :] + jnp.sum(total, axis=-1, keepdims=True)
        acc_scr[rs, :] = alpha * acc_scr[rs, :]
        m_scr[rs, :] = m_new
        for c in range(ATT_TK // LANES):
            p_buf[rs, c * LANES:(c + 1) * LANES] = p[c].astype(BF16)


def _softmax_init(m_scr, l_scr, acc_scr):
    m_scr[...] = jnp.full(m_scr.shape, NEG_INF, F32)
    l_scr[...] = jnp.zeros(l_scr.shape, F32)
    acc_scr[...] = jnp.zeros(acc_scr.shape, F32)


def _attend(first, end, qa_scr, s_bufs, p_scr, m_scr, l_scr, acc_scr, load_k_aug, load_v, bias_for_tile):
    s_a, s_b = s_bufs
    clamp = lambda kt: jnp.minimum(kt, end - 1)

    def scores(kt, s_buf):
        s_buf[...] = _dot_nt(qa_scr[...], load_k_aug(clamp(kt)))

    def absorb(kt, s_buf):
        _softmax_tile(s_buf, p_scr, m_scr, l_scr, acc_scr, bias_for_tile(kt))
        acc_scr[...] += _dot(p_scr[...], load_v(clamp(kt)))

    scores(first, s_a)

    def pair(j, carry):
        a = first + 2 * j
        scores(a + 1, s_b)
        absorb(a, s_a)
        scores(a + 2, s_a)
        absorb(a + 1, s_b)
        return carry

    lax.fori_loop(0, (end - first + 1) // 2, pair, 0)


def _nsa_kernel(q_ref, qx_ref, kc_ref, vc_ref, ks_ref, vs_ref, kw_ref, vw_ref, rel_ref, oh_ref,
                wtab_ref, ctab_ref, gt_ref, o_ref,
                qa_scr, s_a_scr, s_b_scr, p_scr, m_scr, l_scr, acc_scr, ob_scr, sc_scr, *, seq):
    s_bufs = (s_a_scr, s_b_scr)
    t0 = pl.program_id(2) * NSA_TQ
    n_slc = seq // SLC_BLOCK
    heads = [slice(r * NSA_TQ, (r + 1) * NSA_TQ) for r in range(NSA_REP)]
    alibi = [jnp.broadcast_to(qx_ref[r, 0:1, :], (NSA_TQ, LANES)) for r in range(NSA_REP)]
    for r in range(NSA_REP):
        qa_scr[heads[r], :HEAD_DIM] = q_ref[:, r * HEAD_DIM:(r + 1) * HEAD_DIM]
        qa_scr[heads[r], HEAD_DIM:] = alibi[r].astype(BF16)

    ncp = kc_ref.shape[0]
    n_i = lax.broadcasted_iota(I32, (ncp, LANES), 0)
    ln = lax.broadcasted_iota(I32, (ncp, LANES), 1)
    rel_end = n_i * CMP_STRIDE + (CMP_BLOCK - 1) - t0
    digits = jnp.where((ln & 1) == 0, rel_end >> POS_SHIFT, rel_end & (POS_RADIX - 1))
    feat = jnp.where(ln < ALIBI_COLS, digits, 0).astype(F32).astype(BF16)
    kc_aug = jnp.concatenate([kc_ref[...], feat], axis=1)
    row = lax.broadcasted_iota(I32, (NSA_TQ, 1), 0) + t0
    blk_end = lax.broadcasted_iota(I32, (NSA_TQ, ncp), 1) * CMP_STRIDE + (CMP_BLOCK - 1)
    valid_c = row >= blk_end
    vc = vc_ref[...]
    psum = jnp.zeros((NSA_TQ, ncp), F32)
    for r in range(NSA_REP):
        s = jnp.where(valid_c, _dot_nt(qa_scr[heads[r], :], kc_aug), NEG_INF)
        mc = jnp.max(s, axis=-1, keepdims=True)
        pc = jnp.where(valid_c, jnp.exp2(s - mc), 0.0)
        p = pc / jnp.maximum(jnp.sum(pc, axis=-1, keepdims=True), 1e-30)
        psum = psum + p
        ob_scr[0, heads[r], :] = _dot(p.astype(BF16), vc)

    per_slc = SLC_BLOCK // CMP_STRIDE
    back = CMP_BLOCK // CMP_STRIDE - 1
    jb = lax.broadcasted_iota(I32, (LANES, ncp), 0)
    nb = lax.broadcasted_iota(I32, (LANES, ncp), 1)
    overlap_t = jnp.where((nb >= per_slc * jb - back) & (nb < per_slc * (jb + 1)), 1.0, 0.0)
    imp_t = _dot_nt(overlap_t, psum, precision=lax.Precision.HIGHEST)
    j = lax.broadcasted_iota(I32, (LANES, NSA_TQ), 0)
    tq = lax.broadcasted_iota(I32, (LANES, NSA_TQ), 1) + t0
    cur = tq // SLC_BLOCK
    forced = (j == 0) | (j == cur) | (j == cur - 1)
    sc_scr[...] = jnp.where(forced, FORCED_SCORE, jnp.where(j * SLC_BLOCK <= tq, imp_t, -1.0))

    n_slab = n_slc // SUBLANES
    slabs = [sc_scr[SUBLANES * v:SUBLANES * (v + 1), :] for v in range(n_slab)]
    ranks = [jnp.zeros((SUBLANES, NSA_TQ), F32) for _ in range(n_slab)]
    sub = lax.broadcasted_iota(I32, (SUBLANES, NSA_TQ), 0)
    for i in range(n_slc):
        rival = jnp.broadcast_to(sc_scr[i:i + 1, :], (SUBLANES, NSA_TQ))
        for v in range(n_slab):
            wins_ties = jnp.where(rival >= slabs[v], 1.0, 0.0)
            loses_ties = jnp.where(rival > slabs[v], 1.0, 0.0)
            if SUBLANES * v > i:
                ranks[v] = ranks[v] + wins_ties
            elif SUBLANES * (v + 1) <= i:
                ranks[v] = ranks[v] + loses_ties
            else:
                ranks[v] = ranks[v] + jnp.where(sub + SUBLANES * v > i, wins_ties, loses_ties)
    n_top = min(SLC_TOP_N, n_slc)
    pen_rows = [jnp.zeros((SEL_COL0, NSA_TQ), F32)]
    pen_rows += [jnp.where(rk < n_top, 0.0, -MASK_BIG) for rk in ranks]
    if SEL_COL0 + n_slc < LANES:
        pen_rows.append(jnp.zeros((LANES - SEL_COL0 - n_slc, NSA_TQ), F32))
    penalty = jnp.concatenate(pen_rows, axis=0).T
    for r in range(NSA_REP):
        qa_scr[heads[r], HEAD_DIM:] = (alibi[r] + penalty).astype(BF16)

    def tile_rows(kt):
        return pl.ds(pl.multiple_of(kt * ATT_TK, ATT_TK), ATT_TK)

    def k_aug(kt, k_ref, with_blocks):
        extra = rel_ref[pl.ds(pl.multiple_of(kt * ATT_TK - t0 + seq, LANES), ATT_TK), :]
        if with_blocks:
            extra = extra + oh_ref[tile_rows(kt), :]
        return jnp.concatenate([k_ref[tile_rows(kt), :], extra], axis=1)

    def shared_bias(tab_ref, index):
        return lambda kt: (lambda rb, bias=tab_ref[index(kt)]: bias)

    last_kt = t0 // ATT_TK
    tile_shift = lambda kt: (t0 - kt * ATT_TK) // LANES

    def mask_index(tab_ref, kt, inside):
        return jnp.where(kt > last_kt, tab_ref.shape[0] - 1, inside)

    _softmax_init(m_scr, l_scr, acc_scr)
    _attend(0, last_kt + 1, qa_scr, s_bufs, p_scr, m_scr, l_scr, acc_scr,
            lambda kt: k_aug(kt, ks_ref, True), lambda kt: vs_ref[tile_rows(kt), :],
            shared_bias(ctab_ref, lambda kt: mask_index(
                ctab_ref, kt, jnp.where(kt == last_kt, tile_shift(kt), ctab_ref.shape[0] - 2))))
    ob_scr[1] = acc_scr[...] / l_scr[...]

    _softmax_init(m_scr, l_scr, acc_scr)
    first_kt = jnp.maximum(t0 - (WIN_SIZE - 1), 0) // ATT_TK
    _attend(first_kt, last_kt + 1, qa_scr, s_bufs, p_scr, m_scr, l_scr, acc_scr,
            lambda kt: k_aug(kt, kw_ref, False), lambda kt: vw_ref[tile_rows(kt), :],
            shared_bias(wtab_ref, lambda kt: mask_index(wtab_ref, kt, tile_shift(kt))))

    gate = _sigmoid(gt_ref[...].astype(F32))
    for r in range(NSA_REP):
        o_win = acc_scr[heads[r], :] / l_scr[heads[r], :]
        o = (gate[:, 3 * r:3 * r + 1] * ob_scr[0, heads[r], :] + gate[:, 3 * r + 1:3 * r + 2] * ob_scr[1, heads[r], :]
             + gate[:, 3 * r + 2:3 * r + 3] * o_win)
        o_ref[:, r * HEAD_DIM:(r + 1) * HEAD_DIM] = o.astype(o_ref.dtype)


def _nsa_mixer(slopes, proj, cmp_kv, gates_t, rel_tab, batch, seq):
    t = proj.shape[0]
    nq = seq // NSA_TQ
    ncp = cmp_kv.shape[2]
    n_slc = seq // SLC_BLOCK
    assert SEL_COL0 + n_slc <= LANES and n_slc % SUBLANES == 0 and seq % ATT_TK == 0
    rows = NSA_REP * NSA_TQ
    win_tab = _mask_table((WIN_SIZE - 1 + ATT_TK - 1) // LANES + 1, lambda d: (d >= 0) & (d <= WIN_SIZE - 1))
    causal_tab = _mask_table(ATT_TK // LANES, lambda d: d >= 0)

    def seq_spec(cb):
        return pl.BlockSpec((seq, HEAD_DIM), lambda b, g, i: (b, cb + g))

    def cmp_spec(which):
        return pl.BlockSpec((None, None, ncp, HEAD_DIM), lambda b, g, i: (which, b * N_NSA_KV + g, 0, 0))

    def whole(a):
        return pl.BlockSpec(a.shape, lambda b, g, i: (0,) * a.ndim)

    qo_spec = pl.BlockSpec((NSA_TQ, NSA_REP * HEAD_DIM), lambda b, g, i: (b * nq + i, g))
    onehot = _block_onehot_table(seq)
    qx = _alibi_rows(slopes)
    return pl.pallas_call(
        functools.partial(_nsa_kernel, seq=seq),
        out_shape=jax.ShapeDtypeStruct((t, Q_NSA_DIM), BF16),
        grid=(batch, N_NSA_KV, nq),
        in_specs=[
            qo_spec,
            pl.BlockSpec((NSA_REP, SUBLANES, LANES), lambda b, g, i: (g, 0, 0)),
            cmp_spec(0), cmp_spec(1),
            seq_spec(CB_K_SLC), seq_spec(CB_V_SLC), seq_spec(CB_K_WIN), seq_spec(CB_V_WIN),
            whole(rel_tab), whole(onehot), whole(win_tab), whole(causal_tab),
            pl.BlockSpec((None, NSA_TQ, 3 * NSA_REP), lambda b, g, i: (g, b * nq + i, 0)),
        ],
        out_specs=qo_spec,
        scratch_shapes=[
            pltpu.VMEM((rows, HEAD_DIM + LANES), BF16),
            pltpu.VMEM((rows, ATT_TK), F32),
            pltpu.VMEM((rows, ATT_TK), F32),
            pltpu.VMEM((rows, ATT_TK), BF16),
            pltpu.VMEM((rows, LANES), F32),
            pltpu.VMEM((rows, LANES), F32),
            pltpu.VMEM((rows, HEAD_DIM), F32),
            pltpu.VMEM((2, rows, HEAD_DIM), F32),
            pltpu.VMEM((LANES, NSA_TQ), F32),
        ],
        compiler_params=_params("arbitrary", "arbitrary", "arbitrary"),
        name="nsa_mixer",
    )(proj, qx, cmp_kv, cmp_kv, proj, proj, proj, proj, rel_tab, onehot, win_tab, causal_tab, gates_t)


def _dil_kernel(q_ref, qx_ref, k_ref, v_ref, rel_ref, mtab_ref, o_ref,
                qa_scr, s_a_scr, s_b_scr, p_scr, m_scr, l_scr, acc_scr, *, seq):
    t0 = pl.program_id(2) * DIL_TQ
    qa_scr[:, :HEAD_DIM] = q_ref[...]
    qa_scr[:, HEAD_DIM:] = jnp.broadcast_to(qx_ref[0:1, :], (DIL_TQ, LANES)).astype(BF16)
    _softmax_init(m_scr, l_scr, acc_scr)
    masked = mtab_ref.shape[0] - 1

    def tile_rows(kt):
        return pl.ds(pl.multiple_of(kt * ATT_TK, ATT_TK), ATT_TK)

    def k_aug(kt):
        rel = rel_ref[pl.ds(pl.multiple_of(kt * ATT_TK - t0 + seq, LANES), ATT_TK), :]
        return jnp.concatenate([k_ref[tile_rows(kt), :], rel], axis=1)

    def multiplicity(kt):
        def for_block(rb):
            u = (t0 + rb * LANES - kt * ATT_TK) // LANES
            return mtab_ref[jnp.where((u < 0) | (u >= masked), masked, u)]
        return for_block

    reach = max(w for w, _ in DIL_PATTERNS)
    first_kt = jnp.maximum(t0 - reach, 0) // ATT_TK
    _attend(first_kt, (t0 + DIL_TQ - 1) // ATT_TK + 1, qa_scr, (s_a_scr, s_b_scr), p_scr,
            m_scr, l_scr, acc_scr, k_aug, lambda kt: v_ref[tile_rows(kt), :], multiplicity)
    o_ref[...] = (acc_scr[...] / l_scr[...]).astype(o_ref.dtype)


def _dil_mixer(slopes, proj, rel_tab, batch, seq):
    t = proj.shape[0]
    nq = seq // DIL_TQ
    mult_tab = _dilation_table()
    qx = _alibi_rows(slopes)

    def whole(a):
        return pl.BlockSpec(a.shape, lambda b, h, i: (0,) * a.ndim)

    return pl.pallas_call(
        functools.partial(_dil_kernel, seq=seq),
        out_shape=jax.ShapeDtypeStruct((t, DIL_DIM), BF16),
        grid=(batch, N_DIL_HEADS, nq),
        in_specs=[
            pl.BlockSpec((DIL_TQ, HEAD_DIM), lambda b, h, i: (b * nq + i, CB_Q_DIL + h)),
            pl.BlockSpec((None, SUBLANES, LANES), lambda b, h, i: (h, 0, 0)),
            pl.BlockSpec((seq, HEAD_DIM), lambda b, h, i: (b, CB_K_DIL + h)),
            pl.BlockSpec((seq, HEAD_DIM), lambda b, h, i: (b, CB_V_DIL + h)),
            whole(rel_tab), whole(mult_tab),
        ],
        out_specs=pl.BlockSpec((DIL_TQ, HEAD_DIM), lambda b, h, i: (b * nq + i, h)),
        scratch_shapes=[
            pltpu.VMEM((DIL_TQ, HEAD_DIM + LANES), BF16),
            pltpu.VMEM((DIL_TQ, ATT_TK), F32),
            pltpu.VMEM((DIL_TQ, ATT_TK), F32),
            pltpu.VMEM((DIL_TQ, ATT_TK), BF16),
            pltpu.VMEM((DIL_TQ, LANES), F32),
            pltpu.VMEM((DIL_TQ, LANES), F32),
            pltpu.VMEM((DIL_TQ, HEAD_DIM), F32),
        ],
        compiler_params=_params("arbitrary", "arbitrary", "arbitrary"),
        name="dil_mixer",
    )(proj, qx, proj, proj, rel_tab, mult_tab)


def _out_proj_kernel(a_ref, b_ref, ga_ref, gb_ref, w_ref, res_ref, o_ref, h_scr):
    @pl.when(pl.program_id(1) == 0)
    def _():
        na = a_ref.shape[1]
        h_scr[:, :na] = _rms_rows(a_ref[...].astype(F32), ga_ref[...]).astype(BF16)
        h_scr[:, na:] = _rms_rows(b_ref[...].astype(F32), gb_ref[...]).astype(BF16)

    o_ref[...] = res_ref[...] + _dot(h_scr[...], w_ref[...])


def _out_proj(o_nsa, o_dil, g_nsa, g_dil, w, resid):
    t, d = resid.shape
    na, nb = o_nsa.shape[1], o_dil.shape[1]
    return pl.pallas_call(
        _out_proj_kernel,
        out_shape=jax.ShapeDtypeStruct((t, d), F32),
        grid=(t // PROJ_TM, d // PROJ_TN),
        in_specs=[
            pl.BlockSpec((PROJ_TM, na), lambda i, j: (i, 0)),
            pl.BlockSpec((PROJ_TM, nb), lambda i, j: (i, 0)),
            pl.BlockSpec((1, na), lambda i, j: (0, 0)),
            pl.BlockSpec((1, nb), lambda i, j: (0, 0)),
            pl.BlockSpec((na + nb, PROJ_TN), lambda i, j: (0, j)),
            pl.BlockSpec((PROJ_TM, PROJ_TN), lambda i, j: (i, j)),
        ],
        out_specs=pl.BlockSpec((PROJ_TM, PROJ_TN), lambda i, j: (i, j)),
        scratch_shapes=[pltpu.VMEM((PROJ_TM, na + nb), BF16)],
        compiler_params=_params("parallel", "arbitrary"),
        name="out_proj",
    )(o_nsa, o_dil, g_nsa, g_dil, w, resid)


def _router_kernel(x_ref, g_ref, wr_ref, br_ref, hn_ref, idx_ref, tw_ref):
    h = _rms_rows(x_ref[...], g_ref[...])
    hn_ref[...] = h
    logits = _dot(h, wr_ref[...], precision=lax.Precision.HIGHEST) + br_ref[...]
    lane = lax.broadcasted_iota(I32, logits.shape, 1)
    lanef = lane.astype(F32)
    work = jnp.where(lane < N_EXPERTS, logits, -jnp.inf)
    vals, ids = [], []
    for _ in range(TOP_K):
        mx = jnp.max(work, axis=-1, keepdims=True)
        first = jnp.min(jnp.where(work == mx, lanef, float(LANES)), axis=-1, keepdims=True)
        vals.append(mx)
        ids.append(first)
        work = jnp.where(lanef == first, -jnp.inf, work)
    es = [jnp.exp(v - vals[0]) for v in vals]
    den = functools.reduce(lambda a, b: a + b, es)
    idx_out = jnp.zeros(logits.shape, F32)
    tw_out = jnp.zeros(logits.shape, F32)
    for k in range(TOP_K):
        idx_out = jnp.where(lane == k, ids[k], idx_out)
        tw_out = jnp.where(lane == k, es[k] / den, tw_out)
    idx_ref[...] = idx_out.astype(I32)
    tw_ref[...] = tw_out


def _router(x1, gain, wr, br):
    t, d = x1.shape
    return pl.pallas_call(
        _router_kernel,
        out_shape=(jax.ShapeDtypeStruct((t, d), F32),
                   jax.ShapeDtypeStruct((t, LANES), I32),
                   jax.ShapeDtypeStruct((t, LANES), F32)),
        grid=(t // ROUTER_TM,),
        in_specs=[
            pl.BlockSpec((ROUTER_TM, d), lambda i: (i, 0)),
            pl.BlockSpec((1, d), lambda i: (0, 0)),
            pl.BlockSpec((d, LANES), lambda i: (0, 0)),
            pl.BlockSpec((1, LANES), lambda i: (0, 0)),
        ],
        out_specs=(pl.BlockSpec((ROUTER_TM, d), lambda i: (i, 0)),
                   pl.BlockSpec((ROUTER_TM, LANES), lambda i: (i, 0)),
                   pl.BlockSpec((ROUTER_TM, LANES), lambda i: (i, 0))),
        compiler_params=_params("parallel"),
        name="moe_router",
    )(x1, gain, wr, br)


def _rank_kernel(idx_ref, rank_ref, cnt_ref, carry_scr):
    @pl.when(pl.program_id(0) == 0)
    def _():
        carry_scr[...] = jnp.zeros(carry_scr.shape, F32)

    idx = idx_ref[...]
    lane = lax.broadcasted_iota(I32, idx.shape, 1)
    hits = [lane == idx[:, k:k + 1] for k in range(TOP_K)]
    onehot = functools.reduce(lambda a, b: a + b, [jnp.where(h, 1.0, 0.0) for h in hits])
    ri = lax.broadcasted_iota(I32, (RANK_TM, RANK_TM), 0)
    ci = lax.broadcasted_iota(I32, (RANK_TM, RANK_TM), 1)
    before = jnp.where(ci < ri, 1.0, 0.0).astype(BF16)
    rank = _dot(before, onehot.astype(BF16)) + carry_scr[0:1, :]
    out = jnp.zeros(idx.shape, F32)
    for k in range(TOP_K):
        mine = jnp.sum(jnp.where(hits[k], rank, 0.0), axis=-1, keepdims=True)
        out = jnp.where(lane == k, mine, out)
    rank_ref[...] = out.astype(I32)
    carry = carry_scr[...] + jnp.sum(onehot, axis=0, keepdims=True)
    carry_scr[...] = carry
    cnt_ref[...] = carry


def _rank(idx):
    t = idx.shape[0]
    return pl.pallas_call(
        _rank_kernel,
        out_shape=(jax.ShapeDtypeStruct((t, LANES), I32), jax.ShapeDtypeStruct((SUBLANES, LANES), F32)),
        grid=(t // RANK_TM,),
        in_specs=[pl.BlockSpec((RANK_TM, LANES), lambda i: (i, 0))],
        out_specs=(pl.BlockSpec((RANK_TM, LANES), lambda i: (i, 0)),
                   pl.BlockSpec((SUBLANES, LANES), lambda i: (0, 0))),
        scratch_shapes=[pltpu.VMEM((SUBLANES, LANES), F32)],
        compiler_params=_params("arbitrary"),
        name="moe_rank",
    )(idx)


def _pos_kernel(cnt_ref, idx_ref, rank_ref, pos_ref, vt_ref):
    cnt = cnt_ref[0:1, :]
    lane1 = lax.broadcasted_iota(I32, (1, LANES), 1)
    tiles = jnp.where(lane1 < N_EXPERTS, jnp.floor((cnt + (MOE_TM - 0.5)) / MOE_TM), 0.0)
    tiles_before = jnp.zeros((1, LANES), F32)
    rows_before = jnp.zeros((1, LANES), F32)
    for e in range(N_EXPERTS):
        tiles_before = tiles_before + jnp.where(lane1 > e, tiles[:, e:e + 1], 0.0)
        rows_before = rows_before + jnp.where(lane1 > e, cnt[:, e:e + 1], 0.0)
    tiles_through = tiles_before + tiles

    idx = idx_ref[...]
    rank = rank_ref[...].astype(F32)
    lane = lax.broadcasted_iota(I32, idx.shape, 1)
    out = jnp.zeros(idx.shape, F32)
    for k in range(TOP_K):
        mine = lane == idx[:, k:k + 1]
        row0 = jnp.sum(jnp.where(mine, tiles_before * MOE_TM, 0.0), axis=-1, keepdims=True)
        slot0 = jnp.sum(jnp.where(mine, rows_before, 0.0), axis=-1, keepdims=True)
        out = jnp.where(lane == k, row0 + rank[:, k:k + 1], out)
        out = jnp.where(lane == TOP_K + k, slot0 + rank[:, k:k + 1], out)
    pos_ref[...] = out.astype(I32)

    nv = vt_ref.shape[0]
    vl = lax.broadcasted_iota(I32, (nv, LANES), 1)
    v = lax.broadcasted_iota(I32, (nv, LANES), 0).astype(F32)
    total = jnp.sum(tiles, axis=-1, keepdims=True)
    vv = jnp.minimum(v, total - 1.0)
    done = jnp.where((tiles_through <= vv) & (vl < N_EXPERTS), 1.0, 0.0)
    e_v = jnp.minimum(jnp.sum(done, axis=-1, keepdims=True), float(N_EXPERTS - 1))
    mine = vl.astype(F32) == e_v
    first_tile = jnp.sum(jnp.where(mine, tiles_before, 0.0), axis=-1, keepdims=True)
    e_cnt = jnp.sum(jnp.where(mine, cnt, 0.0), axis=-1, keepdims=True)
    e_slot = jnp.sum(jnp.where(mine, rows_before, 0.0), axis=-1, keepdims=True)
    vv1 = vv[:, 0:1]
    valid = v[:, 0:1] < total
    offset = (vv1 - first_tile) * MOE_TM
    held = jnp.where(valid, jnp.clip(e_cnt - offset, 0.0, float(MOE_TM)), 0.0)
    nrows = jnp.ceil(held * (1.0 / MOE_SUB)) * MOE_SUB
    cols = [e_v, vv1, nrows, jnp.where(valid, 1.0, 0.0), e_slot + offset, held]
    table = jnp.zeros((nv, LANES), F32)
    for c, val in enumerate(cols):
        table = jnp.where(vl == c, val, table)
    vt_ref[...] = table.astype(I32)


def _positions(cnt, idx, rank, n_visits):
    t = idx.shape[0]
    nvp = -(-n_visits // SUBLANES) * SUBLANES
    return pl.pallas_call(
        _pos_kernel,
        out_shape=(jax.ShapeDtypeStruct((t, LANES), I32), jax.ShapeDtypeStruct((nvp, LANES), I32)),
        grid=(t // POS_TM,),
        in_specs=[pl.BlockSpec((SUBLANES, LANES), lambda i: (0, 0)),
                  pl.BlockSpec((POS_TM, LANES), lambda i: (i, 0)),
                  pl.BlockSpec((POS_TM, LANES), lambda i: (i, 0))],
        out_specs=(pl.BlockSpec((POS_TM, LANES), lambda i: (i, 0)),
                   pl.BlockSpec((nvp, LANES), lambda i: (0, 0))),
        compiler_params=_params("arbitrary"),
        name="moe_positions",
    )(cnt, idx, rank)


def _invert_kernel(slot_ref, tok_ref):
    def body(i, carry):
        tok_ref[slot_ref[i]] = i // TOP_K
        return carry

    lax.fori_loop(0, slot_ref.shape[0], body, 0, unroll=INVERT_UNROLL)


def _invert(slot_flat):
    return pl.pallas_call(
        _invert_kernel,
        out_shape=jax.ShapeDtypeStruct(slot_flat.shape, I32),
        in_specs=[pl.BlockSpec(memory_space=pltpu.SMEM)],
        out_specs=pl.BlockSpec(memory_space=pltpu.SMEM),
        name="moe_invert",
    )(slot_flat)


def _dispatch_kernel(vt_ref, tok_ref, hn_ref, xs_ref, buf, sems):
    i = pl.program_id(0)
    per_visit = MOE_TM // DISPATCH_TM

    def held_rows(step):
        v = step // per_visit
        part = (step % per_visit) * DISPATCH_TM
        return jnp.clip(vt_ref[v * VT_W + 5] - part, 0, DISPATCH_TM), vt_ref[v * VT_W + 4] + part

    def row_copy(src_row, r, slot):
        return pltpu.make_async_copy(hn_ref.at[pl.ds(src_row, 1)], buf.at[slot, pl.ds(r, 1)], sems.at[slot])

    def start(step, slot):
        n, base = held_rows(step)
        issue = lambda r, c: (row_copy(tok_ref[base + r], r, slot).start(), c)[1]

        @pl.when(n == DISPATCH_TM)
        def _():
            lax.fori_loop(0, DISPATCH_TM, issue, 0, unroll=GATHER_UNROLL)

        @pl.when(n < DISPATCH_TM)
        def _():
            lax.fori_loop(0, n, issue, 0)

    def wait(step, slot):
        n, _ = held_rows(step)

        @pl.when(n == DISPATCH_TM)
        def _():
            pltpu.make_async_copy(hn_ref.at[pl.ds(0, DISPATCH_TM)], buf.at[slot], sems.at[slot]).wait()

        @pl.when(n < DISPATCH_TM)
        def _():
            lax.fori_loop(0, n, lambda r, c: (row_copy(0, r, slot).wait(), c)[1], 0)

    @pl.when(i == 0)
    def _():
        buf[...] = jnp.zeros(buf.shape, buf.dtype)
        start(0, 0)

    @pl.when(i + 1 < pl.num_programs(0))
    def _():
        start(i + 1, (i + 1) % 2)

    wait(i, i % 2)
    n, _ = held_rows(i)
    rowid = lax.broadcasted_iota(I32, (DISPATCH_TM, 1), 0)
    xs_ref[...] = jnp.where(rowid < n, buf[i % 2], 0.0).astype(xs_ref.dtype)


def _dispatch(vt, tok, hn, n_visits):
    d = hn.shape[1]
    return pl.pallas_call(
        _dispatch_kernel,
        out_shape=jax.ShapeDtypeStruct((n_visits * MOE_TM, d), BF16),
        grid_spec=pltpu.PrefetchScalarGridSpec(
            num_scalar_prefetch=2,
            grid=(n_visits * MOE_TM // DISPATCH_TM,),
            in_specs=[pl.BlockSpec(memory_space=pl.ANY)],
            out_specs=pl.BlockSpec((DISPATCH_TM, d), lambda i, vt, tok: (i, 0)),
            scratch_shapes=[pltpu.VMEM((2, DISPATCH_TM, d), F32), pltpu.SemaphoreType.DMA((2,))],
        ),
        compiler_params=_params("arbitrary"),
        name="moe_dispatch",
    )(vt, tok, hn)


def _moe_up_kernel(vt_ref, xs_ref, wg_ref, wu_ref, bg_ref, bu_ref, h_ref, wg_scr, wu_scr):
    nrows = vt_ref[pl.program_id(0) * VT_W + 2]

    for sb in range(MOE_TM // MOE_SUB):
        rs = slice(sb * MOE_SUB, (sb + 1) * MOE_SUB)

        @pl.when(sb * MOE_SUB < nrows)
        def _(sb=sb, rs=rs):
            if sb == 0:
                wg_scr[...] = wg_ref[...].astype(BF16)
                wu_scr[...] = wu_ref[...].astype(BF16)
            x = xs_ref[rs, :]
            gate = jnp.minimum(_dot(x, wg_scr[...]) + bg_ref[...], SWIGLU_LIMIT)
            up = jnp.clip(_dot(x, wu_scr[...]) + bu_ref[...], -SWIGLU_LIMIT, SWIGLU_LIMIT)
            act = (up + 1.0) * (gate * _sigmoid(SWIGLU_ALPHA * gate))
            h_ref[rs, :] = act.astype(h_ref.dtype)

        @pl.when(sb * MOE_SUB >= nrows)
        def _():
            h_ref[rs, :] = jnp.zeros((MOE_SUB, h_ref.shape[1]), h_ref.dtype)


def _moe_up(vt, xs, w_gate_up, b_gate_up, n_visits):
    d = w_gate_up.shape[1]
    nf = D_EXPERT // MOE_TF

    def fsel(v, f, vt):
        return jnp.where(vt[v * VT_W + 3] > 0, f, nf - 1)

    return pl.pallas_call(
        _moe_up_kernel,
        out_shape=jax.ShapeDtypeStruct((n_visits * MOE_TM, D_EXPERT), BF16),
        grid_spec=pltpu.PrefetchScalarGridSpec(
            num_scalar_prefetch=1,
            grid=(n_visits, nf),
            in_specs=[
                pl.BlockSpec((MOE_TM, d), lambda v, f, vt: (vt[v * VT_W + 1], 0)),
                pl.BlockSpec((None, d, MOE_TF), lambda v, f, vt: (vt[v * VT_W], 0, fsel(v, f, vt))),
                pl.BlockSpec((None, d, MOE_TF), lambda v, f, vt: (vt[v * VT_W], 0, nf + fsel(v, f, vt))),
                pl.BlockSpec((None, 1, MOE_TF), lambda v, f, vt: (vt[v * VT_W], 0, fsel(v, f, vt))),
                pl.BlockSpec((None, 1, MOE_TF), lambda v, f, vt: (vt[v * VT_W], 0, nf + fsel(v, f, vt))),
            ],
            out_specs=pl.BlockSpec((MOE_TM, MOE_TF), lambda v, f, vt: (v, f)),
            scratch_shapes=[pltpu.VMEM((d, MOE_TF), BF16), pltpu.VMEM((d, MOE_TF), BF16)],
        ),
        compiler_params=_params("arbitrary", "arbitrary"),
        name="moe_up",
    )(vt, xs, w_gate_up, w_gate_up, b_gate_up, b_gate_up)


def _moe_down_kernel(vt_ref, h_ref, wd_ref, bd_ref, y_ref, wd_scr):
    nrows = vt_ref[pl.program_id(0) * VT_W + 2]

    for sb in range(MOE_TM // MOE_SUB):
        rs = slice(sb * MOE_SUB, (sb + 1) * MOE_SUB)

        @pl.when(sb * MOE_SUB < nrows)
        def _(sb=sb, rs=rs):
            if sb == 0:
                wd_scr[...] = wd_ref[...].astype(BF16)
            y_ref[rs, :] = _dot(h_ref[rs, :], wd_scr[...]) + bd_ref[...]

        @pl.when(sb * MOE_SUB >= nrows)
        def _():
            y_ref[rs, :] = jnp.zeros((MOE_SUB, y_ref.shape[1]), y_ref.dtype)


def _moe_down(vt, h, w_down, b_down, n_visits):
    d = w_down.shape[2]
    nj = d // MOE_TN

    def jsel(v, j, vt):
        return jnp.where(vt[v * VT_W + 3] > 0, j, nj - 1)

    return pl.pallas_call(
        _moe_down_kernel,
        out_shape=jax.ShapeDtypeStruct((n_visits * MOE_TM, d), F32),
        grid_spec=pltpu.PrefetchScalarGridSpec(
            num_scalar_prefetch=1,
            grid=(n_visits, nj),
            in_specs=[
                pl.BlockSpec((MOE_TM, D_EXPERT), lambda v, j, vt: (vt[v * VT_W + 1], 0)),
                pl.BlockSpec((None, D_EXPERT, MOE_TN), lambda v, j, vt: (vt[v * VT_W], 0, jsel(v, j, vt))),
                pl.BlockSpec((None, 1, MOE_TN), lambda v, j, vt: (vt[v * VT_W], 0, jsel(v, j, vt))),
            ],
            out_specs=pl.BlockSpec((MOE_TM, MOE_TN), lambda v, j, vt: (v, j)),
            scratch_shapes=[pltpu.VMEM((D_EXPERT, MOE_TN), BF16)],
        ),
        compiler_params=_params("arbitrary", "arbitrary"),
        name="moe_down",
    )(vt, h, w_down, b_down)


def _combine_kernel(pos_ref, ys_ref, x_ref, tw_ref, o_ref, buf, sems):
    i = pl.program_id(0)

    def start(step, slot):
        def issue(r, carry):
            for k in range(TOP_K):
                src = pos_ref[(step * COMBINE_TM + r) * TOP_K + k]
                pltpu.make_async_copy(ys_ref.at[pl.ds(src, 1)], buf.at[slot, k, pl.ds(r, 1)], sems.at[slot]).start()
            return carry
        lax.fori_loop(0, COMBINE_TM, issue, 0, unroll=GATHER_UNROLL)

    def wait(slot):
        for k in range(TOP_K):
            pltpu.make_async_copy(ys_ref.at[pl.ds(0, COMBINE_TM)], buf.at[slot, k], sems.at[slot]).wait()

    @pl.when(i == 0)
    def _():
        start(0, 0)

    @pl.when(i + 1 < pl.num_programs(0))
    def _():
        start(i + 1, (i + 1) % 2)

    wait(i % 2)
    tw = tw_ref[...]
    acc = x_ref[...]
    for k in range(TOP_K):
        acc = acc + tw[:, k:k + 1] * buf[i % 2, k]
    o_ref[...] = acc


def _combine(pos_flat, ys, x1, tw):
    t, d = x1.shape
    return pl.pallas_call(
        _combine_kernel,
        out_shape=jax.ShapeDtypeStruct((t, d), F32),
        grid_spec=pltpu.PrefetchScalarGridSpec(
            num_scalar_prefetch=1,
            grid=(t // COMBINE_TM,),
            in_specs=[pl.BlockSpec(memory_space=pl.ANY),
                      pl.BlockSpec((COMBINE_TM, d), lambda i, pos: (i, 0)),
                      pl.BlockSpec((COMBINE_TM, LANES), lambda i, pos: (i, 0))],
            out_specs=pl.BlockSpec((COMBINE_TM, d), lambda i, pos: (i, 0)),
            scratch_shapes=[pltpu.VMEM((2, TOP_K, COMBINE_TM, d), F32), pltpu.SemaphoreType.DMA((2,))],
        ),
        compiler_params=_params("arbitrary"),
        name="moe_combine",
    )(pos_flat, ys, x1, tw)


def _alibi_slopes():
    n = N_NSA_HEADS + N_DIL_HEADS
    i = jnp.arange(1, n + 1, dtype=F32)
    return jnp.exp2(-8.0 * i / n)


def _attention_block(x2, batch, seq, attn_norm, w_in, pe_k, w_k1, w_k2, pe_v, w_v1, w_v2,
                     qn_nsa, kn_cmp, kn_slc, kn_win, qn_dil, kn_dil, on_nsa, on_dil, w_out):
    d = x2.shape[1]
    a_dim = Q_NSA_DIM + 6 * KV_NSA_DIM
    d_off = a_dim + GATE_DIM
    n_pad = N_PROJ - (a_dim + 3 * DIL_DIM + GATE_DIM)
    assert a_dim % PROJ_TN == 0
    wa = w_in[:, :a_dim].astype(BF16)
    wb = jnp.concatenate([w_in[:, d_off:].astype(BF16), w_in[:, a_dim:d_off].astype(BF16),
                          jnp.zeros((d, n_pad), BF16)], axis=1)
    ones_kv = jnp.ones((KV_NSA_DIM,), F32)
    ones_dil = jnp.ones((DIL_DIM,), F32)
    tail = jnp.ones((GATE_DIM + n_pad,), F32)
    q_scale = ATTN_SCALE * LOG2E
    col_gain = jnp.concatenate([
        jnp.tile(qn_nsa, N_NSA_HEADS) * q_scale, ones_kv, ones_kv, jnp.tile(kn_slc, N_NSA_KV), ones_kv,
        jnp.tile(kn_win, N_NSA_KV), ones_kv, jnp.tile(qn_dil, N_DIL_HEADS) * q_scale,
        jnp.tile(kn_dil, N_DIL_HEADS), ones_dil, tail])[None, :]
    col_flag = jnp.concatenate([
        jnp.ones((Q_NSA_DIM,), F32), 0 * ones_kv, 0 * ones_kv, ones_kv, 0 * ones_kv, ones_kv, 0 * ones_kv,
        ones_dil, ones_dil, 0 * ones_dil, 0 * tail])[None, :]
    proj = _in_proj(x2, attn_norm[None, :], wa, wb, col_gain, col_flag)

    n_chunks = seq // CMP_STRIDE
    kv = proj[:, CB_K_CMP * LANES:(CB_V_CMP + N_NSA_KV) * LANES]
    kv = kv.reshape(batch, n_chunks, CMP_STRIDE, 2, N_NSA_KV, HEAD_DIM).transpose(3, 0, 4, 1, 2, 5)
    chunks = kv.reshape(2, batch * N_NSA_KV, n_chunks, CMP_STRIDE * HEAD_DIM)
    assert CMP_BLOCK == 2 * CMP_STRIDE
    blocks = jnp.concatenate([chunks[:, :, :-1], chunks[:, :, 1:]], axis=-1)
    blocks = jnp.pad(blocks, ((0, 0), (0, 0), (0, 1), (0, 0)))
    pe = jnp.stack([pe_k.reshape(1, -1), pe_v.reshape(1, -1)])
    w1 = jnp.stack([w_k1, w_v1]).astype(BF16)
    w2 = jnp.stack([w_k2, w_v2]).astype(BF16)
    cmp_kv = _compress(blocks, pe, w1, w2, kn_cmp[None, :])

    slopes = _alibi_slopes()
    rel_tab = _rel_position_table(seq)
    gates = proj[:, CB_GATE * LANES:CB_GATE * LANES + GATE_DIM]
    gates_t = gates.reshape(-1, N_NSA_KV, 3 * NSA_REP).transpose(1, 0, 2)
    o_nsa = _nsa_mixer(slopes[0::2], proj, cmp_kv, gates_t, rel_tab, batch, seq)
    o_dil = _dil_mixer(slopes[1::2], proj, rel_tab, batch, seq)
    return _out_proj(o_nsa, o_dil, on_nsa[None, :], on_dil[None, :], w_out.astype(BF16), x2)


def _moe_block(x1, ffn_norm, w_router, b_router, w_gate_up, b_gate_up, w_down, b_down):
    t, d = x1.shape
    wr = jnp.pad(w_router, ((0, 0), (0, LANES - N_EXPERTS)))
    br = jnp.pad(b_router, (0, LANES - N_EXPERTS))[None, :]
    hn, idx, tw = _router(x1, ffn_norm[None, :], wr, br)
    rank, cnt = _rank(idx)
    n_visits = N_EXPERTS + (t * TOP_K) // MOE_TM
    pos, table = _positions(cnt, idx, rank, n_visits)
    pos_flat = pos[:, :TOP_K].reshape(-1)
    tok = _invert(pos[:, TOP_K:2 * TOP_K].reshape(-1))
    vt = table[:n_visits, :VT_W].reshape(-1)
    xs = _dispatch(vt, tok, hn, n_visits)
    h = _moe_up(vt, xs, w_gate_up, b_gate_up[:, None, :], n_visits)
    ys = _moe_down(vt, h, w_down, b_down[:, None, :], n_visits)
    return _combine(pos_flat, ys, x1, tw)


def kernel(x, attn_norm, w_in, cmp_pos_k, w_cmp_k1, w_cmp_k2, cmp_pos_v, w_cmp_v1, w_cmp_v2, q_norm_nsa, k_norm_cmp, k_norm_slc, k_norm_win, q_norm_dil, k_norm_dil, out_norm_nsa, out_norm_dil, w_out, ffn_norm, w_router, b_router, w_gate_up, b_gate_up, w_down, b_down):
    batch, seq, d = x.shape
    x2 = x.reshape(batch * seq, d)
    for layer in range(attn_norm.shape[0]):
        x2 = _attention_block(
            x2, batch, seq, attn_norm[layer], w_in[layer], cmp_pos_k[layer], w_cmp_k1[layer],
            w_cmp_k2[layer], cmp_pos_v[layer], w_cmp_v1[layer], w_cmp_v2[layer], q_norm_nsa[layer],
            k_norm_cmp[layer], k_norm_slc[layer], k_norm_win[layer], q_norm_dil[layer],
            k_norm_dil[layer], out_norm_nsa[layer], out_norm_dil[layer], w_out[layer])
        x2 = _moe_block(x2, ffn_norm[layer], w_router[layer], b_router[layer], w_gate_up[layer],
                        b_gate_up[layer], w_down[layer], b_down[layer])
    return x2.reshape(batch, seq, d)
```

```python
import functools

import numpy as np
import jax
import jax.numpy as jnp
from jax import lax
from jax.experimental import pallas as pl
from jax.experimental.pallas import tpu as pltpu

F32, BF16, I32 = jnp.float32, jnp.bfloat16, jnp.int32

HEAD_DIM = 128
N_NSA_HEADS = 16
N_NSA_KV = 4
NSA_REP = N_NSA_HEADS // N_NSA_KV
N_DIL_HEADS = 16
CMP_BLOCK = 32
CMP_STRIDE = 16
CMP_HIDDEN = 256
SLC_BLOCK = 64
SLC_TOP_N = 16
FORCED_SCORE = 1.0e4
WIN_SIZE = 512
DIL_PATTERNS = ((128, 1), (512, 4), (2048, 16))
N_EXPERTS = 32
TOP_K = 4
D_EXPERT = 1536
SWIGLU_ALPHA = 1.702
SWIGLU_LIMIT = 7.0
RMS_EPS = 1e-6
NEG_INF = -1e30
ATTN_SCALE = HEAD_DIM ** -0.5
LOG2E = 1.4426950408889634

Q_NSA_DIM = N_NSA_HEADS * HEAD_DIM
KV_NSA_DIM = N_NSA_KV * HEAD_DIM
GATE_DIM = N_NSA_HEADS * 3
DIL_DIM = N_DIL_HEADS * HEAD_DIM

LANES = 128
SUBLANES = 8
VMEM_LIMIT = 56 * 1024 * 1024

CB_Q_NSA = 0
CB_K_CMP = CB_Q_NSA + N_NSA_HEADS
CB_V_CMP = CB_K_CMP + N_NSA_KV
CB_K_SLC = CB_V_CMP + N_NSA_KV
CB_V_SLC = CB_K_SLC + N_NSA_KV
CB_K_WIN = CB_V_SLC + N_NSA_KV
CB_V_WIN = CB_K_WIN + N_NSA_KV
CB_Q_DIL = CB_V_WIN + N_NSA_KV
CB_K_DIL = CB_Q_DIL + N_DIL_HEADS
CB_V_DIL = CB_K_DIL + N_DIL_HEADS
CB_GATE = CB_V_DIL + N_DIL_HEADS
PROJ_TN = 512
N_PROJ = -(-(CB_GATE + 1) * LANES // PROJ_TN) * PROJ_TN

PROJ_TM = 512
NSA_TQ = 256
DIL_TQ = 512
ATT_TK = 256
ROUTER_TM = 256
RANK_TM = 512
POS_TM = 1024
MOE_TM = 1024
MOE_SUB = 256
MOE_TF = 256
MOE_TN = 1024
INVERT_UNROLL = 16
GATHER_UNROLL = 8
DISPATCH_TM = 256
COMBINE_TM = 128
VT_W = 8

POS_SHIFT = 8
POS_RADIX = 1 << POS_SHIFT
ALIBI_PIECES = 3
ALIBI_COLS = 2 * ALIBI_PIECES
SEL_COL0 = 64
MASK_BIG = 2.0 ** 100


def _dot(a, b, **kw):
    return jnp.dot(a, b, preferred_element_type=F32, **kw)


def _dot_nt(a, b, **kw):
    return lax.dot_general(a, b, (((1,), (1,)), ((), ())), preferred_element_type=F32, **kw)


def _rms_rows(x, gain):
    ms = jnp.mean(x * x, axis=-1, keepdims=True)
    return x * lax.rsqrt(ms + RMS_EPS) * gain


def _sigmoid(x):
    return 1.0 / (1.0 + jnp.exp(-x))


def _params(*sem):
    return pltpu.CompilerParams(dimension_semantics=sem, vmem_limit_bytes=VMEM_LIMIT)


def _in_proj_kernel(x_ref, g_ref, wa_ref, wb_ref, cg_ref, cf_ref, o_ref, h_scr, *, na_tiles):
    j = pl.program_id(1)

    @pl.when(j == 0)
    def _():
        h_scr[...] = _rms_rows(x_ref[...], g_ref[...]).astype(BF16)

    def tile(w_ref):
        acc = _dot(h_scr[...], w_ref[...])
        for c in range(PROJ_TN // LANES):
            cs = slice(c * LANES, (c + 1) * LANES)
            a = acc[:, cs]
            r = lax.rsqrt(jnp.mean(a * a, axis=-1, keepdims=True) + RMS_EPS)
            y = a * jnp.where(cf_ref[:, cs] > 0, r, 1.0) * cg_ref[:, cs]
            o_ref[:, cs] = y.astype(o_ref.dtype)

    pl.when(j < na_tiles)(functools.partial(tile, wa_ref))
    pl.when(j >= na_tiles)(functools.partial(tile, wb_ref))


def _in_proj(x2, gain, wa, wb, col_gain, col_flag):
    t, d = x2.shape
    na_tiles = wa.shape[1] // PROJ_TN
    n = wa.shape[1] + wb.shape[1]
    return pl.pallas_call(
        functools.partial(_in_proj_kernel, na_tiles=na_tiles),
        out_shape=jax.ShapeDtypeStruct((t, n), BF16),
        grid=(t // PROJ_TM, n // PROJ_TN),
        in_specs=[
            pl.BlockSpec((PROJ_TM, d), lambda i, j: (i, 0)),
            pl.BlockSpec((1, d), lambda i, j: (0, 0)),
            pl.BlockSpec((d, PROJ_TN), lambda i, j: (0, jnp.minimum(j, na_tiles - 1))),
            pl.BlockSpec((d, PROJ_TN), lambda i, j: (0, jnp.maximum(j - na_tiles, 0))),
            pl.BlockSpec((1, PROJ_TN), lambda i, j: (0, j)),
            pl.BlockSpec((1, PROJ_TN), lambda i, j: (0, j)),
        ],
        out_specs=pl.BlockSpec((PROJ_TM, PROJ_TN), lambda i, j: (i, j)),
        scratch_shapes=[pltpu.VMEM((PROJ_TM, d), BF16)],
        compiler_params=_params("parallel", "arbitrary"),
        name="in_proj",
    )(x2, gain, wa, wb, col_gain, col_flag)


def _compress_kernel(a_ref, pe_ref, w1_ref, w2_ref, kn_ref, o_ref):
    a = (a_ref[...].astype(F32) + pe_ref[...]).astype(BF16)
    hid = _dot(a, w1_ref[...])
    hid = hid * _sigmoid(hid)
    out = _dot(hid.astype(BF16), w2_ref[...])
    normed = _rms_rows(out, kn_ref[...])
    o_ref[...] = jnp.where(pl.program_id(0) == 0, normed, out).astype(o_ref.dtype)


def _compress(blocks, pe, w1, w2, kn):
    two, bg, ncp, flat = blocks.shape
    return pl.pallas_call(
        _compress_kernel,
        out_shape=jax.ShapeDtypeStruct((two, bg, ncp, HEAD_DIM), BF16),
        grid=(two, bg),
        in_specs=[
            pl.BlockSpec((None, None, ncp, flat), lambda s, i: (s, i, 0, 0)),
            pl.BlockSpec((None, 1, flat), lambda s, i: (s, 0, 0)),
            pl.BlockSpec((None, flat, CMP_HIDDEN), lambda s, i: (s, 0, 0)),
            pl.BlockSpec((None, CMP_HIDDEN, HEAD_DIM), lambda s, i: (s, 0, 0)),
            pl.BlockSpec((1, HEAD_DIM), lambda s, i: (0, 0)),
        ],
        out_specs=pl.BlockSpec((None, None, ncp, HEAD_DIM), lambda s, i: (s, i, 0, 0)),
        compiler_params=_params("arbitrary", "arbitrary"),
        name="nsa_compress",
    )(blocks, pe, w1, w2, kn)


def _split_position(v):
    hi = np.floor_divide(v, POS_RADIX)
    return hi, v - POS_RADIX * hi


def _rel_position_table(seq):
    hi, lo = _split_position(np.arange(2 * seq) - seq)
    tab = np.zeros((2 * seq, LANES), np.float32)
    tab[:, 0:ALIBI_COLS:2] = hi[:, None]
    tab[:, 1:ALIBI_COLS:2] = lo[:, None]
    return jnp.asarray(tab, BF16)


def _block_onehot_table(seq):
    tab = np.zeros((seq, LANES), np.float32)
    tab[np.arange(seq), SEL_COL0 + np.arange(seq) // SLC_BLOCK] = 1.0
    return jnp.asarray(tab, BF16)


def _alibi_rows(slopes):
    rest = slopes.astype(F32) * LOG2E
    cols = []
    for _ in range(ALIBI_PIECES):
        piece = rest.astype(BF16).astype(F32)
        rest = rest - piece
        cols += [piece * POS_RADIX, piece]
    rows = jnp.concatenate([jnp.stack(cols, axis=1),
                            jnp.zeros((slopes.shape[0], LANES - ALIBI_COLS), F32)], axis=1)
    return jnp.broadcast_to(rows[:, None, :], (slopes.shape[0], SUBLANES, LANES))


def _tile_distance(u):
    return LANES * u + np.arange(LANES)[:, None] - np.arange(ATT_TK)[None, :]


def _mask_table(n, valid_fn):
    tabs = [np.where(valid_fn(_tile_distance(u)), 0.0, -MASK_BIG) for u in range(n)]
    tabs += [np.zeros((LANES, ATT_TK)), np.full((LANES, ATT_TK), -MASK_BIG)]
    return jnp.asarray(np.stack(tabs), F32)


def _dilation_table():
    reach = max(w for w, _ in DIL_PATTERNS)
    n = (reach + ATT_TK - 1) // LANES + 1
    tabs = []
    for u in range(n):
        d = _tile_distance(u)
        mult = sum(((d >= 0) & (d <= w) & (d % dil == 0)).astype(np.float32) for w, dil in DIL_PATTERNS)
        tabs.append(np.where(mult > 0, np.log2(np.maximum(mult, 1.0)), -MASK_BIG))
    tabs.append(np.full((LANES, ATT_TK), -MASK_BIG))
    return jnp.asarray(np.stack(tabs), F32)


def _softmax_tile(s_buf, p_buf, m_scr, l_scr, acc_scr, bias_for_block):
    for rb in range(s_buf.shape[0] // LANES):
        rs = slice(rb * LANES, (rb + 1) * LANES)
        s = s_buf[rs, :] + bias_for_block(rb)
        m_old = m_scr[rs, :]
        m_new = jnp.maximum(m_old, jnp.max(s, axis=-1, keepdims=True))
        alpha = jnp.exp2(m_old - m_new)
        p = [jnp.exp2(s[:, c * LANES:(c + 1) * LANES] - m_new) for c in range(ATT_TK // LANES)]
        total = functools.reduce(lambda a, b: a + b, p)
        l_scr[rs, :] = alpha * l_scr[rs, :] + jnp.sum(total, axis=-1, keepdims=True)
        acc_scr[rs, :] = alpha * acc_scr[rs, :]
        m_scr[rs, :] = m_new
        for c in range(ATT_TK // LANES):
            p_buf[rs, c * LANES:(c + 1) * LANES] = p[c].astype(BF16)


def _softmax_init(m_scr, l_scr, acc_scr):
    m_scr[...] = jnp.full(m_scr.shape, NEG_INF, F32)
    l_scr[...] = jnp.zeros(l_scr.shape, F32)
    acc_scr[...] = jnp.zeros(acc_scr.shape, F32)


def _attend(first, end, qa_scr, s_bufs, p_scr, m_scr, l_scr, acc_scr, load_k_aug, load_v, bias_for_tile):
    s_a, s_b = s_bufs
    last = end - 1

    def scores(kt, s_buf):
        s_buf[...] = _dot_nt(qa_scr[...], load_k_aug(jnp.minimum(kt, last)))

    def absorb(kt, s_buf):
        _softmax_tile(s_buf, p_scr, m_scr, l_scr, acc_scr, bias_for_tile(kt))
        acc_scr[...] += _dot(p_scr[...], load_v(kt))

    scores(first, s_a)

    def pair(j, carry):
        a = first + 2 * j
        scores(a + 1, s_b)
        absorb(a, s_a)
        scores(a + 2, s_a)
        absorb(a + 1, s_b)
        return carry

    lax.fori_loop(0, (end - first) // 2, pair, 0)

    @pl.when((end - first) % 2 == 1)
    def _():
        absorb(last, s_a)


def _nsa_kernel(q_ref, qx_ref, kc_ref, vc_ref, ks_ref, vs_ref, kw_ref, vw_ref, rel_ref, oh_ref,
                wtab_ref, ctab_ref, gt_ref, o_ref,
                qa_scr, s_a_scr, s_b_scr, p_scr, m_scr, l_scr, acc_scr, ob_scr, sc_scr, *, seq):
    s_bufs = (s_a_scr, s_b_scr)
    t0 = pl.program_id(2) * NSA_TQ
    n_slc = seq // SLC_BLOCK
    heads = [slice(r * NSA_TQ, (r + 1) * NSA_TQ) for r in range(NSA_REP)]
    alibi = [jnp.broadcast_to(qx_ref[r, 0:1, :], (NSA_TQ, LANES)) for r in range(NSA_REP)]
    for r in range(NSA_REP):
        qa_scr[heads[r], :HEAD_DIM] = q_ref[:, r * HEAD_DIM:(r + 1) * HEAD_DIM]
        qa_scr[heads[r], HEAD_DIM:] = alibi[r].astype(BF16)

    ncp = kc_ref.shape[0]
    n_i = lax.broadcasted_iota(I32, (ncp, LANES), 0)
    ln = lax.broadcasted_iota(I32, (ncp, LANES), 1)
    rel_end = n_i * CMP_STRIDE + (CMP_BLOCK - 1) - t0
    digits = jnp.where((ln & 1) == 0, rel_end >> POS_SHIFT, rel_end & (POS_RADIX - 1))
    feat = jnp.where(ln < ALIBI_COLS, digits, 0).astype(F32).astype(BF16)
    kc_aug = jnp.concatenate([kc_ref[...], feat], axis=1)
    row = lax.broadcasted_iota(I32, (NSA_TQ, 1), 0) + t0
    blk_end = lax.broadcasted_iota(I32, (NSA_TQ, ncp), 1) * CMP_STRIDE + (CMP_BLOCK - 1)
    valid_c = row >= blk_end
    vc = vc_ref[...]
    psum = jnp.zeros((NSA_TQ, ncp), F32)
    for r in range(NSA_REP):
        s = jnp.where(valid_c, _dot_nt(qa_scr[heads[r], :], kc_aug), NEG_INF)
        mc = jnp.max(s, axis=-1, keepdims=True)
        pc = jnp.where(valid_c, jnp.exp2(s - mc), 0.0)
        p = pc / jnp.maximum(jnp.sum(pc, axis=-1, keepdims=True), 1e-30)
        psum = psum + p
        ob_scr[0, heads[r], :] = _dot(p.astype(BF16), vc)

    per_slc = SLC_BLOCK // CMP_STRIDE
    back = CMP_BLOCK // CMP_STRIDE - 1
    jb = lax.broadcasted_iota(I32, (LANES, ncp), 0)
    nb = lax.broadcasted_iota(I32, (LANES, ncp), 1)
    overlap_t = jnp.where((nb >= per_slc * jb - back) & (nb < per_slc * (jb + 1)), 1.0, 0.0)
    imp_t = _dot_nt(overlap_t, psum, precision=lax.Precision.HIGHEST)
    j = lax.broadcasted_iota(I32, (LANES, NSA_TQ), 0)
    tq = lax.broadcasted_iota(I32, (LANES, NSA_TQ), 1) + t0
    cur = tq // SLC_BLOCK
    forced = (j == 0) | (j == cur) | (j == cur - 1)
    sc_scr[...] = jnp.where(forced, FORCED_SCORE, jnp.where(j * SLC_BLOCK <= tq, imp_t, -1.0))

    n_slab = n_slc // SUBLANES
    slabs = [sc_scr[SUBLANES * v:SUBLANES * (v + 1), :] for v in range(n_slab)]
    ranks = [jnp.zeros((SUBLANES, NSA_TQ), F32) for _ in range(n_slab)]
    sub = lax.broadcasted_iota(I32, (SUBLANES, NSA_TQ), 0)
    for i in range(n_slc):
        rival = jnp.broadcast_to(sc_scr[i:i + 1, :], (SUBLANES, NSA_TQ))
        for v in range(n_slab):
            wins_ties = jnp.where(rival >= slabs[v], 1.0, 0.0)
            loses_ties = jnp.where(rival > slabs[v], 1.0, 0.0)
            if SUBLANES * v > i:
                ranks[v] = ranks[v] + wins_ties
            elif SUBLANES * (v + 1) <= i:
                ranks[v] = ranks[v] + loses_ties
            else:
                ranks[v] = ranks[v] + jnp.where(sub + SUBLANES * v > i, wins_ties, loses_ties)
    n_top = min(SLC_TOP_N, n_slc)
    pen_rows = [jnp.zeros((SEL_COL0, NSA_TQ), F32)]
    pen_rows += [jnp.where(rk < n_top, 0.0, -MASK_BIG) for rk in ranks]
    if SEL_COL0 + n_slc < LANES:
        pen_rows.append(jnp.zeros((LANES - SEL_COL0 - n_slc, NSA_TQ), F32))
    penalty = jnp.concatenate(pen_rows, axis=0).T
    for r in range(NSA_REP):
        qa_scr[heads[r], HEAD_DIM:] = (alibi[r] + penalty).astype(BF16)

    def tile_rows(kt):
        return pl.ds(pl.multiple_of(kt * ATT_TK, ATT_TK), ATT_TK)

    def k_aug(kt, k_ref, with_blocks):
        extra = rel_ref[pl.ds(pl.multiple_of(kt * ATT_TK - t0 + seq, LANES), ATT_TK), :]
        if with_blocks:
            extra = extra + oh_ref[tile_rows(kt), :]
        return jnp.concatenate([k_ref[tile_rows(kt), :], extra], axis=1)

    q_blocks = NSA_TQ // LANES
    last_kt = (t0 + NSA_TQ - 1) // ATT_TK

    def tile_shift(kt, rb):
        return (t0 + (rb % q_blocks) * LANES - kt * ATT_TK) // LANES

    def masks(tab_ref, index):
        return lambda kt: (lambda rb: tab_ref[index(kt, rb)])

    unmasked = ctab_ref.shape[0] - 2
    _softmax_init(m_scr, l_scr, acc_scr)
    _attend(0, last_kt + 1, qa_scr, s_bufs, p_scr, m_scr, l_scr, acc_scr,
            lambda kt: k_aug(kt, ks_ref, True), lambda kt: vs_ref[tile_rows(kt), :],
            masks(ctab_ref, lambda kt, qb: jnp.where(kt == last_kt, tile_shift(kt, qb), unmasked)))
    ob_scr[1] = acc_scr[...] / l_scr[...]

    _softmax_init(m_scr, l_scr, acc_scr)
    first_kt = jnp.maximum(t0 - (WIN_SIZE - 1), 0) // ATT_TK
    _attend(first_kt, last_kt + 1, qa_scr, s_bufs, p_scr, m_scr, l_scr, acc_scr,
            lambda kt: k_aug(kt, kw_ref, False), lambda kt: vw_ref[tile_rows(kt), :],
            masks(wtab_ref, tile_shift))

    gate = _sigmoid(gt_ref[...].astype(F32))
    for r in range(NSA_REP):
        o_win = acc_scr[heads[r], :] / l_scr[heads[r], :]
        o = (gate[:, 3 * r:3 * r + 1] * ob_scr[0, heads[r], :] + gate[:, 3 * r + 1:3 * r + 2] * ob_scr[1, heads[r], :]
             + gate[:, 3 * r + 2:3 * r + 3] * o_win)
        o_ref[:, r * HEAD_DIM:(r + 1) * HEAD_DIM] = o.astype(o_ref.dtype)


def _nsa_mixer(slopes, proj, cmp_kv, gates_t, rel_tab, batch, seq):
    t = proj.shape[0]
    nq = seq // NSA_TQ
    ncp = cmp_kv.shape[2]
    n_slc = seq // SLC_BLOCK
    assert SEL_COL0 + n_slc <= LANES and n_slc % SUBLANES == 0 and seq % ATT_TK == 0
    rows = NSA_REP * NSA_TQ
    win_shifts = (WIN_SIZE - 1 + ATT_TK - 1 + NSA_TQ - LANES) // LANES + 1
    win_tab = _mask_table(win_shifts, lambda d: (d >= 0) & (d <= WIN_SIZE - 1))
    causal_tab = _mask_table(ATT_TK // LANES, lambda d: d >= 0)

    def seq_spec(cb):
        return pl.BlockSpec((seq, HEAD_DIM), lambda b, g, i: (b, cb + g))

    def cmp_spec(which):
        return pl.BlockSpec((None, None, ncp, HEAD_DIM), lambda b, g, i: (which, b * N_NSA_KV + g, 0, 0))

    def whole(a):
        return pl.BlockSpec(a.shape, lambda b, g, i: (0,) * a.ndim)

    qo_spec = pl.BlockSpec((NSA_TQ, NSA_REP * HEAD_DIM), lambda b, g, i: (b * nq + i, g))
    onehot = _block_onehot_table(seq)
    qx = _alibi_rows(slopes)
    return pl.pallas_call(
        functools.partial(_nsa_kernel, seq=seq),
        out_shape=jax.ShapeDtypeStruct((t, Q_NSA_DIM), BF16),
        grid=(batch, N_NSA_KV, nq),
        in_specs=[
            qo_spec,
            pl.BlockSpec((NSA_REP, SUBLANES, LANES), lambda b, g, i: (g, 0, 0)),
            cmp_spec(0), cmp_spec(1),
            seq_spec(CB_K_SLC), seq_spec(CB_V_SLC), seq_spec(CB_K_WIN), seq_spec(CB_V_WIN),
            whole(rel_tab), whole(onehot), whole(win_tab), whole(causal_tab),
            pl.BlockSpec((None, NSA_TQ, 3 * NSA_REP), lambda b, g, i: (g, b * nq + i, 0)),
        ],
        out_specs=qo_spec,
        scratch_shapes=[
            pltpu.VMEM((rows, HEAD_DIM + LANES), BF16),
            pltpu.VMEM((rows, ATT_TK), F32),
            pltpu.VMEM((rows, ATT_TK), F32),
            pltpu.VMEM((rows, ATT_TK), BF16),
            pltpu.VMEM((rows, LANES), F32),
            pltpu.VMEM((rows, LANES), F32),
            pltpu.VMEM((rows, HEAD_DIM), F32),
            pltpu.VMEM((2, rows, HEAD_DIM), F32),
            pltpu.VMEM((LANES, NSA_TQ), F32),
        ],
        compiler_params=_params("arbitrary", "arbitrary", "arbitrary"),
        name="nsa_mixer",
    )(proj, qx, cmp_kv, cmp_kv, proj, proj, proj, proj, rel_tab, onehot, win_tab, causal_tab, gates_t)


def _dil_kernel(q_ref, qx_ref, k_ref, v_ref, rel_ref, mtab_ref, o_ref,
                qa_scr, s_a_scr, s_b_scr, p_scr, m_scr, l_scr, acc_scr, *, seq):
    t0 = pl.program_id(2) * DIL_TQ
    qa_scr[:, :HEAD_DIM] = q_ref[...]
    qa_scr[:, HEAD_DIM:] = jnp.broadcast_to(qx_ref[0:1, :], (DIL_TQ, LANES)).astype(BF16)
    _softmax_init(m_scr, l_scr, acc_scr)
    masked = mtab_ref.shape[0] - 1

    def tile_rows(kt):
        return pl.ds(pl.multiple_of(kt * ATT_TK, ATT_TK), ATT_TK)

    def k_aug(kt):
        rel = rel_ref[pl.ds(pl.multiple_of(kt * ATT_TK - t0 + seq, LANES), ATT_TK), :]
        return jnp.concatenate([k_ref[tile_rows(kt), :], rel], axis=1)

    def multiplicity(kt):
        def for_block(rb):
            u = (t0 + rb * LANES - kt * ATT_TK) // LANES
            return mtab_ref[jnp.where((u < 0) | (u >= masked), masked, u)]
        return for_block

    reach = max(w for w, _ in DIL_PATTERNS)
    first_kt = jnp.maximum(t0 - reach, 0) // ATT_TK
    _attend(first_kt, (t0 + DIL_TQ - 1) // ATT_TK + 1, qa_scr, (s_a_scr, s_b_scr), p_scr,
            m_scr, l_scr, acc_scr, k_aug, lambda kt: v_ref[tile_rows(kt), :], multiplicity)
    o_ref[...] = (acc_scr[...] / l_scr[...]).astype(o_ref.dtype)


def _dil_mixer(slopes, proj, rel_tab, batch, seq):
    t = proj.shape[0]
    nq = seq // DIL_TQ
    mult_tab = _dilation_table()
    qx = _alibi_rows(slopes)

    def whole(a):
        return pl.BlockSpec(a.shape, lambda b, h, i: (0,) * a.ndim)

    return pl.pallas_call(
        functools.partial(_dil_kernel, seq=seq),
        out_shape=jax.ShapeDtypeStruct((t, DIL_DIM), BF16),
        grid=(batch, N_DIL_HEADS, nq),
        in_specs=[
            pl.BlockSpec((DIL_TQ, HEAD_DIM), lambda b, h, i: (b * nq + i, CB_Q_DIL + h)),
            pl.BlockSpec((None, SUBLANES, LANES), lambda b, h, i: (h, 0, 0)),
            pl.BlockSpec((seq, HEAD_DIM), lambda b, h, i: (b, CB_K_DIL + h)),
            pl.BlockSpec((seq, HEAD_DIM), lambda b, h, i: (b, CB_V_DIL + h)),
            whole(rel_tab), whole(mult_tab),
        ],
        out_specs=pl.BlockSpec((DIL_TQ, HEAD_DIM), lambda b, h, i: (b * nq + i, h)),
        scratch_shapes=[
            pltpu.VMEM((DIL_TQ, HEAD_DIM + LANES), BF16),
            pltpu.VMEM((DIL_TQ, ATT_TK), F32),
            pltpu.VMEM((DIL_TQ, ATT_TK), F32),
            pltpu.VMEM((DIL_TQ, ATT_TK), BF16),
            pltpu.VMEM((DIL_TQ, LANES), F32),
            pltpu.VMEM((DIL_TQ, LANES), F32),
            pltpu.VMEM((DIL_TQ, HEAD_DIM), F32),
        ],
        compiler_params=_params("arbitrary", "arbitrary", "arbitrary"),
        name="dil_mixer",
    )(proj, qx, proj, proj, rel_tab, mult_tab)


def _out_proj_kernel(a_ref, b_ref, ga_ref, gb_ref, w_ref, res_ref, o_ref, h_scr):
    @pl.when(pl.program_id(1) == 0)
    def _():
        na = a_ref.shape[1]
        h_scr[:, :na] = _rms_rows(a_ref[...].astype(F32), ga_ref[...]).astype(BF16)
        h_scr[:, na:] = _rms_rows(b_ref[...].astype(F32), gb_ref[...]).astype(BF16)

    o_ref[...] = res_ref[...] + _dot(h_scr[...], w_ref[...])


def _out_proj(o_nsa, o_dil, g_nsa, g_dil, w, resid):
    t, d = resid.shape
    na, nb = o_nsa.shape[1], o_dil.shape[1]
    return pl.pallas_call(
        _out_proj_kernel,
        out_shape=jax.ShapeDtypeStruct((t, d), F32),
        grid=(t // PROJ_TM, d // PROJ_TN),
        in_specs=[
            pl.BlockSpec((PROJ_TM, na), lambda i, j: (i, 0)),
            pl.BlockSpec((PROJ_TM, nb), lambda i, j: (i, 0)),
            pl.BlockSpec((1, na), lambda i, j: (0, 0)),
            pl.BlockSpec((1, nb), lambda i, j: (0, 0)),
            pl.BlockSpec((na + nb, PROJ_TN), lambda i, j: (0, j)),
            pl.BlockSpec((PROJ_TM, PROJ_TN), lambda i, j: (i, j)),
        ],
        out_specs=pl.BlockSpec((PROJ_TM, PROJ_TN), lambda i, j: (i, j)),
        scratch_shapes=[pltpu.VMEM((PROJ_TM, na + nb), BF16)],
        compiler_params=_params("parallel", "arbitrary"),
        name="out_proj",
    )(o_nsa, o_dil, g_nsa, g_dil, w, resid)


def _router_kernel(x_ref, g_ref, wr_ref, br_ref, hn_ref, idx_ref, tw_ref):
    h = _rms_rows(x_ref[...], g_ref[...])
    hn_ref[...] = h
    logits = _dot(h, wr_ref[...], precision=lax.Precision.HIGHEST) + br_ref[...]
    lane = lax.broadcasted_iota(I32, logits.shape, 1)
    lanef = lane.astype(F32)
    work = jnp.where(lane < N_EXPERTS, logits, -jnp.inf)
    vals, ids = [], []
    for _ in range(TOP_K):
        mx = jnp.max(work, axis=-1, keepdims=True)
        first = jnp.min(jnp.where(work == mx, lanef, float(LANES)), axis=-1, keepdims=True)
        vals.append(mx)
        ids.append(first)
        work = jnp.where(lanef == first, -jnp.inf, work)
    es = [jnp.exp(v - vals[0]) for v in vals]
    den = functools.reduce(lambda a, b: a + b, es)
    idx_out = jnp.zeros(logits.shape, F32)
    tw_out = jnp.zeros(logits.shape, F32)
    for k in range(TOP_K):
        idx_out = jnp.where(lane == k, ids[k], idx_out)
        tw_out = jnp.where(lane == k, es[k] / den, tw_out)
    idx_ref[...] = idx_out.astype(I32)
    tw_ref[...] = tw_out


def _router(x1, gain, wr, br):
    t, d = x1.shape
    return pl.pallas_call(
        _router_kernel,
        out_shape=(jax.ShapeDtypeStruct((t, d), F32),
                   jax.ShapeDtypeStruct((t, LANES), I32),
                   jax.ShapeDtypeStruct((t, LANES), F32)),
        grid=(t // ROUTER_TM,),
        in_specs=[
            pl.BlockSpec((ROUTER_TM, d), lambda i: (i, 0)),
            pl.BlockSpec((1, d), lambda i: (0, 0)),
            pl.BlockSpec((d, LANES), lambda i: (0, 0)),
            pl.BlockSpec((1, LANES), lambda i: (0, 0)),
        ],
        out_specs=(pl.BlockSpec((ROUTER_TM, d), lambda i: (i, 0)),
                   pl.BlockSpec((ROUTER_TM, LANES), lambda i: (i, 0)),
                   pl.BlockSpec((ROUTER_TM, LANES), lambda i: (i, 0))),
        compiler_params=_params("parallel"),
        name="moe_router",
    )(x1, gain, wr, br)


def _rank_kernel(idx_ref, rank_ref, cnt_ref, carry_scr):
    @pl.when(pl.program_id(0) == 0)
    def _():
        carry_scr[...] = jnp.zeros(carry_scr.shape, F32)

    idx = idx_ref[...]
    lane = lax.broadcasted_iota(I32, idx.shape, 1)
    hits = [lane == idx[:, k:k + 1] for k in range(TOP_K)]
    onehot = functools.reduce(lambda a, b: a + b, [jnp.where(h, 1.0, 0.0) for h in hits])
    ri = lax.broadcasted_iota(I32, (RANK_TM, RANK_TM), 0)
    ci = lax.broadcasted_iota(I32, (RANK_TM, RANK_TM), 1)
    before = jnp.where(ci < ri, 1.0, 0.0).astype(BF16)
    rank = _dot(before, onehot.astype(BF16)) + carry_scr[0:1, :]
    out = jnp.zeros(idx.shape, F32)
    for k in range(TOP_K):
        mine = jnp.sum(jnp.where(hits[k], rank, 0.0), axis=-1, keepdims=True)
        out = jnp.where(lane == k, mine, out)
    rank_ref[...] = out.astype(I32)
    carry = carry_scr[...] + jnp.sum(onehot, axis=0, keepdims=True)
    carry_scr[...] = carry
    cnt_ref[...] = carry


def _rank(idx):
    t = idx.shape[0]
    return pl.pallas_call(
        _rank_kernel,
        out_shape=(jax.ShapeDtypeStruct((t, LANES), I32), jax.ShapeDtypeStruct((SUBLANES, LANES), F32)),
        grid=(t // RANK_TM,),
        in_specs=[pl.BlockSpec((RANK_TM, LANES), lambda i: (i, 0))],
        out_specs=(pl.BlockSpec((RANK_TM, LANES), lambda i: (i, 0)),
                   pl.BlockSpec((SUBLANES, LANES), lambda i: (0, 0))),
        scratch_shapes=[pltpu.VMEM((SUBLANES, LANES), F32)],
        compiler_params=_params("arbitrary"),
        name="moe_rank",
    )(idx)


def _pos_kernel(cnt_ref, idx_ref, rank_ref, pos_ref, vt_ref):
    cnt = cnt_ref[0:1, :]
    lane1 = lax.broadcasted_iota(I32, (1, LANES), 1)
    tiles = jnp.where(lane1 < N_EXPERTS, jnp.floor((cnt + (MOE_TM - 0.5)) / MOE_TM), 0.0)
    tiles_before = jnp.zeros((1, LANES), F32)
    rows_before = jnp.zeros((1, LANES), F32)
    for e in range(N_EXPERTS):
        tiles_before = tiles_before + jnp.where(lane1 > e, tiles[:, e:e + 1], 0.0)
        rows_before = rows_before + jnp.where(lane1 > e, cnt[:, e:e + 1], 0.0)
    tiles_through = tiles_before + tiles

    idx = idx_ref[...]
    rank = rank_ref[...].astype(F32)
    lane = lax.broadcasted_iota(I32, idx.shape, 1)
    out = jnp.zeros(idx.shape, F32)
    for k in range(TOP_K):
        mine = lane == idx[:, k:k + 1]
        row0 = jnp.sum(jnp.where(mine, tiles_before * MOE_TM, 0.0), axis=-1, keepdims=True)
        slot0 = jnp.sum(jnp.where(mine, rows_before, 0.0), axis=-1, keepdims=True)
        out = jnp.where(lane == k, row0 + rank[:, k:k + 1], out)
        out = jnp.where(lane == TOP_K + k, slot0 + rank[:, k:k + 1], out)
    pos_ref[...] = out.astype(I32)

    nv = vt_ref.shape[0]
    vl = lax.broadcasted_iota(I32, (nv, LANES), 1)
    v = lax.broadcasted_iota(I32, (nv, LANES), 0).astype(F32)
    total = jnp.sum(tiles, axis=-1, keepdims=True)
    vv = jnp.minimum(v, total - 1.0)
    done = jnp.where((tiles_through <= vv) & (vl < N_EXPERTS), 1.0, 0.0)
    e_v = jnp.minimum(jnp.sum(done, axis=-1, keepdims=True), float(N_EXPERTS - 1))
    mine = vl.astype(F32) == e_v
    first_tile = jnp.sum(jnp.where(mine, tiles_before, 0.0), axis=-1, keepdims=True)
    e_cnt = jnp.sum(jnp.where(mine, cnt, 0.0), axis=-1, keepdims=True)
    e_slot = jnp.sum(jnp.where(mine, rows_before, 0.0), axis=-1, keepdims=True)
    vv1 = vv[:, 0:1]
    valid = v[:, 0:1] < total
    offset = (vv1 - first_tile) * MOE_TM
    held = jnp.where(valid, jnp.clip(e_cnt - offset, 0.0, float(MOE_TM)), 0.0)
    nrows = jnp.ceil(held * (1.0 / MOE_SUB)) * MOE_SUB
    cols = [e_v, vv1, nrows, jnp.where(valid, 1.0, 0.0), e_slot + offset, held]
    table = jnp.zeros((nv, LANES), F32)
    for c, val in enumerate(cols):
        table = jnp.where(vl == c, val, table)
    vt_ref[...] = table.astype(I32)


def _positions(cnt, idx, rank, n_visits):
    t = idx.shape[0]
    nvp = -(-n_visits // SUBLANES) * SUBLANES
    return pl.pallas_call(
        _pos_kernel,
        out_shape=(jax.ShapeDtypeStruct((t, LANES), I32), jax.ShapeDtypeStruct((nvp, LANES), I32)),
        grid=(t // POS_TM,),
        in_specs=[pl.BlockSpec((SUBLANES, LANES), lambda i: (0, 0)),
                  pl.BlockSpec((POS_TM, LANES), lambda i: (i, 0)),
                  pl.BlockSpec((POS_TM, LANES), lambda i: (i, 0))],
        out_specs=(pl.BlockSpec((POS_TM, LANES), lambda i: (i, 0)),
                   pl.BlockSpec((nvp, LANES), lambda i: (0, 0))),
        compiler_params=_params("arbitrary"),
        name="moe_positions",
    )(cnt, idx, rank)


def _invert_kernel(slot_ref, tok_ref):
    def body(i, carry):
        tok_ref[slot_ref[i]] = i // TOP_K
        return carry

    lax.fori_loop(0, slot_ref.shape[0], body, 0, unroll=INVERT_UNROLL)


def _invert(slot_flat):
    return pl.pallas_call(
        _invert_kernel,
        out_shape=jax.ShapeDtypeStruct(slot_flat.shape, I32),
        in_specs=[pl.BlockSpec(memory_space=pltpu.SMEM)],
        out_specs=pl.BlockSpec(memory_space=pltpu.SMEM),
        name="moe_invert",
    )(slot_flat)


def _dispatch_kernel(vt_ref, tok_ref, hn_ref, xs_ref, buf, sems):
    i = pl.program_id(0)
    per_visit = MOE_TM // DISPATCH_TM

    def held_rows(step):
        v = step // per_visit
        part = (step % per_visit) * DISPATCH_TM
        return jnp.clip(vt_ref[v * VT_W + 5] - part, 0, DISPATCH_TM), vt_ref[v * VT_W + 4] + part

    def row_copy(src_row, r, slot):
        return pltpu.make_async_copy(hn_ref.at[pl.ds(src_row, 1)], buf.at[slot, pl.ds(r, 1)], sems.at[slot])

    def start(step, slot):
        n, base = held_rows(step)
        issue = lambda r, c: (row_copy(tok_ref[base + r], r, slot).start(), c)[1]

        @pl.when(n == DISPATCH_TM)
        def _():
            lax.fori_loop(0, DISPATCH_TM, issue, 0, unroll=GATHER_UNROLL)

        @pl.when(n < DISPATCH_TM)
        def _():
            lax.fori_loop(0, n, issue, 0)

    def wait(step, slot):
        n, _ = held_rows(step)

        @pl.when(n == DISPATCH_TM)
        def _():
            pltpu.make_async_copy(hn_ref.at[pl.ds(0, DISPATCH_TM)], buf.at[slot], sems.at[slot]).wait()

        @pl.when(n < DISPATCH_TM)
        def _():
            lax.fori_loop(0, n, lambda r, c: (row_copy(0, r, slot).wait(), c)[1], 0)

    @pl.when(i == 0)
    def _():
        buf[...] = jnp.zeros(buf.shape, buf.dtype)
        start(0, 0)

    @pl.when(i + 1 < pl.num_programs(0))
    def _():
        start(i + 1, (i + 1) % 2)

    wait(i, i % 2)
    n, _ = held_rows(i)
    rowid = lax.broadcasted_iota(I32, (DISPATCH_TM, 1), 0)
    xs_ref[...] = jnp.where(rowid < n, buf[i % 2], 0.0).astype(xs_ref.dtype)


def _dispatch(vt, tok, hn, n_visits):
    d = hn.shape[1]
    return pl.pallas_call(
        _dispatch_kernel,
        out_shape=jax.ShapeDtypeStruct((n_visits * MOE_TM, d), BF16),
        grid_spec=pltpu.PrefetchScalarGridSpec(
            num_scalar_prefetch=2,
            grid=(n_visits * MOE_TM // DISPATCH_TM,),
            in_specs=[pl.BlockSpec(memory_space=pl.ANY)],
            out_specs=pl.BlockSpec((DISPATCH_TM, d), lambda i, vt, tok: (i, 0)),
            scratch_shapes=[pltpu.VMEM((2, DISPATCH_TM, d), F32), pltpu.SemaphoreType.DMA((2,))],
        ),
        compiler_params=_params("arbitrary"),
        name="moe_dispatch",
    )(vt, tok, hn)


def _moe_up_kernel(vt_ref, xs_ref, wg_ref, wu_ref, bg_ref, bu_ref, h_ref, wg_scr, wu_scr):
    nrows = vt_ref[pl.program_id(0) * VT_W + 2]

    for sb in range(MOE_TM // MOE_SUB):
        rs = slice(sb * MOE_SUB, (sb + 1) * MOE_SUB)

        @pl.when(sb * MOE_SUB < nrows)
        def _(sb=sb, rs=rs):
            if sb == 0:
                wg_scr[...] = wg_ref[...].astype(BF16)
                wu_scr[...] = wu_ref[...].astype(BF16)
            x = xs_ref[rs, :]
            gate = jnp.minimum(_dot(x, wg_scr[...]) + bg_ref[...], SWIGLU_LIMIT)
            up = jnp.clip(_dot(x, wu_scr[...]) + bu_ref[...], -SWIGLU_LIMIT, SWIGLU_LIMIT)
            act = (up + 1.0) * (gate * _sigmoid(SWIGLU_ALPHA * gate))
            h_ref[rs, :] = act.astype(h_ref.dtype)

        @pl.when(sb * MOE_SUB >= nrows)
        def _():
            h_ref[rs, :] = jnp.zeros((MOE_SUB, h_ref.shape[1]), h_ref.dtype)


def _moe_up(vt, xs, w_gate_up, b_gate_up, n_visits):
    d = w_gate_up.shape[1]
    nf = D_EXPERT // MOE_TF

    def fsel(v, f, vt):
        return jnp.where(vt[v * VT_W + 3] > 0, f, nf - 1)

    return pl.pallas_call(
        _moe_up_kernel,
        out_shape=jax.ShapeDtypeStruct((n_visits * MOE_TM, D_EXPERT), BF16),
        grid_spec=pltpu.PrefetchScalarGridSpec(
            num_scalar_prefetch=1,
            grid=(n_visits, nf),
            in_specs=[
                pl.BlockSpec((MOE_TM, d), lambda v, f, vt: (vt[v * VT_W + 1], 0)),
                pl.BlockSpec((None, d, MOE_TF), lambda v, f, vt: (vt[v * VT_W], 0, fsel(v, f, vt))),
                pl.BlockSpec((None, d, MOE_TF), lambda v, f, vt: (vt[v * VT_W], 0, nf + fsel(v, f, vt))),
                pl.BlockSpec((None, 1, MOE_TF), lambda v, f, vt: (vt[v * VT_W], 0, fsel(v, f, vt))),
                pl.BlockSpec((None, 1, MOE_TF), lambda v, f, vt: (vt[v * VT_W], 0, nf + fsel(v, f, vt))),
            ],
            out_specs=pl.BlockSpec((MOE_TM, MOE_TF), lambda v, f, vt: (v, f)),
            scratch_shapes=[pltpu.VMEM((d, MOE_TF), BF16), pltpu.VMEM((d, MOE_TF), BF16)],
        ),
        compiler_params=_params("arbitrary", "arbitrary"),
        name="moe_up",
    )(vt, xs, w_gate_up, w_gate_up, b_gate_up, b_gate_up)


def _moe_down_kernel(vt_ref, h_ref, wd_ref, bd_ref, y_ref, wd_scr):
    nrows = vt_ref[pl.program_id(0) * VT_W + 2]

    for sb in range(MOE_TM // MOE_SUB):
        rs = slice(sb * MOE_SUB, (sb + 1) * MOE_SUB)

        @pl.when(sb * MOE_SUB < nrows)
        def _(sb=sb, rs=rs):
            if sb == 0:
                wd_scr[...] = wd_ref[...].astype(BF16)
            y_ref[rs, :] = _dot(h_ref[rs, :], wd_scr[...]) + bd_ref[...]

        @pl.when(sb * MOE_SUB >= nrows)
        def _():
            y_ref[rs, :] = jnp.zeros((MOE_SUB, y_ref.shape[1]), y_ref.dtype)


def _moe_down(vt, h, w_down, b_down, n_visits):
    d = w_down.shape[2]
    nj = d // MOE_TN

    def jsel(v, j, vt):
        return jnp.where(vt[v * VT_W + 3] > 0, j, nj - 1)

    return pl.pallas_call(
        _moe_down_kernel,
        out_shape=jax.ShapeDtypeStruct((n_visits * MOE_TM, d), F32),
        grid_spec=pltpu.PrefetchScalarGridSpec(
            num_scalar_prefetch=1,
            grid=(n_visits, nj),
            in_specs=[
                pl.BlockSpec((MOE_TM, D_EXPERT), lambda v, j, vt: (vt[v * VT_W + 1], 0)),
                pl.BlockSpec((None, D_EXPERT, MOE_TN), lambda v, j, vt: (vt[v * VT_W], 0, jsel(v, j, vt))),
                pl.BlockSpec((None, 1, MOE_TN), lambda v, j, vt: (vt[v * VT_W], 0, jsel(v, j, vt))),
            ],
            out_specs=pl.BlockSpec((MOE_TM, MOE_TN), lambda v, j, vt: (v, j)),
            scratch_shapes=[pltpu.VMEM((D_EXPERT, MOE_TN), BF16)],
        ),
        compiler_params=_params("arbitrary", "arbitrary"),
        name="moe_down",
    )(vt, h, w_down, b_down)


def _combine_kernel(pos_ref, ys_ref, x_ref, tw_ref, o_ref, buf, sems):
    i = pl.program_id(0)

    def start(step, slot):
        def issue(r, carry):
            for k in range(TOP_K):
                src = pos_ref[(step * COMBINE_TM + r) * TOP_K + k]
                pltpu.make_async_copy(ys_ref.at[pl.ds(src, 1)], buf.at[slot, k, pl.ds(r, 1)], sems.at[slot]).start()
            return carry
        lax.fori_loop(0, COMBINE_TM, issue, 0, unroll=GATHER_UNROLL)

    def wait(slot):
        for k in range(TOP_K):
            pltpu.make_async_copy(ys_ref.at[pl.ds(0, COMBINE_TM)], buf.at[slot, k], sems.at[slot]).wait()

    @pl.when(i == 0)
    def _():
        start(0, 0)

    @pl.when(i + 1 < pl.num_programs(0))
    def _():
        start(i + 1, (i + 1) % 2)

    wait(i % 2)
    tw = tw_ref[...]
    acc = x_ref[...]
    for k in range(TOP_K):
        acc = acc + tw[:, k:k + 1] * buf[i % 2, k]
    o_ref[...] = acc


def _combine(pos_flat, ys, x1, tw):
    t, d = x1.shape
    return pl.pallas_call(
        _combine_kernel,
        out_shape=jax.ShapeDtypeStruct((t, d), F32),
        grid_spec=pltpu.PrefetchScalarGridSpec(
            num_scalar_prefetch=1,
            grid=(t // COMBINE_TM,),
            in_specs=[pl.BlockSpec(memory_space=pl.ANY),
                      pl.BlockSpec((COMBINE_TM, d), lambda i, pos: (i, 0)),
                      pl.BlockSpec((COMBINE_TM, LANES), lambda i, pos: (i, 0))],
            out_specs=pl.BlockSpec((COMBINE_TM, d), lambda i, pos: (i, 0)),
            scratch_shapes=[pltpu.VMEM((2, TOP_K, COMBINE_TM, d), F32), pltpu.SemaphoreType.DMA((2,))],
        ),
        compiler_params=_params("arbitrary"),
        name="moe_combine",
    )(pos_flat, ys, x1, tw)


def _alibi_slopes():
    n = N_NSA_HEADS + N_DIL_HEADS
    i = jnp.arange(1, n + 1, dtype=F32)
    return jnp.exp2(-8.0 * i / n)


def _attention_block(x2, batch, seq, attn_norm, w_in, pe_k, w_k1, w_k2, pe_v, w_v1, w_v2,
                     qn_nsa, kn_cmp, kn_slc, kn_win, qn_dil, kn_dil, on_nsa, on_dil, w_out):
    d = x2.shape[1]
    a_dim = Q_NSA_DIM + 6 * KV_NSA_DIM
    d_off = a_dim + GATE_DIM
    n_pad = N_PROJ - (a_dim + 3 * DIL_DIM + GATE_DIM)
    assert a_dim % PROJ_TN == 0
    wa = w_in[:, :a_dim].astype(BF16)
    wb = jnp.concatenate([w_in[:, d_off:].astype(BF16), w_in[:, a_dim:d_off].astype(BF16),
                          jnp.zeros((d, n_pad), BF16)], axis=1)
    ones_kv = jnp.ones((KV_NSA_DIM,), F32)
    ones_dil = jnp.ones((DIL_DIM,), F32)
    tail = jnp.ones((GATE_DIM + n_pad,), F32)
    q_scale = ATTN_SCALE * LOG2E
    col_gain = jnp.concatenate([
        jnp.tile(qn_nsa, N_NSA_HEADS) * q_scale, ones_kv, ones_kv, jnp.tile(kn_slc, N_NSA_KV), ones_kv,
        jnp.tile(kn_win, N_NSA_KV), ones_kv, jnp.tile(qn_dil, N_DIL_HEADS) * q_scale,
        jnp.tile(kn_dil, N_DIL_HEADS), ones_dil, tail])[None, :]
    col_flag = jnp.concatenate([
        jnp.ones((Q_NSA_DIM,), F32), 0 * ones_kv, 0 * ones_kv, ones_kv, 0 * ones_kv, ones_kv, 0 * ones_kv,
        ones_dil, ones_dil, 0 * ones_dil, 0 * tail])[None, :]
    proj = _in_proj(x2, attn_norm[None, :], wa, wb, col_gain, col_flag)

    n_chunks = seq // CMP_STRIDE
    kv = proj[:, CB_K_CMP * LANES:(CB_V_CMP + N_NSA_KV) * LANES]
    kv = kv.reshape(batch, n_chunks, CMP_STRIDE, 2, N_NSA_KV, HEAD_DIM).transpose(3, 0, 4, 1, 2, 5)
    chunks = kv.reshape(2, batch * N_NSA_KV, n_chunks, CMP_STRIDE * HEAD_DIM)
    assert CMP_BLOCK == 2 * CMP_STRIDE
    blocks = jnp.concatenate([chunks[:, :, :-1], chunks[:, :, 1:]], axis=-1)
    blocks = jnp.pad(blocks, ((0, 0), (0, 0), (0, 1), (0, 0)))
    pe = jnp.stack([pe_k.reshape(1, -1), pe_v.reshape(1, -1)])
    w1 = jnp.stack([w_k1, w_v1]).astype(BF16)
    w2 = jnp.stack([w_k2, w_v2]).astype(BF16)
    cmp_kv = _compress(blocks, pe, w1, w2, kn_cmp[None, :])

    slopes = _alibi_slopes()
    rel_tab = _rel_position_table(seq)
    gates = proj[:, CB_GATE * LANES:CB_GATE * LANES + GATE_DIM]
    gates_t = gates.reshape(-1, N_NSA_KV, 3 * NSA_REP).transpose(1, 0, 2)
    o_nsa = _nsa_mixer(slopes[0::2], proj, cmp_kv, gates_t, rel_tab, batch, seq)
    o_dil = _dil_mixer(slopes[1::2], proj, rel_tab, batch, seq)
    return _out_proj(o_nsa, o_dil, on_nsa[None, :], on_dil[None, :], w_out.astype(BF16), x2)


def _moe_block(x1, ffn_norm, w_router, b_router, w_gate_up, b_gate_up, w_down, b_down):
    t, d = x1.shape
    wr = jnp.pad(w_router, ((0, 0), (0, LANES - N_EXPERTS)))
    br = jnp.pad(b_router, (0, LANES - N_EXPERTS))[None, :]
    hn, idx, tw = _router(x1, ffn_norm[None, :], wr, br)
    rank, cnt = _rank(idx)
    n_visits = N_EXPERTS + (t * TOP_K) // MOE_TM
    pos, table = _positions(cnt, idx, rank, n_visits)
    pos_flat = pos[:, :TOP_K].reshape(-1)
    tok = _invert(pos[:, TOP_K:2 * TOP_K].reshape(-1))
    vt = table[:n_visits, :VT_W].reshape(-1)
    xs = _dispatch(vt, tok, hn, n_visits)
    h = _moe_up(vt, xs, w_gate_up, b_gate_up[:, None, :], n_visits)
    ys = _moe_down(vt, h, w_down, b_down[:, None, :], n_visits)
    return _combine(pos_flat, ys, x1, tw)


def kernel(x, attn_norm, w_in, cmp_pos_k, w_cmp_k1, w_cmp_k2, cmp_pos_v, w_cmp_v1, w_cmp_v2, q_norm_nsa, k_norm_cmp, k_norm_slc, k_norm_win, q_norm_dil, k_norm_dil, out_norm_nsa, out_norm_dil, w_out, ffn_norm, w_router, b_router, w_gate_up, b_gate_up, w_down, b_down):
    batch, seq, d = x.shape
    x2 = x.reshape(batch * seq, d)
    for layer in range(attn_norm.shape[0]):
        x2 = _attention_block(
            x2, batch, seq, attn_norm[layer], w_in[layer], cmp_pos_k[layer], w_cmp_k1[layer],
            w_cmp_k2[layer], cmp_pos_v[layer], w_cmp_v1[layer], w_cmp_v2[layer], q_norm_nsa[layer],
            k_norm_cmp[layer], k_norm_slc[layer], k_norm_win[layer], q_norm_dil[layer],
            k_norm_dil[layer], out_norm_nsa[layer], out_norm_dil[layer], w_out[layer])
        x2 = _moe_block(x2, ffn_norm[layer], w_router[layer], b_router[layer], w_gate_up[layer],
                        b_gate_up[layer], w_down[layer], b_down[layer])
    return x2.reshape(batch, seq, d)
```

```python
import functools

import numpy as np
import jax
import jax.numpy as jnp
from jax import lax
from jax.experimental import pallas as pl
from jax.experimental.pallas import tpu as pltpu

F32, BF16, I32 = jnp.float32, jnp.bfloat16, jnp.int32

HEAD_DIM = 128
N_NSA_HEADS = 16
N_NSA_KV = 4
NSA_REP = N_NSA_HEADS // N_NSA_KV
N_DIL_HEADS = 16
CMP_BLOCK = 32
CMP_STRIDE = 16
CMP_HIDDEN = 256
SLC_BLOCK = 64
SLC_TOP_N = 16
FORCED_SCORE = 1.0e4
WIN_SIZE = 512
DIL_PATTERNS = ((128, 1), (512, 4), (2048, 16))
N_EXPERTS = 32
TOP_K = 4
D_EXPERT = 1536
SWIGLU_ALPHA = 1.702
SWIGLU_LIMIT = 7.0
RMS_EPS = 1e-6
NEG_INF = -1e30
ATTN_SCALE = HEAD_DIM ** -0.5
LOG2E = 1.4426950408889634

Q_NSA_DIM = N_NSA_HEADS * HEAD_DIM
KV_NSA_DIM = N_NSA_KV * HEAD_DIM
GATE_DIM = N_NSA_HEADS * 3
DIL_DIM = N_DIL_HEADS * HEAD_DIM

LANES = 128
SUBLANES = 8
VMEM_LIMIT = 56 * 1024 * 1024

CB_Q_NSA = 0
CB_K_CMP = CB_Q_NSA + N_NSA_HEADS
CB_V_CMP = CB_K_CMP + N_NSA_KV
CB_K_SLC = CB_V_CMP + N_NSA_KV
CB_V_SLC = CB_K_SLC + N_NSA_KV
CB_K_WIN = CB_V_SLC + N_NSA_KV
CB_V_WIN = CB_K_WIN + N_NSA_KV
CB_Q_DIL = CB_V_WIN + N_NSA_KV
CB_K_DIL = CB_Q_DIL + N_DIL_HEADS
CB_V_DIL = CB_K_DIL + N_DIL_HEADS
CB_GATE = CB_V_DIL + N_DIL_HEADS
PROJ_TN = 512
N_PROJ = -(-(CB_GATE + 1) * LANES // PROJ_TN) * PROJ_TN

PROJ_TM = 512
NSA_TQ = 256
DIL_TQ = 512
ATT_TK = 256
ROUTER_TM = 256
RANK_TM = 512
POS_TM = 1024
MOE_TM = 1536
MOE_SUB = 256
MOE_TF = 256
MOE_TN = 1024
INVERT_UNROLL = 16
GATHER_UNROLL = 8
DISPATCH_TM = 256
COMBINE_TM = 128
VT_W = 8
VT_TOTAL = 6
MOE_SPARE_VISITS = 8

POS_SHIFT = 8
POS_RADIX = 1 << POS_SHIFT
ALIBI_PIECES = 3
ALIBI_COLS = 2 * ALIBI_PIECES
SEL_COL0 = 64
MASK_BIG = 2.0 ** 100


def _dot(a, b, **kw):
    return jnp.dot(a, b, preferred_element_type=F32, **kw)


def _dot_nt(a, b, **kw):
    return lax.dot_general(a, b, (((1,), (1,)), ((), ())), preferred_element_type=F32, **kw)


def _rms_rows(x, gain):
    ms = jnp.mean(x * x, axis=-1, keepdims=True)
    return x * lax.rsqrt(ms + RMS_EPS) * gain


def _sigmoid(x):
    return 1.0 / (1.0 + jnp.exp(-x))


def _params(*sem):
    return pltpu.CompilerParams(dimension_semantics=sem, vmem_limit_bytes=VMEM_LIMIT)


def _in_proj_kernel(x_ref, g_ref, wa_ref, wb_ref, cg_ref, cf_ref, o_ref, h_scr, *, na_tiles):
    j = pl.program_id(1)

    @pl.when(j == 0)
    def _():
        h_scr[...] = _rms_rows(x_ref[...], g_ref[...]).astype(BF16)

    def tile(w_ref):
        acc = _dot(h_scr[...], w_ref[...])
        for c in range(PROJ_TN // LANES):
            cs = slice(c * LANES, (c + 1) * LANES)
            a = acc[:, cs]
            r = lax.rsqrt(jnp.mean(a * a, axis=-1, keepdims=True) + RMS_EPS)
            y = a * jnp.where(cf_ref[:, cs] > 0, r, 1.0) * cg_ref[:, cs]
            o_ref[:, cs] = y.astype(o_ref.dtype)

    pl.when(j < na_tiles)(functools.partial(tile, wa_ref))
    pl.when(j >= na_tiles)(functools.partial(tile, wb_ref))


def _in_proj(x2, gain, wa, wb, col_gain, col_flag):
    t, d = x2.shape
    na_tiles = wa.shape[1] // PROJ_TN
    n = wa.shape[1] + wb.shape[1]
    return pl.pallas_call(
        functools.partial(_in_proj_kernel, na_tiles=na_tiles),
        out_shape=jax.ShapeDtypeStruct((t, n), BF16),
        grid=(t // PROJ_TM, n // PROJ_TN),
        in_specs=[
            pl.BlockSpec((PROJ_TM, d), lambda i, j: (i, 0)),
            pl.BlockSpec((1, d), lambda i, j: (0, 0)),
            pl.BlockSpec((d, PROJ_TN), lambda i, j: (0, jnp.minimum(j, na_tiles - 1))),
            pl.BlockSpec((d, PROJ_TN), lambda i, j: (0, jnp.maximum(j - na_tiles, 0))),
            pl.BlockSpec((1, PROJ_TN), lambda i, j: (0, j)),
            pl.BlockSpec((1, PROJ_TN), lambda i, j: (0, j)),
        ],
        out_specs=pl.BlockSpec((PROJ_TM, PROJ_TN), lambda i, j: (i, j)),
        scratch_shapes=[pltpu.VMEM((PROJ_TM, d), BF16)],
        compiler_params=_params("parallel", "arbitrary"),
        name="in_proj",
    )(x2, gain, wa, wb, col_gain, col_flag)


def _compress_kernel(a_ref, pe_ref, w1_ref, w2_ref, kn_ref, o_ref):
    a = (a_ref[...].astype(F32) + pe_ref[...]).astype(BF16)
    hid = _dot(a, w1_ref[...])
    hid = hid * _sigmoid(hid)
    out = _dot(hid.astype(BF16), w2_ref[...])
    normed = _rms_rows(out, kn_ref[...])
    o_ref[...] = jnp.where(pl.program_id(0) == 0, normed, out).astype(o_ref.dtype)


def _compress(blocks, pe, w1, w2, kn):
    two, bg, ncp, flat = blocks.shape
    return pl.pallas_call(
        _compress_kernel,
        out_shape=jax.ShapeDtypeStruct((two, bg, ncp, HEAD_DIM), BF16),
        grid=(two, bg),
        in_specs=[
            pl.BlockSpec((None, None, ncp, flat), lambda s, i: (s, i, 0, 0)),
            pl.BlockSpec((None, 1, flat), lambda s, i: (s, 0, 0)),
            pl.BlockSpec((None, flat, CMP_HIDDEN), lambda s, i: (s, 0, 0)),
            pl.BlockSpec((None, CMP_HIDDEN, HEAD_DIM), lambda s, i: (s, 0, 0)),
            pl.BlockSpec((1, HEAD_DIM), lambda s, i: (0, 0)),
        ],
        out_specs=pl.BlockSpec((None, None, ncp, HEAD_DIM), lambda s, i: (s, i, 0, 0)),
        compiler_params=_params("arbitrary", "arbitrary"),
        name="nsa_compress",
    )(blocks, pe, w1, w2, kn)


def _split_position(v):
    hi = np.floor_divide(v, POS_RADIX)
    return hi, v - POS_RADIX * hi


def _rel_position_table(seq):
    hi, lo = _split_position(np.arange(2 * seq) - seq)
    tab = np.zeros((2 * seq, LANES), np.float32)
    tab[:, 0:ALIBI_COLS:2] = hi[:, None]
    tab[:, 1:ALIBI_COLS:2] = lo[:, None]
    return jnp.asarray(tab, BF16)


def _block_onehot_table(seq):
    tab = np.zeros((seq, LANES), np.float32)
    tab[np.arange(seq), SEL_COL0 + np.arange(seq) // SLC_BLOCK] = 1.0
    return jnp.asarray(tab, BF16)


def _alibi_rows(slopes):
    rest = slopes.astype(F32) * LOG2E
    cols = []
    for _ in range(ALIBI_PIECES):
        piece = rest.astype(BF16).astype(F32)
        rest = rest - piece
        cols += [piece * POS_RADIX, piece]
    rows = jnp.concatenate([jnp.stack(cols, axis=1),
                            jnp.zeros((slopes.shape[0], LANES - ALIBI_COLS), F32)], axis=1)
    return jnp.broadcast_to(rows[:, None, :], (slopes.shape[0], SUBLANES, LANES))


def _tile_distance(u):
    return LANES * u + np.arange(LANES)[:, None] - np.arange(ATT_TK)[None, :]


def _mask_table(n, valid_fn):
    tabs = [np.where(valid_fn(_tile_distance(u)), 0.0, -MASK_BIG) for u in range(n)]
    tabs += [np.zeros((LANES, ATT_TK)), np.full((LANES, ATT_TK), -MASK_BIG)]
    return jnp.asarray(np.stack(tabs), F32)


def _dilation_table():
    reach = max(w for w, _ in DIL_PATTERNS)
    n = (reach + ATT_TK - 1) // LANES + 1
    tabs = []
    for u in range(n):
        d = _tile_distance(u)
        mult = sum(((d >= 0) & (d <= w) & (d % dil == 0)).astype(np.float32) for w, dil in DIL_PATTERNS)
        tabs.append(np.where(mult > 0, np.log2(np.maximum(mult, 1.0)), -MASK_BIG))
    tabs.append(np.full((LANES, ATT_TK), -MASK_BIG))
    return jnp.asarray(np.stack(tabs), F32)


def _softmax_tile(s_buf, p_buf, m_scr, l_scr, acc_scr, bias_for_block):
    for rb in range(s_buf.shape[0] // LANES):
        rs = slice(rb * LANES, (rb + 1) * LANES)
        s = s_buf[rs, :] + bias_for_block(rb)
        m_old = m_scr[rs, :]
        m_new = jnp.maximum(m_old, jnp.max(s, axis=-1, keepdims=True))
        alpha = jnp.exp2(m_old - m_new)
        p = [jnp.exp2(s[:, c * LANES:(c + 1) * LANES] - m_new) for c in range(ATT_TK // LANES)]
        total = functools.reduce(lambda a, b: a + b, p)
        l_scr[rs, :] = alpha * l_scr[rs, :] + jnp.sum(total, axis=-1, keepdims=True)
        acc_scr[rs, :] = alpha * acc_scr[rs, :]
        m_scr[rs, :] = m_new
        for c in range(ATT_TK // LANES):
            p_buf[rs, c * LANES:(c + 1) * LANES] = p[c].astype(BF16)


def _softmax_init(m_scr, l_scr, acc_scr):
    m_scr[...] = jnp.full(m_scr.shape, NEG_INF, F32)
    l_scr[...] = jnp.zeros(l_scr.shape, F32)
    acc_scr[...] = jnp.zeros(acc_scr.shape, F32)


def _attend(first, end, qa_scr, s_bufs, p_scr, m_scr, l_scr, acc_scr, load_k_aug, load_v, bias_for_tile):
    s_a, s_b = s_bufs
    last = end - 1

    def scores(kt, s_buf):
        s_buf[...] = _dot_nt(qa_scr[...], load_k_aug(jnp.minimum(kt, last)))

    def absorb(kt, s_buf):
        _softmax_tile(s_buf, p_scr, m_scr, l_scr, acc_scr, bias_for_tile(kt))
        acc_scr[...] += _dot(p_scr[...], load_v(kt))

    scores(first, s_a)

    def pair(j, carry):
        a = first + 2 * j
        scores(a + 1, s_b)
        absorb(a, s_a)
        scores(a + 2, s_a)
        absorb(a + 1, s_b)
        return carry

    lax.fori_loop(0, (end - first) // 2, pair, 0)

    @pl.when((end - first) % 2 == 1)
    def _():
        absorb(last, s_a)


def _nsa_kernel(q_ref, qx_ref, kc_ref, vc_ref, ks_ref, vs_ref, kw_ref, vw_ref, rel_ref, oh_ref,
                wtab_ref, ctab_ref, gt_ref, o_ref,
                qa_scr, s_a_scr, s_b_scr, p_scr, m_scr, l_scr, acc_scr, ob_scr, sc_scr, *, seq):
    s_bufs = (s_a_scr, s_b_scr)
    t0 = pl.program_id(2) * NSA_TQ
    n_slc = seq // SLC_BLOCK
    heads = [slice(r * NSA_TQ, (r + 1) * NSA_TQ) for r in range(NSA_REP)]
    alibi = [jnp.broadcast_to(qx_ref[r, 0:1, :], (NSA_TQ, LANES)) for r in range(NSA_REP)]
    for r in range(NSA_REP):
        qa_scr[heads[r], :HEAD_DIM] = q_ref[:, r * HEAD_DIM:(r + 1) * HEAD_DIM]
        qa_scr[heads[r], HEAD_DIM:] = alibi[r].astype(BF16)

    ncp = kc_ref.shape[0]
    n_i = lax.broadcasted_iota(I32, (ncp, LANES), 0)
    ln = lax.broadcasted_iota(I32, (ncp, LANES), 1)
    rel_end = n_i * CMP_STRIDE + (CMP_BLOCK - 1) - t0
    digits = jnp.where((ln & 1) == 0, rel_end >> POS_SHIFT, rel_end & (POS_RADIX - 1))
    feat = jnp.where(ln < ALIBI_COLS, digits, 0).astype(F32).astype(BF16)
    kc_aug = jnp.concatenate([kc_ref[...], feat], axis=1)
    row = lax.broadcasted_iota(I32, (NSA_TQ, 1), 0) + t0
    blk_end = lax.broadcasted_iota(I32, (NSA_TQ, ncp), 1) * CMP_STRIDE + (CMP_BLOCK - 1)
    valid_c = row >= blk_end
    vc = vc_ref[...]
    psum = jnp.zeros((NSA_TQ, ncp), F32)
    for r in range(NSA_REP):
        s = jnp.where(valid_c, _dot_nt(qa_scr[heads[r], :], kc_aug), NEG_INF)
        mc = jnp.max(s, axis=-1, keepdims=True)
        pc = jnp.where(valid_c, jnp.exp2(s - mc), 0.0)
        p = pc / jnp.maximum(jnp.sum(pc, axis=-1, keepdims=True), 1e-30)
        psum = psum + p
        ob_scr[0, heads[r], :] = _dot(p.astype(BF16), vc)

    per_slc = SLC_BLOCK // CMP_STRIDE
    back = CMP_BLOCK // CMP_STRIDE - 1
    jb = lax.broadcasted_iota(I32, (LANES, ncp), 0)
    nb = lax.broadcasted_iota(I32, (LANES, ncp), 1)
    overlap_t = jnp.where((nb >= per_slc * jb - back) & (nb < per_slc * (jb + 1)), 1.0, 0.0)
    imp_t = _dot_nt(overlap_t, psum, precision=lax.Precision.HIGHEST)
    j = lax.broadcasted_iota(I32, (LANES, NSA_TQ), 0)
    tq = lax.broadcasted_iota(I32, (LANES, NSA_TQ), 1) + t0
    cur = tq // SLC_BLOCK
    forced = (j == 0) | (j == cur) | (j == cur - 1)
    sc_scr[...] = jnp.where(forced, FORCED_SCORE, jnp.where(j * SLC_BLOCK <= tq, imp_t, -1.0))

    n_slab = n_slc // SUBLANES
    slabs = [sc_scr[SUBLANES * v:SUBLANES * (v + 1), :] for v in range(n_slab)]
    ranks = [jnp.zeros((SUBLANES, NSA_TQ), F32) for _ in range(n_slab)]
    sub = lax.broadcasted_iota(I32, (SUBLANES, NSA_TQ), 0)
    for i in range(n_slc):
        rival = jnp.broadcast_to(sc_scr[i:i + 1, :], (SUBLANES, NSA_TQ))
        for v in range(n_slab):
            wins_ties = jnp.where(rival >= slabs[v], 1.0, 0.0)
            loses_ties = jnp.where(rival > slabs[v], 1.0, 0.0)
            if SUBLANES * v > i:
                ranks[v] = ranks[v] + wins_ties
            elif SUBLANES * (v + 1) <= i:
                ranks[v] = ranks[v] + loses_ties
            else:
                ranks[v] = ranks[v] + jnp.where(sub + SUBLANES * v > i, wins_ties, loses_ties)
    n_top = min(SLC_TOP_N, n_slc)
    pen_rows = [jnp.zeros((SEL_COL0, NSA_TQ), F32)]
    pen_rows += [jnp.where(rk < n_top, 0.0, -MASK_BIG) for rk in ranks]
    if SEL_COL0 + n_slc < LANES:
        pen_rows.append(jnp.zeros((LANES - SEL_COL0 - n_slc, NSA_TQ), F32))
    penalty = jnp.concatenate(pen_rows, axis=0).T
    for r in range(NSA_REP):
        qa_scr[heads[r], HEAD_DIM:] = (alibi[r] + penalty).astype(BF16)

    def tile_rows(kt):
        return pl.ds(pl.multiple_of(kt * ATT_TK, ATT_TK), ATT_TK)

    def k_aug(kt, k_ref, with_blocks):
        extra = rel_ref[pl.ds(pl.multiple_of(kt * ATT_TK - t0 + seq, LANES), ATT_TK), :]
        if with_blocks:
            extra = extra + oh_ref[tile_rows(kt), :]
        return jnp.concatenate([k_ref[tile_rows(kt), :], extra], axis=1)

    q_blocks = NSA_TQ // LANES
    last_kt = (t0 + NSA_TQ - 1) // ATT_TK

    def tile_shift(kt, rb):
        return (t0 + (rb % q_blocks) * LANES - kt * ATT_TK) // LANES

    def masks(tab_ref, index):
        return lambda kt: (lambda rb: tab_ref[index(kt, rb)])

    unmasked = ctab_ref.shape[0] - 2
    _softmax_init(m_scr, l_scr, acc_scr)
    _attend(0, last_kt + 1, qa_scr, s_bufs, p_scr, m_scr, l_scr, acc_scr,
            lambda kt: k_aug(kt, ks_ref, True), lambda kt: vs_ref[tile_rows(kt), :],
            masks(ctab_ref, lambda kt, qb: jnp.where(kt == last_kt, tile_shift(kt, qb), unmasked)))
    ob_scr[1] = acc_scr[...] / l_scr[...]

    _softmax_init(m_scr, l_scr, acc_scr)
    first_kt = jnp.maximum(t0 - (WIN_SIZE - 1), 0) // ATT_TK
    _attend(first_kt, last_kt + 1, qa_scr, s_bufs, p_scr, m_scr, l_scr, acc_scr,
            lambda kt: k_aug(kt, kw_ref, False), lambda kt: vw_ref[tile_rows(kt), :],
            masks(wtab_ref, tile_shift))

    gate = _sigmoid(gt_ref[...].astype(F32))
    for r in range(NSA_REP):
        o_win = acc_scr[heads[r], :] / l_scr[heads[r], :]
        o = (gate[:, 3 * r:3 * r + 1] * ob_scr[0, heads[r], :] + gate[:, 3 * r + 1:3 * r + 2] * ob_scr[1, heads[r], :]
             + gate[:, 3 * r + 2:3 * r + 3] * o_win)
        o_ref[:, r * HEAD_DIM:(r + 1) * HEAD_DIM] = o.astype(o_ref.dtype)


def _nsa_mixer(slopes, proj, cmp_kv, gates_t, rel_tab, batch, seq):
    t = proj.shape[0]
    nq = seq // NSA_TQ
    ncp = cmp_kv.shape[2]
    n_slc = seq // SLC_BLOCK
    assert SEL_COL0 + n_slc <= LANES and n_slc % SUBLANES == 0 and seq % ATT_TK == 0
    rows = NSA_REP * NSA_TQ
    win_shifts = (WIN_SIZE - 1 + ATT_TK - 1 + NSA_TQ - LANES) // LANES + 1
    win_tab = _mask_table(win_shifts, lambda d: (d >= 0) & (d <= WIN_SIZE - 1))
    causal_tab = _mask_table(ATT_TK // LANES, lambda d: d >= 0)

    def seq_spec(cb):
        return pl.BlockSpec((seq, HEAD_DIM), lambda b, g, i: (b, cb + g))

    def cmp_spec(which):
        return pl.BlockSpec((None, None, ncp, HEAD_DIM), lambda b, g, i: (which, b * N_NSA_KV + g, 0, 0))

    def whole(a):
        return pl.BlockSpec(a.shape, lambda b, g, i: (0,) * a.ndim)

    qo_spec = pl.BlockSpec((NSA_TQ, NSA_REP * HEAD_DIM), lambda b, g, i: (b * nq + i, g))
    onehot = _block_onehot_table(seq)
    qx = _alibi_rows(slopes)
    return pl.pallas_call(
        functools.partial(_nsa_kernel, seq=seq),
        out_shape=jax.ShapeDtypeStruct((t, Q_NSA_DIM), BF16),
        grid=(batch, N_NSA_KV, nq),
        in_specs=[
            qo_spec,
            pl.BlockSpec((NSA_REP, SUBLANES, LANES), lambda b, g, i: (g, 0, 0)),
            cmp_spec(0), cmp_spec(1),
            seq_spec(CB_K_SLC), seq_spec(CB_V_SLC), seq_spec(CB_K_WIN), seq_spec(CB_V_WIN),
            whole(rel_tab), whole(onehot), whole(win_tab), whole(causal_tab),
            pl.BlockSpec((None, NSA_TQ, 3 * NSA_REP), lambda b, g, i: (g, b * nq + i, 0)),
        ],
        out_specs=qo_spec,
        scratch_shapes=[
            pltpu.VMEM((rows, HEAD_DIM + LANES), BF16),
            pltpu.VMEM((rows, ATT_TK), F32),
            pltpu.VMEM((rows, ATT_TK), F32),
            pltpu.VMEM((rows, ATT_TK), BF16),
            pltpu.VMEM((rows, LANES), F32),
            pltpu.VMEM((rows, LANES), F32),
            pltpu.VMEM((rows, HEAD_DIM), F32),
            pltpu.VMEM((2, rows, HEAD_DIM), F32),
            pltpu.VMEM((LANES, NSA_TQ), F32),
        ],
        compiler_params=_params("arbitrary", "arbitrary", "arbitrary"),
        name="nsa_mixer",
    )(proj, qx, cmp_kv, cmp_kv, proj, proj, proj, proj, rel_tab, onehot, win_tab, causal_tab, gates_t)


def _dil_kernel(q_ref, qx_ref, k_ref, v_ref, rel_ref, mtab_ref, o_ref,
                qa_scr, s_a_scr, s_b_scr, p_scr, m_scr, l_scr, acc_scr, *, seq):
    t0 = pl.program_id(2) * DIL_TQ
    qa_scr[:, :HEAD_DIM] = q_ref[...]
    qa_scr[:, HEAD_DIM:] = jnp.broadcast_to(qx_ref[0:1, :], (DIL_TQ, LANES)).astype(BF16)
    _softmax_init(m_scr, l_scr, acc_scr)
    masked = mtab_ref.shape[0] - 1

    def tile_rows(kt):
        return pl.ds(pl.multiple_of(kt * ATT_TK, ATT_TK), ATT_TK)

    def k_aug(kt):
        rel = rel_ref[pl.ds(pl.multiple_of(kt * ATT_TK - t0 + seq, LANES), ATT_TK), :]
        return jnp.concatenate([k_ref[tile_rows(kt), :], rel], axis=1)

    def multiplicity(kt):
        def for_block(rb):
            u = (t0 + rb * LANES - kt * ATT_TK) // LANES
            return mtab_ref[jnp.where((u < 0) | (u >= masked), masked, u)]
        return for_block

    reach = max(w for w, _ in DIL_PATTERNS)
    first_kt = jnp.maximum(t0 - reach, 0) // ATT_TK
    _attend(first_kt, (t0 + DIL_TQ - 1) // ATT_TK + 1, qa_scr, (s_a_scr, s_b_scr), p_scr,
            m_scr, l_scr, acc_scr, k_aug, lambda kt: v_ref[tile_rows(kt), :], multiplicity)
    o_ref[...] = (acc_scr[...] / l_scr[...]).astype(o_ref.dtype)


def _dil_mixer(slopes, proj, rel_tab, batch, seq):
    t = proj.shape[0]
    nq = seq // DIL_TQ
    mult_tab = _dilation_table()
    qx = _alibi_rows(slopes)

    def whole(a):
        return pl.BlockSpec(a.shape, lambda b, h, i: (0,) * a.ndim)

    return pl.pallas_call(
        functools.partial(_dil_kernel, seq=seq),
        out_shape=jax.ShapeDtypeStruct((t, DIL_DIM), BF16),
        grid=(batch, N_DIL_HEADS, nq),
        in_specs=[
            pl.BlockSpec((DIL_TQ, HEAD_DIM), lambda b, h, i: (b * nq + i, CB_Q_DIL + h)),
            pl.BlockSpec((None, SUBLANES, LANES), lambda b, h, i: (h, 0, 0)),
            pl.BlockSpec((seq, HEAD_DIM), lambda b, h, i: (b, CB_K_DIL + h)),
            pl.BlockSpec((seq, HEAD_DIM), lambda b, h, i: (b, CB_V_DIL + h)),
            whole(rel_tab), whole(mult_tab),
        ],
        out_specs=pl.BlockSpec((DIL_TQ, HEAD_DIM), lambda b, h, i: (b * nq + i, h)),
        scratch_shapes=[
            pltpu.VMEM((DIL_TQ, HEAD_DIM + LANES), BF16),
            pltpu.VMEM((DIL_TQ, ATT_TK), F32),
            pltpu.VMEM((DIL_TQ, ATT_TK), F32),
            pltpu.VMEM((DIL_TQ, ATT_TK), BF16),
            pltpu.VMEM((DIL_TQ, LANES), F32),
            pltpu.VMEM((DIL_TQ, LANES), F32),
            pltpu.VMEM((DIL_TQ, HEAD_DIM), F32),
        ],
        compiler_params=_params("arbitrary", "arbitrary", "arbitrary"),
        name="dil_mixer",
    )(proj, qx, proj, proj, rel_tab, mult_tab)


def _out_proj_kernel(a_ref, b_ref, ga_ref, gb_ref, w_ref, res_ref, o_ref, h_scr):
    @pl.when(pl.program_id(1) == 0)
    def _():
        na = a_ref.shape[1]
        h_scr[:, :na] = _rms_rows(a_ref[...].astype(F32), ga_ref[...]).astype(BF16)
        h_scr[:, na:] = _rms_rows(b_ref[...].astype(F32), gb_ref[...]).astype(BF16)

    o_ref[...] = res_ref[...] + _dot(h_scr[...], w_ref[...])


def _out_proj(o_nsa, o_dil, g_nsa, g_dil, w, resid):
    t, d = resid.shape
    na, nb = o_nsa.shape[1], o_dil.shape[1]
    return pl.pallas_call(
        _out_proj_kernel,
        out_shape=jax.ShapeDtypeStruct((t, d), F32),
        grid=(t // PROJ_TM, d // PROJ_TN),
        in_specs=[
            pl.BlockSpec((PROJ_TM, na), lambda i, j: (i, 0)),
            pl.BlockSpec((PROJ_TM, nb), lambda i, j: (i, 0)),
            pl.BlockSpec((1, na), lambda i, j: (0, 0)),
            pl.BlockSpec((1, nb), lambda i, j: (0, 0)),
            pl.BlockSpec((na + nb, PROJ_TN), lambda i, j: (0, j)),
            pl.BlockSpec((PROJ_TM, PROJ_TN), lambda i, j: (i, j)),
        ],
        out_specs=pl.BlockSpec((PROJ_TM, PROJ_TN), lambda i, j: (i, j)),
        scratch_shapes=[pltpu.VMEM((PROJ_TM, na + nb), BF16)],
        compiler_params=_params("parallel", "arbitrary"),
        name="out_proj",
    )(o_nsa, o_dil, g_nsa, g_dil, w, resid)


def _router_kernel(x_ref, g_ref, wr_ref, br_ref, hn_ref, idx_ref, tw_ref):
    h = _rms_rows(x_ref[...], g_ref[...])
    hn_ref[...] = h
    logits = _dot(h, wr_ref[...], precision=lax.Precision.HIGHEST) + br_ref[...]
    lane = lax.broadcasted_iota(I32, logits.shape, 1)
    lanef = lane.astype(F32)
    work = jnp.where(lane < N_EXPERTS, logits, -jnp.inf)
    vals, ids = [], []
    for _ in range(TOP_K):
        mx = jnp.max(work, axis=-1, keepdims=True)
        first = jnp.min(jnp.where(work == mx, lanef, float(LANES)), axis=-1, keepdims=True)
        vals.append(mx)
        ids.append(first)
        work = jnp.where(lanef == first, -jnp.inf, work)
    es = [jnp.exp(v - vals[0]) for v in vals]
    den = functools.reduce(lambda a, b: a + b, es)
    idx_out = jnp.zeros(logits.shape, F32)
    tw_out = jnp.zeros(logits.shape, F32)
    for k in range(TOP_K):
        idx_out = jnp.where(lane == k, ids[k], idx_out)
        tw_out = jnp.where(lane == k, es[k] / den, tw_out)
    idx_ref[...] = idx_out.astype(I32)
    tw_ref[...] = tw_out


def _router(x1, gain, wr, br):
    t, d = x1.shape
    return pl.pallas_call(
        _router_kernel,
        out_shape=(jax.ShapeDtypeStruct((t, d), F32),
                   jax.ShapeDtypeStruct((t, LANES), I32),
                   jax.ShapeDtypeStruct((t, LANES), F32)),
        grid=(t // ROUTER_TM,),
        in_specs=[
            pl.BlockSpec((ROUTER_TM, d), lambda i: (i, 0)),
            pl.BlockSpec((1, d), lambda i: (0, 0)),
            pl.BlockSpec((d, LANES), lambda i: (0, 0)),
            pl.BlockSpec((1, LANES), lambda i: (0, 0)),
        ],
        out_specs=(pl.BlockSpec((ROUTER_TM, d), lambda i: (i, 0)),
                   pl.BlockSpec((ROUTER_TM, LANES), lambda i: (i, 0)),
                   pl.BlockSpec((ROUTER_TM, LANES), lambda i: (i, 0))),
        compiler_params=_params("parallel"),
        name="moe_router",
    )(x1, gain, wr, br)


def _rank_kernel(idx_ref, rank_ref, cnt_ref, carry_scr):
    @pl.when(pl.program_id(0) == 0)
    def _():
        carry_scr[...] = jnp.zeros(carry_scr.shape, F32)

    idx = idx_ref[...]
    lane = lax.broadcasted_iota(I32, idx.shape, 1)
    hits = [lane == idx[:, k:k + 1] for k in range(TOP_K)]
    onehot = functools.reduce(lambda a, b: a + b, [jnp.where(h, 1.0, 0.0) for h in hits])
    ri = lax.broadcasted_iota(I32, (RANK_TM, RANK_TM), 0)
    ci = lax.broadcasted_iota(I32, (RANK_TM, RANK_TM), 1)
    before = jnp.where(ci < ri, 1.0, 0.0).astype(BF16)
    rank = _dot(before, onehot.astype(BF16)) + carry_scr[0:1, :]
    out = jnp.zeros(idx.shape, F32)
    for k in range(TOP_K):
        mine = jnp.sum(jnp.where(hits[k], rank, 0.0), axis=-1, keepdims=True)
        out = jnp.where(lane == k, mine, out)
    rank_ref[...] = out.astype(I32)
    carry = carry_scr[...] + jnp.sum(onehot, axis=0, keepdims=True)
    carry_scr[...] = carry
    cnt_ref[...] = carry


def _rank(idx):
    t = idx.shape[0]
    return pl.pallas_call(
        _rank_kernel,
        out_shape=(jax.ShapeDtypeStruct((t, LANES), I32), jax.ShapeDtypeStruct((SUBLANES, LANES), F32)),
        grid=(t // RANK_TM,),
        in_specs=[pl.BlockSpec((RANK_TM, LANES), lambda i: (i, 0))],
        out_specs=(pl.BlockSpec((RANK_TM, LANES), lambda i: (i, 0)),
                   pl.BlockSpec((SUBLANES, LANES), lambda i: (0, 0))),
        scratch_shapes=[pltpu.VMEM((SUBLANES, LANES), F32)],
        compiler_params=_params("arbitrary"),
        name="moe_rank",
    )(idx)


def _pos_kernel(cnt_ref, idx_ref, rank_ref, pos_ref, vt_ref):
    cnt = cnt_ref[0:1, :]
    lane1 = lax.broadcasted_iota(I32, (1, LANES), 1)
    tiles = jnp.where(lane1 < N_EXPERTS, jnp.floor((cnt + (MOE_TM - 0.5)) / MOE_TM), 0.0)
    tiles_before = jnp.zeros((1, LANES), F32)
    rows_before = jnp.zeros((1, LANES), F32)
    for e in range(N_EXPERTS):
        tiles_before = tiles_before + jnp.where(lane1 > e, tiles[:, e:e + 1], 0.0)
        rows_before = rows_before + jnp.where(lane1 > e, cnt[:, e:e + 1], 0.0)
    tiles_through = tiles_before + tiles

    idx = idx_ref[...]
    rank = rank_ref[...].astype(F32)
    lane = lax.broadcasted_iota(I32, idx.shape, 1)
    out = jnp.zeros(idx.shape, F32)
    for k in range(TOP_K):
        mine = lane == idx[:, k:k + 1]
        row0 = jnp.sum(jnp.where(mine, tiles_before * MOE_TM, 0.0), axis=-1, keepdims=True)
        slot0 = jnp.sum(jnp.where(mine, rows_before, 0.0), axis=-1, keepdims=True)
        out = jnp.where(lane == k, row0 + rank[:, k:k + 1], out)
        out = jnp.where(lane == TOP_K + k, slot0 + rank[:, k:k + 1], out)
    pos_ref[...] = out.astype(I32)

    nv = vt_ref.shape[0]
    vl = lax.broadcasted_iota(I32, (nv, LANES), 1)
    v = lax.broadcasted_iota(I32, (nv, LANES), 0).astype(F32)
    total = jnp.sum(tiles, axis=-1, keepdims=True)
    vv = jnp.minimum(v, total - 1.0)
    done = jnp.where((tiles_through <= vv) & (vl < N_EXPERTS), 1.0, 0.0)
    e_v = jnp.minimum(jnp.sum(done, axis=-1, keepdims=True), float(N_EXPERTS - 1))
    mine = vl.astype(F32) == e_v
    first_tile = jnp.sum(jnp.where(mine, tiles_before, 0.0), axis=-1, keepdims=True)
    e_cnt = jnp.sum(jnp.where(mine, cnt, 0.0), axis=-1, keepdims=True)
    e_slot = jnp.sum(jnp.where(mine, rows_before, 0.0), axis=-1, keepdims=True)
    vv1 = vv[:, 0:1]
    valid = v[:, 0:1] < total
    offset = (vv1 - first_tile) * MOE_TM
    held = jnp.where(valid, jnp.clip(e_cnt - offset, 0.0, float(MOE_TM)), 0.0)
    nrows = jnp.ceil(held * (1.0 / MOE_SUB)) * MOE_SUB
    cols = [e_v, vv1, nrows, jnp.where(valid, 1.0, 0.0), e_slot + offset, held, total]
    table = jnp.zeros((nv, LANES), F32)
    for c, val in enumerate(cols):
        table = jnp.where(vl == c, val, table)
    vt_ref[...] = table.astype(I32)


def _positions(cnt, idx, rank, n_visits):
    t = idx.shape[0]
    nvp = -(-n_visits // SUBLANES) * SUBLANES
    return pl.pallas_call(
        _pos_kernel,
        out_shape=(jax.ShapeDtypeStruct((t, LANES), I32), jax.ShapeDtypeStruct((nvp, LANES), I32)),
        grid=(t // POS_TM,),
        in_specs=[pl.BlockSpec((SUBLANES, LANES), lambda i: (0, 0)),
                  pl.BlockSpec((POS_TM, LANES), lambda i: (i, 0)),
                  pl.BlockSpec((POS_TM, LANES), lambda i: (i, 0))],
        out_specs=(pl.BlockSpec((POS_TM, LANES), lambda i: (i, 0)),
                   pl.BlockSpec((nvp, LANES), lambda i: (0, 0))),
        compiler_params=_params("arbitrary"),
        name="moe_positions",
    )(cnt, idx, rank)


def _invert_kernel(slot_ref, tok_ref):
    def body(g, carry):
        base = g * INVERT_UNROLL
        slots = [slot_ref[base + u] for u in range(INVERT_UNROLL)]
        for u in range(INVERT_UNROLL):
            tok_ref[slots[u]] = (base + u) >> top_k_shift
        return carry

    top_k_shift = TOP_K.bit_length() - 1
    assert TOP_K == 1 << top_k_shift and slot_ref.shape[0] % INVERT_UNROLL == 0
    lax.fori_loop(0, slot_ref.shape[0] // INVERT_UNROLL, body, 0)


def _invert(slot_flat):
    return pl.pallas_call(
        _invert_kernel,
        out_shape=jax.ShapeDtypeStruct(slot_flat.shape, I32),
        in_specs=[pl.BlockSpec(memory_space=pltpu.SMEM)],
        out_specs=pl.BlockSpec(memory_space=pltpu.SMEM),
        name="moe_invert",
    )(slot_flat)


def _dispatch_kernel(vt_ref, tok_ref, hn_ref, xs_ref, buf, sems):
    i = pl.program_id(0)
    per_visit = MOE_TM // DISPATCH_TM

    def held_rows(step):
        v = step // per_visit
        part = (step % per_visit) * DISPATCH_TM
        return jnp.clip(vt_ref[v * VT_W + 5] - part, 0, DISPATCH_TM), vt_ref[v * VT_W + 4] + part

    def row_copy(src_row, r, slot):
        return pltpu.make_async_copy(hn_ref.at[pl.ds(src_row, 1)], buf.at[slot, pl.ds(r, 1)], sems.at[slot])

    def start(step, slot):
        n, base = held_rows(step)
        issue = lambda r, c: (row_copy(tok_ref[base + r], r, slot).start(), c)[1]

        @pl.when(n == DISPATCH_TM)
        def _():
            lax.fori_loop(0, DISPATCH_TM, issue, 0, unroll=GATHER_UNROLL)

        @pl.when(n < DISPATCH_TM)
        def _():
            lax.fori_loop(0, n, issue, 0)

    def wait(step, slot):
        n, _ = held_rows(step)

        @pl.when(n == DISPATCH_TM)
        def _():
            pltpu.make_async_copy(hn_ref.at[pl.ds(0, DISPATCH_TM)], buf.at[slot], sems.at[slot]).wait()

        @pl.when(n < DISPATCH_TM)
        def _():
            lax.fori_loop(0, n, lambda r, c: (row_copy(0, r, slot).wait(), c)[1], 0)

    @pl.when(i == 0)
    def _():
        buf[...] = jnp.zeros(buf.shape, buf.dtype)
        start(0, 0)

    @pl.when(i + 1 < pl.num_programs(0))
    def _():
        start(i + 1, (i + 1) % 2)

    wait(i, i % 2)
    n, _ = held_rows(i)
    rowid = lax.broadcasted_iota(I32, (DISPATCH_TM, 1), 0)
    xs_ref[...] = jnp.where(rowid < n, buf[i % 2], 0.0).astype(xs_ref.dtype)


def _dispatch(vt, tok, hn, n_visits):
    d = hn.shape[1]
    return pl.pallas_call(
        _dispatch_kernel,
        out_shape=jax.ShapeDtypeStruct((n_visits * MOE_TM, d), BF16),
        grid_spec=pltpu.PrefetchScalarGridSpec(
            num_scalar_prefetch=2,
            grid=(n_visits * MOE_TM // DISPATCH_TM,),
            in_specs=[pl.BlockSpec(memory_space=pl.ANY)],
            out_specs=pl.BlockSpec((DISPATCH_TM, d), lambda i, vt, tok: (i, 0)),
            scratch_shapes=[pltpu.VMEM((2, DISPATCH_TM, d), F32), pltpu.SemaphoreType.DMA((2,))],
        ),
        compiler_params=_params("arbitrary"),
        name="moe_dispatch",
    )(vt, tok, hn)


def _moe_up_kernel(vt_ref, xs_ref, wg_ref, wu_ref, bg_ref, bu_ref, h_ref, wg_scr, wu_scr):
    nrows = vt_ref[pl.program_id(0) * VT_W + 2]

    for sb in range(MOE_TM // MOE_SUB):
        rs = slice(sb * MOE_SUB, (sb + 1) * MOE_SUB)

        @pl.when(sb * MOE_SUB < nrows)
        def _(sb=sb, rs=rs):
            if sb == 0:
                wg_scr[...] = wg_ref[...].astype(BF16)
                wu_scr[...] = wu_ref[...].astype(BF16)
            x = xs_ref[rs, :]
            gate = jnp.minimum(_dot(x, wg_scr[...]) + bg_ref[...], SWIGLU_LIMIT)
            up = jnp.clip(_dot(x, wu_scr[...]) + bu_ref[...], -SWIGLU_LIMIT, SWIGLU_LIMIT)
            act = (up + 1.0) * (gate * _sigmoid(SWIGLU_ALPHA * gate))
            h_ref[rs, :] = act.astype(h_ref.dtype)

        @pl.when(sb * MOE_SUB >= nrows)
        def _():
            h_ref[rs, :] = jnp.zeros((MOE_SUB, h_ref.shape[1]), h_ref.dtype)


def _moe_up(vt, xs, w_gate_up, b_gate_up, n_visits):
    d = w_gate_up.shape[1]
    nf = D_EXPERT // MOE_TF

    def fsel(v, f, vt):
        return jnp.where(vt[v * VT_W + 3] > 0, f, nf - 1)

    return pl.pallas_call(
        _moe_up_kernel,
        out_shape=jax.ShapeDtypeStruct((n_visits * MOE_TM, D_EXPERT), BF16),
        grid_spec=pltpu.PrefetchScalarGridSpec(
            num_scalar_prefetch=1,
            grid=(n_visits, nf),
            in_specs=[
                pl.BlockSpec((MOE_TM, d), lambda v, f, vt: (vt[v * VT_W + 1], 0)),
                pl.BlockSpec((None, d, MOE_TF), lambda v, f, vt: (vt[v * VT_W], 0, fsel(v, f, vt))),
                pl.BlockSpec((None, d, MOE_TF), lambda v, f, vt: (vt[v * VT_W], 0, nf + fsel(v, f, vt))),
                pl.BlockSpec((None, 1, MOE_TF), lambda v, f, vt: (vt[v * VT_W], 0, fsel(v, f, vt))),
                pl.BlockSpec((None, 1, MOE_TF), lambda v, f, vt: (vt[v * VT_W], 0, nf + fsel(v, f, vt))),
            ],
            out_specs=pl.BlockSpec((MOE_TM, MOE_TF), lambda v, f, vt: (v, f)),
            scratch_shapes=[pltpu.VMEM((d, MOE_TF), BF16), pltpu.VMEM((d, MOE_TF), BF16)],
        ),
        compiler_params=_params("arbitrary", "arbitrary"),
        name="moe_up",
    )(vt, xs, w_gate_up, w_gate_up, b_gate_up, b_gate_up)


def _moe_down_kernel(vt_ref, h_ref, wd_ref, bd_ref, y_ref, wd_scr):
    nrows = vt_ref[pl.program_id(0) * VT_W + 2]

    for sb in range(MOE_TM // MOE_SUB):
        rs = slice(sb * MOE_SUB, (sb + 1) * MOE_SUB)

        @pl.when(sb * MOE_SUB < nrows)
        def _(sb=sb, rs=rs):
            if sb == 0:
                wd_scr[...] = wd_ref[...].astype(BF16)
            y_ref[rs, :] = _dot(h_ref[rs, :], wd_scr[...]) + bd_ref[...]

        @pl.when(sb * MOE_SUB >= nrows)
        def _():
            y_ref[rs, :] = jnp.zeros((MOE_SUB, y_ref.shape[1]), y_ref.dtype)


def _moe_down(vt, h, w_down, b_down, n_visits):
    d = w_down.shape[2]
    nj = d // MOE_TN

    def jsel(v, j, vt):
        return jnp.where(vt[v * VT_W + 3] > 0, j, nj - 1)

    return pl.pallas_call(
        _moe_down_kernel,
        out_shape=jax.ShapeDtypeStruct((n_visits * MOE_TM, d), F32),
        grid_spec=pltpu.PrefetchScalarGridSpec(
            num_scalar_prefetch=1,
            grid=(n_visits, nj),
            in_specs=[
                pl.BlockSpec((MOE_TM, D_EXPERT), lambda v, j, vt: (vt[v * VT_W + 1], 0)),
                pl.BlockSpec((None, D_EXPERT, MOE_TN), lambda v, j, vt: (vt[v * VT_W], 0, jsel(v, j, vt))),
                pl.BlockSpec((None, 1, MOE_TN), lambda v, j, vt: (vt[v * VT_W], 0, jsel(v, j, vt))),
            ],
            out_specs=pl.BlockSpec((MOE_TM, MOE_TN), lambda v, j, vt: (v, j)),
            scratch_shapes=[pltpu.VMEM((D_EXPERT, MOE_TN), BF16)],
        ),
        compiler_params=_params("arbitrary", "arbitrary"),
        name="moe_down",
    )(vt, h, w_down, b_down)


def _combine_kernel(pos_ref, ys_ref, x_ref, tw_ref, o_ref, buf, sems):
    i = pl.program_id(0)

    def start(step, slot):
        def issue(r, carry):
            for k in range(TOP_K):
                src = pos_ref[(step * COMBINE_TM + r) * TOP_K + k]
                pltpu.make_async_copy(ys_ref.at[pl.ds(src, 1)], buf.at[slot, k, pl.ds(r, 1)], sems.at[slot]).start()
            return carry
        lax.fori_loop(0, COMBINE_TM, issue, 0, unroll=GATHER_UNROLL)

    def wait(slot):
        for k in range(TOP_K):
            pltpu.make_async_copy(ys_ref.at[pl.ds(0, COMBINE_TM)], buf.at[slot, k], sems.at[slot]).wait()

    @pl.when(i == 0)
    def _():
        start(0, 0)

    @pl.when(i + 1 < pl.num_programs(0))
    def _():
        start(i + 1, (i + 1) % 2)

    wait(i % 2)
    tw = tw_ref[...]
    acc = x_ref[...]
    for k in range(TOP_K):
        acc = acc + tw[:, k:k + 1] * buf[i % 2, k]
    o_ref[...] = acc


def _combine(pos_flat, ys, x1, tw):
    t, d = x1.shape
    return pl.pallas_call(
        _combine_kernel,
        out_shape=jax.ShapeDtypeStruct((t, d), F32),
        grid_spec=pltpu.PrefetchScalarGridSpec(
            num_scalar_prefetch=1,
            grid=(t // COMBINE_TM,),
            in_specs=[pl.BlockSpec(memory_space=pl.ANY),
                      pl.BlockSpec((COMBINE_TM, d), lambda i, pos: (i, 0)),
                      pl.BlockSpec((COMBINE_TM, LANES), lambda i, pos: (i, 0))],
            out_specs=pl.BlockSpec((COMBINE_TM, d), lambda i, pos: (i, 0)),
            scratch_shapes=[pltpu.VMEM((2, TOP_K, COMBINE_TM, d), F32), pltpu.SemaphoreType.DMA((2,))],
        ),
        compiler_params=_params("arbitrary"),
        name="moe_combine",
    )(pos_flat, ys, x1, tw)


def _alibi_slopes():
    n = N_NSA_HEADS + N_DIL_HEADS
    i = jnp.arange(1, n + 1, dtype=F32)
    return jnp.exp2(-8.0 * i / n)


def _attention_block(x2, batch, seq, attn_norm, w_in, pe_k, w_k1, w_k2, pe_v, w_v1, w_v2,
                     qn_nsa, kn_cmp, kn_slc, kn_win, qn_dil, kn_dil, on_nsa, on_dil, w_out):
    d = x2.shape[1]
    a_dim = Q_NSA_DIM + 6 * KV_NSA_DIM
    d_off = a_dim + GATE_DIM
    n_pad = N_PROJ - (a_dim + 3 * DIL_DIM + GATE_DIM)
    assert a_dim % PROJ_TN == 0
    wa = w_in[:, :a_dim].astype(BF16)
    wb = jnp.concatenate([w_in[:, d_off:].astype(BF16), w_in[:, a_dim:d_off].astype(BF16),
                          jnp.zeros((d, n_pad), BF16)], axis=1)
    ones_kv = jnp.ones((KV_NSA_DIM,), F32)
    ones_dil = jnp.ones((DIL_DIM,), F32)
    tail = jnp.ones((GATE_DIM + n_pad,), F32)
    q_scale = ATTN_SCALE * LOG2E
    col_gain = jnp.concatenate([
        jnp.tile(qn_nsa, N_NSA_HEADS) * q_scale, ones_kv, ones_kv, jnp.tile(kn_slc, N_NSA_KV), ones_kv,
        jnp.tile(kn_win, N_NSA_KV), ones_kv, jnp.tile(qn_dil, N_DIL_HEADS) * q_scale,
        jnp.tile(kn_dil, N_DIL_HEADS), ones_dil, tail])[None, :]
    col_flag = jnp.concatenate([
        jnp.ones((Q_NSA_DIM,), F32), 0 * ones_kv, 0 * ones_kv, ones_kv, 0 * ones_kv, ones_kv, 0 * ones_kv,
        ones_dil, ones_dil, 0 * ones_dil, 0 * tail])[None, :]
    proj = _in_proj(x2, attn_norm[None, :], wa, wb, col_gain, col_flag)

    n_chunks = seq // CMP_STRIDE
    kv = proj[:, CB_K_CMP * LANES:(CB_V_CMP + N_NSA_KV) * LANES]
    kv = kv.reshape(batch, n_chunks, CMP_STRIDE, 2, N_NSA_KV, HEAD_DIM).transpose(3, 0, 4, 1, 2, 5)
    chunks = kv.reshape(2, batch * N_NSA_KV, n_chunks, CMP_STRIDE * HEAD_DIM)
    assert CMP_BLOCK == 2 * CMP_STRIDE
    blocks = jnp.concatenate([chunks[:, :, :-1], chunks[:, :, 1:]], axis=-1)
    blocks = jnp.pad(blocks, ((0, 0), (0, 0), (0, 1), (0, 0)))
    pe = jnp.stack([pe_k.reshape(1, -1), pe_v.reshape(1, -1)])
    w1 = jnp.stack([w_k1, w_v1]).astype(BF16)
    w2 = jnp.stack([w_k2, w_v2]).astype(BF16)
    cmp_kv = _compress(blocks, pe, w1, w2, kn_cmp[None, :])

    slopes = _alibi_slopes()
    rel_tab = _rel_position_table(seq)
    gates = proj[:, CB_GATE * LANES:CB_GATE * LANES + GATE_DIM]
    gates_t = gates.reshape(-1, N_NSA_KV, 3 * NSA_REP).transpose(1, 0, 2)
    o_nsa = _nsa_mixer(slopes[0::2], proj, cmp_kv, gates_t, rel_tab, batch, seq)
    o_dil = _dil_mixer(slopes[1::2], proj, rel_tab, batch, seq)
    return _out_proj(o_nsa, o_dil, on_nsa[None, :], on_dil[None, :], w_out.astype(BF16), x2)


def _moe_block(x1, ffn_norm, w_router, b_router, w_gate_up, b_gate_up, w_down, b_down):
    t, d = x1.shape
    wr = jnp.pad(w_router, ((0, 0), (0, LANES - N_EXPERTS)))
    br = jnp.pad(b_router, (0, LANES - N_EXPERTS))[None, :]
    hn, idx, tw = _router(x1, ffn_norm[None, :], wr, br)
    rank, cnt = _rank(idx)
    n_worst = N_EXPERTS + (t * TOP_K) // MOE_TM
    pos, table = _positions(cnt, idx, rank, n_worst)
    pos_flat = pos[:, :TOP_K].reshape(-1)
    tok = _invert(pos[:, TOP_K:2 * TOP_K].reshape(-1))
    bias_gu, bias_d = b_gate_up[:, None, :], b_down[:, None, :]

    def experts(n_visits):
        vt = table[:n_visits, :VT_W].reshape(-1)
        xs = _dispatch(vt, tok, hn, n_visits)
        h = _moe_up(vt, xs, w_gate_up, bias_gu, n_visits)
        ys = _moe_down(vt, h, w_down, bias_d, n_visits)
        return _combine(pos_flat, ys, x1, tw)

    n_even = min(n_worst, N_EXPERTS + MOE_SPARE_VISITS)
    return lax.cond(table[0, VT_TOTAL] <= n_even, lambda: experts(n_even), lambda: experts(n_worst))


def kernel(x, attn_norm, w_in, cmp_pos_k, w_cmp_k1, w_cmp_k2, cmp_pos_v, w_cmp_v1, w_cmp_v2, q_norm_nsa, k_norm_cmp, k_norm_slc, k_norm_win, q_norm_dil, k_norm_dil, out_norm_nsa, out_norm_dil, w_out, ffn_norm, w_router, b_router, w_gate_up, b_gate_up, w_down, b_down):
    batch, seq, d = x.shape
    x2 = x.reshape(batch * seq, d)
    for layer in range(attn_norm.shape[0]):
        x2 = _attention_block(
            x2, batch, seq, attn_norm[layer], w_in[layer], cmp_pos_k[layer], w_cmp_k1[layer],
            w_cmp_k2[layer], cmp_pos_v[layer], w_cmp_v1[layer], w_cmp_v2[layer], q_norm_nsa[layer],
            k_norm_cmp[layer], k_norm_slc[layer], k_norm_win[layer], q_norm_dil[layer],
            k_norm_dil[layer], out_norm_nsa[layer], out_norm_dil[layer], w_out[layer])
        x2 = _moe_block(x2, ffn_norm[layer], w_router[layer], b_router[layer], w_gate_up[layer],
                        b_gate_up[layer], w_down[layer], b_down[layer])
    return x2.reshape(batch, seq, d)
```

```python
import functools

import numpy as np
import jax
import jax.numpy as jnp
from jax import lax
from jax.experimental import pallas as pl
from jax.experimental.pallas import tpu as pltpu

F32, BF16, I32 = jnp.float32, jnp.bfloat16, jnp.int32

HEAD_DIM = 128
N_NSA_HEADS = 16
N_NSA_KV = 4
NSA_REP = N_NSA_HEADS // N_NSA_KV
N_DIL_HEADS = 16
CMP_BLOCK = 32
CMP_STRIDE = 16
CMP_HIDDEN = 256
SLC_BLOCK = 64
SLC_TOP_N = 16
FORCED_SCORE = 1.0e4
WIN_SIZE = 512
DIL_PATTERNS = ((128, 1), (512, 4), (2048, 16))
N_EXPERTS = 32
TOP_K = 4
D_EXPERT = 1536
SWIGLU_ALPHA = 1.702
SWIGLU_LIMIT = 7.0
RMS_EPS = 1e-6
NEG_INF = -1e30
ATTN_SCALE = HEAD_DIM ** -0.5
LOG2E = 1.4426950408889634

Q_NSA_DIM = N_NSA_HEADS * HEAD_DIM
KV_NSA_DIM = N_NSA_KV * HEAD_DIM
GATE_DIM = N_NSA_HEADS * 3
DIL_DIM = N_DIL_HEADS * HEAD_DIM

LANES = 128
SUBLANES = 8
VMEM_LIMIT = 56 * 1024 * 1024

CB_Q_NSA = 0
CB_K_CMP = CB_Q_NSA + N_NSA_HEADS
CB_V_CMP = CB_K_CMP + N_NSA_KV
CB_K_SLC = CB_V_CMP + N_NSA_KV
CB_V_SLC = CB_K_SLC + N_NSA_KV
CB_K_WIN = CB_V_SLC + N_NSA_KV
CB_V_WIN = CB_K_WIN + N_NSA_KV
CB_Q_DIL = CB_V_WIN + N_NSA_KV
CB_K_DIL = CB_Q_DIL + N_DIL_HEADS
CB_V_DIL = CB_K_DIL + N_DIL_HEADS
CB_GATE = CB_V_DIL + N_DIL_HEADS
PROJ_TN = 512
N_PROJ = -(-(CB_GATE + 1) * LANES // PROJ_TN) * PROJ_TN

PROJ_TM = 512
OUT_TM = 1024
NSA_TQ = 256
DIL_TQ = 512
ATT_TK = 256
ROUTER_TM = 256
RANK_TM = 512
POS_TM = 1024
MOE_TM = 1536
MOE_SUB = 256
MOE_HALF = MOE_SUB // 2
MOE_TF = 256
MOE_TN = 1024
INVERT_UNROLL = 16
GATHER_UNROLL = 8
DISPATCH_TM = 512
COMBINE_TM = 128
VT_W = 8
VT_TOTAL = 6
MOE_SPARE_VISITS = 8

POS_SHIFT = 8
POS_RADIX = 1 << POS_SHIFT
ALIBI_PIECES = 3
ALIBI_COLS = 2 * ALIBI_PIECES
SEL_COL0 = 64
MASK_BIG = 2.0 ** 100


def _dot(a, b, **kw):
    return jnp.dot(a, b, preferred_element_type=F32, **kw)


def _dot_nt(a, b, **kw):
    return lax.dot_general(a, b, (((1,), (1,)), ((), ())), preferred_element_type=F32, **kw)


def _rms_rows(x, gain):
    ms = jnp.mean(x * x, axis=-1, keepdims=True)
    return x * lax.rsqrt(ms + RMS_EPS) * gain


def _sigmoid(x):
    return 1.0 / (1.0 + jnp.exp(-x))


def _params(*sem):
    return pltpu.CompilerParams(dimension_semantics=sem, vmem_limit_bytes=VMEM_LIMIT)


def _in_proj_kernel(x_ref, g_ref, wa_ref, wb_ref, cg_ref, cf_ref, o_ref, h_scr, *, na_tiles):
    j = pl.program_id(1)

    @pl.when(j == 0)
    def _():
        h_scr[...] = _rms_rows(x_ref[...], g_ref[...]).astype(BF16)

    def tile(w_ref):
        acc = _dot(h_scr[...], w_ref[...])
        for c in range(PROJ_TN // LANES):
            cs = slice(c * LANES, (c + 1) * LANES)
            a = acc[:, cs]
            r = lax.rsqrt(jnp.mean(a * a, axis=-1, keepdims=True) + RMS_EPS)
            y = a * jnp.where(cf_ref[:, cs] > 0, r, 1.0) * cg_ref[:, cs]
            o_ref[:, cs] = y.astype(o_ref.dtype)

    pl.when(j < na_tiles)(functools.partial(tile, wa_ref))
    pl.when(j >= na_tiles)(functools.partial(tile, wb_ref))


def _in_proj(x2, gain, wa, wb, col_gain, col_flag):
    t, d = x2.shape
    na_tiles = wa.shape[1] // PROJ_TN
    n = wa.shape[1] + wb.shape[1]
    return pl.pallas_call(
        functools.partial(_in_proj_kernel, na_tiles=na_tiles),
        out_shape=jax.ShapeDtypeStruct((t, n), BF16),
        grid=(t // PROJ_TM, n // PROJ_TN),
        in_specs=[
            pl.BlockSpec((PROJ_TM, d), lambda i, j: (i, 0)),
            pl.BlockSpec((1, d), lambda i, j: (0, 0)),
            pl.BlockSpec((d, PROJ_TN), lambda i, j: (0, jnp.minimum(j, na_tiles - 1))),
            pl.BlockSpec((d, PROJ_TN), lambda i, j: (0, jnp.maximum(j - na_tiles, 0))),
            pl.BlockSpec((1, PROJ_TN), lambda i, j: (0, j)),
            pl.BlockSpec((1, PROJ_TN), lambda i, j: (0, j)),
        ],
        out_specs=pl.BlockSpec((PROJ_TM, PROJ_TN), lambda i, j: (i, j)),
        scratch_shapes=[pltpu.VMEM((PROJ_TM, d), BF16)],
        compiler_params=_params("parallel", "arbitrary"),
        name="in_proj",
    )(x2, gain, wa, wb, col_gain, col_flag)


def _compress_kernel(a_ref, pe_ref, w1_ref, w2_ref, kn_ref, o_ref):
    a = (a_ref[...].astype(F32) + pe_ref[...]).astype(BF16)
    hid = _dot(a, w1_ref[...])
    hid = hid * _sigmoid(hid)
    out = _dot(hid.astype(BF16), w2_ref[...])
    normed = _rms_rows(out, kn_ref[...])
    o_ref[...] = jnp.where(pl.program_id(0) == 0, normed, out).astype(o_ref.dtype)


def _compress(blocks, pe, w1, w2, kn):
    two, bg, ncp, flat = blocks.shape
    return pl.pallas_call(
        _compress_kernel,
        out_shape=jax.ShapeDtypeStruct((two, bg, ncp, HEAD_DIM), BF16),
        grid=(two, bg),
        in_specs=[
            pl.BlockSpec((None, None, ncp, flat), lambda s, i: (s, i, 0, 0)),
            pl.BlockSpec((None, 1, flat), lambda s, i: (s, 0, 0)),
            pl.BlockSpec((None, flat, CMP_HIDDEN), lambda s, i: (s, 0, 0)),
            pl.BlockSpec((None, CMP_HIDDEN, HEAD_DIM), lambda s, i: (s, 0, 0)),
            pl.BlockSpec((1, HEAD_DIM), lambda s, i: (0, 0)),
        ],
        out_specs=pl.BlockSpec((None, None, ncp, HEAD_DIM), lambda s, i: (s, i, 0, 0)),
        compiler_params=_params("arbitrary", "arbitrary"),
        name="nsa_compress",
    )(blocks, pe, w1, w2, kn)


def _split_position(v):
    hi = np.floor_divide(v, POS_RADIX)
    return hi, v - POS_RADIX * hi


def _rel_position_table(seq):
    hi, lo = _split_position(np.arange(2 * seq) - seq)
    tab = np.zeros((2 * seq, LANES), np.float32)
    tab[:, 0:ALIBI_COLS:2] = hi[:, None]
    tab[:, 1:ALIBI_COLS:2] = lo[:, None]
    return jnp.asarray(tab, BF16)


def _block_onehot_table(seq):
    tab = np.zeros((seq, LANES), np.float32)
    tab[np.arange(seq), SEL_COL0 + np.arange(seq) // SLC_BLOCK] = 1.0
    return jnp.asarray(tab, BF16)


def _alibi_rows(slopes):
    rest = slopes.astype(F32) * LOG2E
    cols = []
    for _ in range(ALIBI_PIECES):
        piece = rest.astype(BF16).astype(F32)
        rest = rest - piece
        cols += [piece * POS_RADIX, piece]
    rows = jnp.concatenate([jnp.stack(cols, axis=1),
                            jnp.zeros((slopes.shape[0], LANES - ALIBI_COLS), F32)], axis=1)
    return jnp.broadcast_to(rows[:, None, :], (slopes.shape[0], SUBLANES, LANES))


def _tile_distance(u):
    return LANES * u + np.arange(LANES)[:, None] - np.arange(ATT_TK)[None, :]


def _mask_table(n, valid_fn):
    tabs = [np.where(valid_fn(_tile_distance(u)), 0.0, -MASK_BIG) for u in range(n)]
    tabs += [np.zeros((LANES, ATT_TK)), np.full((LANES, ATT_TK), -MASK_BIG)]
    return jnp.asarray(np.stack(tabs), F32)


def _dilation_table():
    reach = max(w for w, _ in DIL_PATTERNS)
    n = (reach + ATT_TK - 1) // LANES + 1
    tabs = []
    for u in range(n):
        d = _tile_distance(u)
        mult = sum(((d >= 0) & (d <= w) & (d % dil == 0)).astype(np.float32) for w, dil in DIL_PATTERNS)
        tabs.append(np.where(mult > 0, np.log2(np.maximum(mult, 1.0)), -MASK_BIG))
    tabs.append(np.full((LANES, ATT_TK), -MASK_BIG))
    return jnp.asarray(np.stack(tabs), F32)


def _softmax_tile(s_buf, p_buf, m_scr, l_scr, acc_scr, bias_for_block):
    for rb in range(s_buf.shape[0] // LANES):
        rs = slice(rb * LANES, (rb + 1) * LANES)
        s = s_buf[rs, :] + bias_for_block(rb)
        m_old = m_scr[rs, :]
        m_new = jnp.maximum(m_old, jnp.max(s, axis=-1, keepdims=True))
        alpha = jnp.exp2(m_old - m_new)
        p = [jnp.exp2(s[:, c * LANES:(c + 1) * LANES] - m_new) for c in range(ATT_TK // LANES)]
        total = functools.reduce(lambda a, b: a + b, p)
        l_scr[rs, :] = alpha * l_scr[rs, :] + jnp.sum(total, axis=-1, keepdims=True)
        acc_scr[rs, :] = alpha * acc_scr[rs, :]
        m_scr[rs, :] = m_new
        for c in range(ATT_TK // LANES):
            p_buf[rs, c * LANES:(c + 1) * LANES] = p[c].astype(BF16)


def _softmax_init(m_scr, l_scr, acc_scr):
    m_scr[...] = jnp.full(m_scr.shape, NEG_INF, F32)
    l_scr[...] = jnp.zeros(l_scr.shape, F32)
    acc_scr[...] = jnp.zeros(acc_scr.shape, F32)


def _attend(first, end, qa_scr, s_bufs, p_scr, m_scr, l_scr, acc_scr, load_k_aug, load_v, bias_for_tile):
    s_a, s_b = s_bufs
    last = end - 1

    def scores(kt, s_buf):
        s_buf[...] = _dot_nt(qa_scr[...], load_k_aug(jnp.minimum(kt, last)))

    def absorb(kt, s_buf):
        _softmax_tile(s_buf, p_scr, m_scr, l_scr, acc_scr, bias_for_tile(kt))
        acc_scr[...] += _dot(p_scr[...], load_v(kt))

    scores(first, s_a)

    def pair(j, carry):
        a = first + 2 * j
        scores(a + 1, s_b)
        absorb(a, s_a)
        scores(a + 2, s_a)
        absorb(a + 1, s_b)
        return carry

    lax.fori_loop(0, (end - first) // 2, pair, 0)

    @pl.when((end - first) % 2 == 1)
    def _():
        absorb(last, s_a)


def _nsa_kernel(q_ref, qx_ref, kc_ref, vc_ref, ks_ref, vs_ref, kw_ref, vw_ref, rel_ref, oh_ref,
                wtab_ref, ctab_ref, gt_ref, o_ref,
                qa_scr, s_a_scr, s_b_scr, p_scr, m_scr, l_scr, acc_scr, ob_scr, sc_scr, *, seq):
    s_bufs = (s_a_scr, s_b_scr)
    t0 = pl.program_id(2) * NSA_TQ
    n_slc = seq // SLC_BLOCK
    heads = [slice(r * NSA_TQ, (r + 1) * NSA_TQ) for r in range(NSA_REP)]
    alibi = [jnp.broadcast_to(qx_ref[r, 0:1, :], (NSA_TQ, LANES)) for r in range(NSA_REP)]
    for r in range(NSA_REP):
        qa_scr[heads[r], :HEAD_DIM] = q_ref[:, r * HEAD_DIM:(r + 1) * HEAD_DIM]
        qa_scr[heads[r], HEAD_DIM:] = alibi[r].astype(BF16)

    ncp = kc_ref.shape[0]
    n_i = lax.broadcasted_iota(I32, (ncp, LANES), 0)
    ln = lax.broadcasted_iota(I32, (ncp, LANES), 1)
    rel_end = n_i * CMP_STRIDE + (CMP_BLOCK - 1) - t0
    digits = jnp.where((ln & 1) == 0, rel_end >> POS_SHIFT, rel_end & (POS_RADIX - 1))
    feat = jnp.where(ln < ALIBI_COLS, digits, 0).astype(F32).astype(BF16)
    kc_aug = jnp.concatenate([kc_ref[...], feat], axis=1)
    row = lax.broadcasted_iota(I32, (NSA_TQ, 1), 0) + t0
    blk_end = lax.broadcasted_iota(I32, (NSA_TQ, ncp), 1) * CMP_STRIDE + (CMP_BLOCK - 1)
    valid_c = row >= blk_end
    vc = vc_ref[...]
    psum = jnp.zeros((NSA_TQ, ncp), F32)
    for r in range(NSA_REP):
        s = jnp.where(valid_c, _dot_nt(qa_scr[heads[r], :], kc_aug), NEG_INF)
        mc = jnp.max(s, axis=-1, keepdims=True)
        pc = jnp.where(valid_c, jnp.exp2(s - mc), 0.0)
        p = pc / jnp.maximum(jnp.sum(pc, axis=-1, keepdims=True), 1e-30)
        psum = psum + p
        ob_scr[0, heads[r], :] = _dot(p.astype(BF16), vc)

    per_slc = SLC_BLOCK // CMP_STRIDE
    back = CMP_BLOCK // CMP_STRIDE - 1
    jb = lax.broadcasted_iota(I32, (LANES, ncp), 0)
    nb = lax.broadcasted_iota(I32, (LANES, ncp), 1)
    overlap_t = jnp.where((nb >= per_slc * jb - back) & (nb < per_slc * (jb + 1)), 1.0, 0.0)
    imp_t = _dot_nt(overlap_t, psum, precision=lax.Precision.HIGHEST)
    j = lax.broadcasted_iota(I32, (LANES, NSA_TQ), 0)
    tq = lax.broadcasted_iota(I32, (LANES, NSA_TQ), 1) + t0
    cur = tq // SLC_BLOCK
    forced = (j == 0) | (j == cur) | (j == cur - 1)
    sc_scr[...] = jnp.where(forced, FORCED_SCORE, jnp.where(j * SLC_BLOCK <= tq, imp_t, -1.0))

    n_slab = n_slc // SUBLANES
    slabs = [sc_scr[SUBLANES * v:SUBLANES * (v + 1), :] for v in range(n_slab)]
    ranks = [jnp.zeros((SUBLANES, NSA_TQ), F32) for _ in range(n_slab)]
    sub = lax.broadcasted_iota(I32, (SUBLANES, NSA_TQ), 0)
    for i in range(n_slc):
        rival = jnp.broadcast_to(sc_scr[i:i + 1, :], (SUBLANES, NSA_TQ))
        for v in range(n_slab):
            wins_ties = jnp.where(rival >= slabs[v], 1.0, 0.0)
            loses_ties = jnp.where(rival > slabs[v], 1.0, 0.0)
            if SUBLANES * v > i:
                ranks[v] = ranks[v] + wins_ties
            elif SUBLANES * (v + 1) <= i:
                ranks[v] = ranks[v] + loses_ties
            else:
                ranks[v] = ranks[v] + jnp.where(sub + SUBLANES * v > i, wins_ties, loses_ties)
    n_top = min(SLC_TOP_N, n_slc)
    pen_rows = [jnp.zeros((SEL_COL0, NSA_TQ), F32)]
    pen_rows += [jnp.where(rk < n_top, 0.0, -MASK_BIG) for rk in ranks]
    if SEL_COL0 + n_slc < LANES:
        pen_rows.append(jnp.zeros((LANES - SEL_COL0 - n_slc, NSA_TQ), F32))
    penalty = jnp.concatenate(pen_rows, axis=0).T
    for r in range(NSA_REP):
        qa_scr[heads[r], HEAD_DIM:] = (alibi[r] + penalty).astype(BF16)

    def tile_rows(kt):
        return pl.ds(pl.multiple_of(kt * ATT_TK, ATT_TK), ATT_TK)

    def k_aug(kt, k_ref, with_blocks):
        extra = rel_ref[pl.ds(pl.multiple_of(kt * ATT_TK - t0 + seq, LANES), ATT_TK), :]
        if with_blocks:
            extra = extra + oh_ref[tile_rows(kt), :]
        return jnp.concatenate([k_ref[tile_rows(kt), :], extra], axis=1)

    q_blocks = NSA_TQ // LANES
    last_kt = (t0 + NSA_TQ - 1) // ATT_TK

    def tile_shift(kt, rb):
        return (t0 + (rb % q_blocks) * LANES - kt * ATT_TK) // LANES

    def masks(tab_ref, index):
        return lambda kt: (lambda rb: tab_ref[index(kt, rb)])

    unmasked = ctab_ref.shape[0] - 2
    _softmax_init(m_scr, l_scr, acc_scr)
    _attend(0, last_kt + 1, qa_scr, s_bufs, p_scr, m_scr, l_scr, acc_scr,
            lambda kt: k_aug(kt, ks_ref, True), lambda kt: vs_ref[tile_rows(kt), :],
            masks(ctab_ref, lambda kt, qb: jnp.where(kt == last_kt, tile_shift(kt, qb), unmasked)))
    ob_scr[1] = acc_scr[...] / l_scr[...]

    _softmax_init(m_scr, l_scr, acc_scr)
    first_kt = jnp.maximum(t0 - (WIN_SIZE - 1), 0) // ATT_TK
    _attend(first_kt, last_kt + 1, qa_scr, s_bufs, p_scr, m_scr, l_scr, acc_scr,
            lambda kt: k_aug(kt, kw_ref, False), lambda kt: vw_ref[tile_rows(kt), :],
            masks(wtab_ref, tile_shift))

    gate = _sigmoid(gt_ref[...].astype(F32))
    for r in range(NSA_REP):
        o_win = acc_scr[heads[r], :] / l_scr[heads[r], :]
        o = (gate[:, 3 * r:3 * r + 1] * ob_scr[0, heads[r], :] + gate[:, 3 * r + 1:3 * r + 2] * ob_scr[1, heads[r], :]
             + gate[:, 3 * r + 2:3 * r + 3] * o_win)
        o_ref[:, r * HEAD_DIM:(r + 1) * HEAD_DIM] = o.astype(o_ref.dtype)


def _nsa_mixer(slopes, proj, cmp_kv, gates_t, rel_tab, batch, seq):
    t = proj.shape[0]
    nq = seq // NSA_TQ
    ncp = cmp_kv.shape[2]
    n_slc = seq // SLC_BLOCK
    assert SEL_COL0 + n_slc <= LANES and n_slc % SUBLANES == 0 and seq % ATT_TK == 0
    rows = NSA_REP * NSA_TQ
    win_shifts = (WIN_SIZE - 1 + ATT_TK - 1 + NSA_TQ - LANES) // LANES + 1
    win_tab = _mask_table(win_shifts, lambda d: (d >= 0) & (d <= WIN_SIZE - 1))
    causal_tab = _mask_table(ATT_TK // LANES, lambda d: d >= 0)

    def seq_spec(cb):
        return pl.BlockSpec((seq, HEAD_DIM), lambda b, g, i: (b, cb + g))

    def cmp_spec(which):
        return pl.BlockSpec((None, None, ncp, HEAD_DIM), lambda b, g, i: (which, b * N_NSA_KV + g, 0, 0))

    def whole(a):
        return pl.BlockSpec(a.shape, lambda b, g, i: (0,) * a.ndim)

    qo_spec = pl.BlockSpec((NSA_TQ, NSA_REP * HEAD_DIM), lambda b, g, i: (b * nq + i, g))
    onehot = _block_onehot_table(seq)
    qx = _alibi_rows(slopes)
    return pl.pallas_call(
        functools.partial(_nsa_kernel, seq=seq),
        out_shape=jax.ShapeDtypeStruct((t, Q_NSA_DIM), BF16),
        grid=(batch, N_NSA_KV, nq),
        in_specs=[
            qo_spec,
            pl.BlockSpec((NSA_REP, SUBLANES, LANES), lambda b, g, i: (g, 0, 0)),
            cmp_spec(0), cmp_spec(1),
            seq_spec(CB_K_SLC), seq_spec(CB_V_SLC), seq_spec(CB_K_WIN), seq_spec(CB_V_WIN),
            whole(rel_tab), whole(onehot), whole(win_tab), whole(causal_tab),
            pl.BlockSpec((None, NSA_TQ, 3 * NSA_REP), lambda b, g, i: (g, b * nq + i, 0)),
        ],
        out_specs=qo_spec,
        scratch_shapes=[
            pltpu.VMEM((rows, HEAD_DIM + LANES), BF16),
            pltpu.VMEM((rows, ATT_TK), F32),
            pltpu.VMEM((rows, ATT_TK), F32),
            pltpu.VMEM((rows, ATT_TK), BF16),
            pltpu.VMEM((rows, LANES), F32),
            pltpu.VMEM((rows, LANES), F32),
            pltpu.VMEM((rows, HEAD_DIM), F32),
            pltpu.VMEM((2, rows, HEAD_DIM), F32),
            pltpu.VMEM((LANES, NSA_TQ), F32),
        ],
        compiler_params=_params("arbitrary", "arbitrary", "arbitrary"),
        name="nsa_mixer",
    )(proj, qx, cmp_kv, cmp_kv, proj, proj, proj, proj, rel_tab, onehot, win_tab, causal_tab, gates_t)


def _dil_kernel(q_ref, qx_ref, k_ref, v_ref, rel_ref, mtab_ref, o_ref,
                qa_scr, s_a_scr, s_b_scr, p_scr, m_scr, l_scr, acc_scr, *, seq):
    t0 = pl.program_id(2) * DIL_TQ
    qa_scr[:, :HEAD_DIM] = q_ref[...]
    qa_scr[:, HEAD_DIM:] = jnp.broadcast_to(qx_ref[0:1, :], (DIL_TQ, LANES)).astype(BF16)
    _softmax_init(m_scr, l_scr, acc_scr)
    masked = mtab_ref.shape[0] - 1

    def tile_rows(kt):
        return pl.ds(pl.multiple_of(kt * ATT_TK, ATT_TK), ATT_TK)

    def k_aug(kt):
        rel = rel_ref[pl.ds(pl.multiple_of(kt * ATT_TK - t0 + seq, LANES), ATT_TK), :]
        return jnp.concatenate([k_ref[tile_rows(kt), :], rel], axis=1)

    def multiplicity(kt):
        def for_block(rb):
            u = (t0 + rb * LANES - kt * ATT_TK) // LANES
            return mtab_ref[jnp.where((u < 0) | (u >= masked), masked, u)]
        return for_block

    reach = max(w for w, _ in DIL_PATTERNS)
    first_kt = jnp.maximum(t0 - reach, 0) // ATT_TK
    _attend(first_kt, (t0 + DIL_TQ - 1) // ATT_TK + 1, qa_scr, (s_a_scr, s_b_scr), p_scr,
            m_scr, l_scr, acc_scr, k_aug, lambda kt: v_ref[tile_rows(kt), :], multiplicity)
    o_ref[...] = (acc_scr[...] / l_scr[...]).astype(o_ref.dtype)


def _dil_mixer(slopes, proj, rel_tab, batch, seq):
    t = proj.shape[0]
    nq = seq // DIL_TQ
    mult_tab = _dilation_table()
    qx = _alibi_rows(slopes)

    def whole(a):
        return pl.BlockSpec(a.shape, lambda b, h, i: (0,) * a.ndim)

    return pl.pallas_call(
        functools.partial(_dil_kernel, seq=seq),
        out_shape=jax.ShapeDtypeStruct((t, DIL_DIM), BF16),
        grid=(batch, N_DIL_HEADS, nq),
        in_specs=[
            pl.BlockSpec((DIL_TQ, HEAD_DIM), lambda b, h, i: (b * nq + i, CB_Q_DIL + h)),
            pl.BlockSpec((None, SUBLANES, LANES), lambda b, h, i: (h, 0, 0)),
            pl.BlockSpec((seq, HEAD_DIM), lambda b, h, i: (b, CB_K_DIL + h)),
            pl.BlockSpec((seq, HEAD_DIM), lambda b, h, i: (b, CB_V_DIL + h)),
            whole(rel_tab), whole(mult_tab),
        ],
        out_specs=pl.BlockSpec((DIL_TQ, HEAD_DIM), lambda b, h, i: (b * nq + i, h)),
        scratch_shapes=[
            pltpu.VMEM((DIL_TQ, HEAD_DIM + LANES), BF16),
            pltpu.VMEM((DIL_TQ, ATT_TK), F32),
            pltpu.VMEM((DIL_TQ, ATT_TK), F32),
            pltpu.VMEM((DIL_TQ, ATT_TK), BF16),
            pltpu.VMEM((DIL_TQ, LANES), F32),
            pltpu.VMEM((DIL_TQ, LANES), F32),
            pltpu.VMEM((DIL_TQ, HEAD_DIM), F32),
        ],
        compiler_params=_params("arbitrary", "arbitrary", "arbitrary"),
        name="dil_mixer",
    )(proj, qx, proj, proj, rel_tab, mult_tab)


def _out_proj_kernel(a_ref, b_ref, ga_ref, gb_ref, w_ref, res_ref, o_ref, h_scr):
    @pl.when(pl.program_id(1) == 0)
    def _():
        na = a_ref.shape[1]
        h_scr[:, :na] = _rms_rows(a_ref[...].astype(F32), ga_ref[...]).astype(BF16)
        h_scr[:, na:] = _rms_rows(b_ref[...].astype(F32), gb_ref[...]).astype(BF16)

    o_ref[...] = res_ref[...] + _dot(h_scr[...], w_ref[...])


def _out_proj(o_nsa, o_dil, g_nsa, g_dil, w, resid):
    t, d = resid.shape
    na, nb = o_nsa.shape[1], o_dil.shape[1]
    return pl.pallas_call(
        _out_proj_kernel,
        out_shape=jax.ShapeDtypeStruct((t, d), F32),
        grid=(t // OUT_TM, d // PROJ_TN),
        in_specs=[
            pl.BlockSpec((OUT_TM, na), lambda i, j: (i, 0)),
            pl.BlockSpec((OUT_TM, nb), lambda i, j: (i, 0)),
            pl.BlockSpec((1, na), lambda i, j: (0, 0)),
            pl.BlockSpec((1, nb), lambda i, j: (0, 0)),
            pl.BlockSpec((na + nb, PROJ_TN), lambda i, j: (0, j)),
            pl.BlockSpec((OUT_TM, PROJ_TN), lambda i, j: (i, j)),
        ],
        out_specs=pl.BlockSpec((OUT_TM, PROJ_TN), lambda i, j: (i, j)),
        scratch_shapes=[pltpu.VMEM((OUT_TM, na + nb), BF16)],
        compiler_params=_params("parallel", "arbitrary"),
        name="out_proj",
    )(o_nsa, o_dil, g_nsa, g_dil, w, resid)


def _router_kernel(x_ref, g_ref, wr_ref, br_ref, hn_ref, idx_ref, tw_ref):
    h = _rms_rows(x_ref[...], g_ref[...])
    hn_ref[...] = h
    logits = _dot(h, wr_ref[...], precision=lax.Precision.HIGHEST) + br_ref[...]
    lane = lax.broadcasted_iota(I32, logits.shape, 1)
    lanef = lane.astype(F32)
    work = jnp.where(lane < N_EXPERTS, logits, -jnp.inf)
    vals, ids = [], []
    for _ in range(TOP_K):
        mx = jnp.max(work, axis=-1, keepdims=True)
        first = jnp.min(jnp.where(work == mx, lanef, float(LANES)), axis=-1, keepdims=True)
        vals.append(mx)
        ids.append(first)
        work = jnp.where(lanef == first, -jnp.inf, work)
    es = [jnp.exp(v - vals[0]) for v in vals]
    den = functools.reduce(lambda a, b: a + b, es)
    idx_out = jnp.zeros(logits.shape, F32)
    tw_out = jnp.zeros(logits.shape, F32)
    for k in range(TOP_K):
        idx_out = jnp.where(lane == k, ids[k], idx_out)
        tw_out = jnp.where(lane == k, es[k] / den, tw_out)
    idx_ref[...] = idx_out.astype(I32)
    tw_ref[...] = tw_out


def _router(x1, gain, wr, br):
    t, d = x1.shape
    return pl.pallas_call(
        _router_kernel,
        out_shape=(jax.ShapeDtypeStruct((t, d), F32),
                   jax.ShapeDtypeStruct((t, LANES), I32),
                   jax.ShapeDtypeStruct((t, LANES), F32)),
        grid=(t // ROUTER_TM,),
        in_specs=[
            pl.BlockSpec((ROUTER_TM, d), lambda i: (i, 0)),
            pl.BlockSpec((1, d), lambda i: (0, 0)),
            pl.BlockSpec((d, LANES), lambda i: (0, 0)),
            pl.BlockSpec((1, LANES), lambda i: (0, 0)),
        ],
        out_specs=(pl.BlockSpec((ROUTER_TM, d), lambda i: (i, 0)),
                   pl.BlockSpec((ROUTER_TM, LANES), lambda i: (i, 0)),
                   pl.BlockSpec((ROUTER_TM, LANES), lambda i: (i, 0))),
        compiler_params=_params("parallel"),
        name="moe_router",
    )(x1, gain, wr, br)


def _rank_kernel(idx_ref, rank_ref, cnt_ref, carry_scr):
    @pl.when(pl.program_id(0) == 0)
    def _():
        carry_scr[...] = jnp.zeros(carry_scr.shape, F32)

    idx = idx_ref[...]
    lane = lax.broadcasted_iota(I32, idx.shape, 1)
    hits = [lane == idx[:, k:k + 1] for k in range(TOP_K)]
    onehot = functools.reduce(lambda a, b: a + b, [jnp.where(h, 1.0, 0.0) for h in hits])
    ri = lax.broadcasted_iota(I32, (RANK_TM, RANK_TM), 0)
    ci = lax.broadcasted_iota(I32, (RANK_TM, RANK_TM), 1)
    before = jnp.where(ci < ri, 1.0, 0.0).astype(BF16)
    rank = _dot(before, onehot.astype(BF16)) + carry_scr[0:1, :]
    out = jnp.zeros(idx.shape, F32)
    for k in range(TOP_K):
        mine = jnp.sum(jnp.where(hits[k], rank, 0.0), axis=-1, keepdims=True)
        out = jnp.where(lane == k, mine, out)
    rank_ref[...] = out.astype(I32)
    carry = carry_scr[...] + jnp.sum(onehot, axis=0, keepdims=True)
    carry_scr[...] = carry
    cnt_ref[...] = carry


def _rank(idx):
    t = idx.shape[0]
    return pl.pallas_call(
        _rank_kernel,
        out_shape=(jax.ShapeDtypeStruct((t, LANES), I32), jax.ShapeDtypeStruct((SUBLANES, LANES), F32)),
        grid=(t // RANK_TM,),
        in_specs=[pl.BlockSpec((RANK_TM, LANES), lambda i: (i, 0))],
        out_specs=(pl.BlockSpec((RANK_TM, LANES), lambda i: (i, 0)),
                   pl.BlockSpec((SUBLANES, LANES), lambda i: (0, 0))),
        scratch_shapes=[pltpu.VMEM((SUBLANES, LANES), F32)],
        compiler_params=_params("arbitrary"),
        name="moe_rank",
    )(idx)


def _pos_kernel(cnt_ref, idx_ref, rank_ref, pos_ref, vt_ref):
    cnt = cnt_ref[0:1, :]
    lane1 = lax.broadcasted_iota(I32, (1, LANES), 1)
    tiles = jnp.where(lane1 < N_EXPERTS, jnp.floor((cnt + (MOE_TM - 0.5)) / MOE_TM), 0.0)
    tiles_before = jnp.zeros((1, LANES), F32)
    rows_before = jnp.zeros((1, LANES), F32)
    for e in range(N_EXPERTS):
        tiles_before = tiles_before + jnp.where(lane1 > e, tiles[:, e:e + 1], 0.0)
        rows_before = rows_before + jnp.where(lane1 > e, cnt[:, e:e + 1], 0.0)
    tiles_through = tiles_before + tiles

    idx = idx_ref[...]
    rank = rank_ref[...].astype(F32)
    lane = lax.broadcasted_iota(I32, idx.shape, 1)
    out = jnp.zeros(idx.shape, F32)
    for k in range(TOP_K):
        mine = lane == idx[:, k:k + 1]
        row0 = jnp.sum(jnp.where(mine, tiles_before * MOE_TM, 0.0), axis=-1, keepdims=True)
        slot0 = jnp.sum(jnp.where(mine, rows_before, 0.0), axis=-1, keepdims=True)
        out = jnp.where(lane == k, row0 + rank[:, k:k + 1], out)
        out = jnp.where(lane == TOP_K + k, slot0 + rank[:, k:k + 1], out)
    pos_ref[...] = out.astype(I32)

    nv = vt_ref.shape[0]
    vl = lax.broadcasted_iota(I32, (nv, LANES), 1)
    v = lax.broadcasted_iota(I32, (nv, LANES), 0).astype(F32)
    total = jnp.sum(tiles, axis=-1, keepdims=True)
    vv = jnp.minimum(v, total - 1.0)
    done = jnp.where((tiles_through <= vv) & (vl < N_EXPERTS), 1.0, 0.0)
    e_v = jnp.minimum(jnp.sum(done, axis=-1, keepdims=True), float(N_EXPERTS - 1))
    mine = vl.astype(F32) == e_v
    first_tile = jnp.sum(jnp.where(mine, tiles_before, 0.0), axis=-1, keepdims=True)
    e_cnt = jnp.sum(jnp.where(mine, cnt, 0.0), axis=-1, keepdims=True)
    e_slot = jnp.sum(jnp.where(mine, rows_before, 0.0), axis=-1, keepdims=True)
    vv1 = vv[:, 0:1]
    valid = v[:, 0:1] < total
    offset = (vv1 - first_tile) * MOE_TM
    held = jnp.where(valid, jnp.clip(e_cnt - offset, 0.0, float(MOE_TM)), 0.0)
    nrows = jnp.ceil(held * (1.0 / MOE_HALF)) * MOE_HALF
    cols = [e_v, vv1, nrows, jnp.where(valid, 1.0, 0.0), e_slot + offset, held, total]
    table = jnp.zeros((nv, LANES), F32)
    for c, val in enumerate(cols):
        table = jnp.where(vl == c, val, table)
    vt_ref[...] = table.astype(I32)


def _positions(cnt, idx, rank, n_visits):
    t = idx.shape[0]
    nvp = -(-n_visits // SUBLANES) * SUBLANES
    return pl.pallas_call(
        _pos_kernel,
        out_shape=(jax.ShapeDtypeStruct((t, LANES), I32), jax.ShapeDtypeStruct((nvp, LANES), I32)),
        grid=(t // POS_TM,),
        in_specs=[pl.BlockSpec((SUBLANES, LANES), lambda i: (0, 0)),
                  pl.BlockSpec((POS_TM, LANES), lambda i: (i, 0)),
                  pl.BlockSpec((POS_TM, LANES), lambda i: (i, 0))],
        out_specs=(pl.BlockSpec((POS_TM, LANES), lambda i: (i, 0)),
                   pl.BlockSpec((nvp, LANES), lambda i: (0, 0))),
        compiler_params=_params("arbitrary"),
        name="moe_positions",
    )(cnt, idx, rank)


def _invert_kernel(slot_ref, tok_ref):
    def body(g, carry):
        base = g * INVERT_UNROLL
        slots = [slot_ref[base + u] for u in range(INVERT_UNROLL)]
        for u in range(INVERT_UNROLL):
            tok_ref[slots[u]] = (base + u) >> top_k_shift
        return carry

    top_k_shift = TOP_K.bit_length() - 1
    assert TOP_K == 1 << top_k_shift and slot_ref.shape[0] % INVERT_UNROLL == 0
    lax.fori_loop(0, slot_ref.shape[0] // INVERT_UNROLL, body, 0)


def _invert(slot_flat):
    return pl.pallas_call(
        _invert_kernel,
        out_shape=jax.ShapeDtypeStruct(slot_flat.shape, I32),
        in_specs=[pl.BlockSpec(memory_space=pltpu.SMEM)],
        out_specs=pl.BlockSpec(memory_space=pltpu.SMEM),
        name="moe_invert",
    )(slot_flat)


def _dispatch_kernel(vt_ref, tok_ref, hn_ref, xs_ref, buf, sems):
    i = pl.program_id(0)
    per_visit = MOE_TM // DISPATCH_TM

    def held_rows(step):
        v = step // per_visit
        part = (step % per_visit) * DISPATCH_TM
        return jnp.clip(vt_ref[v * VT_W + 5] - part, 0, DISPATCH_TM), vt_ref[v * VT_W + 4] + part

    def row_copy(src_row, r, slot):
        return pltpu.make_async_copy(hn_ref.at[pl.ds(src_row, 1)], buf.at[slot, pl.ds(r, 1)], sems.at[slot])

    def start(step, slot):
        n, base = held_rows(step)
        issue = lambda r, c: (row_copy(tok_ref[base + r], r, slot).start(), c)[1]

        @pl.when(n == DISPATCH_TM)
        def _():
            lax.fori_loop(0, DISPATCH_TM, issue, 0, unroll=GATHER_UNROLL)

        @pl.when(n < DISPATCH_TM)
        def _():
            lax.fori_loop(0, n, issue, 0)

    def wait(step, slot):
        n, _ = held_rows(step)

        @pl.when(n == DISPATCH_TM)
        def _():
            pltpu.make_async_copy(hn_ref.at[pl.ds(0, DISPATCH_TM)], buf.at[slot], sems.at[slot]).wait()

        @pl.when(n < DISPATCH_TM)
        def _():
            lax.fori_loop(0, n, lambda r, c: (row_copy(0, r, slot).wait(), c)[1], 0)

    @pl.when(i == 0)
    def _():
        buf[...] = jnp.zeros(buf.shape, buf.dtype)
        start(0, 0)

    @pl.when(i + 1 < pl.num_programs(0))
    def _():
        start(i + 1, (i + 1) % 2)

    wait(i, i % 2)
    n, _ = held_rows(i)
    rowid = lax.broadcasted_iota(I32, (DISPATCH_TM, 1), 0)
    xs_ref[...] = jnp.where(rowid < n, buf[i % 2], 0.0).astype(xs_ref.dtype)


def _dispatch(vt, tok, hn, n_visits):
    d = hn.shape[1]
    return pl.pallas_call(
        _dispatch_kernel,
        out_shape=jax.ShapeDtypeStruct((n_visits * MOE_TM, d), BF16),
        grid_spec=pltpu.PrefetchScalarGridSpec(
            num_scalar_prefetch=2,
            grid=(n_visits * MOE_TM // DISPATCH_TM,),
            in_specs=[pl.BlockSpec(memory_space=pl.ANY)],
            out_specs=pl.BlockSpec((DISPATCH_TM, d), lambda i, vt, tok: (i, 0)),
            scratch_shapes=[pltpu.VMEM((2, DISPATCH_TM, d), F32), pltpu.SemaphoreType.DMA((2,))],
        ),
        compiler_params=_params("arbitrary"),
        name="moe_dispatch",
    )(vt, tok, hn)


def _visit_blocks(nrows, compute, clear):
    for sb in range(MOE_TM // MOE_SUB):
        lo, mid, hi = sb * MOE_SUB, sb * MOE_SUB + MOE_HALF, (sb + 1) * MOE_SUB
        pl.when(nrows >= hi)(functools.partial(compute, slice(lo, hi), sb == 0))

        @pl.when(nrows == mid)
        def _(lo=lo, mid=mid, hi=hi, sb=sb):
            compute(slice(lo, mid), sb == 0)
            clear(slice(mid, hi))

        pl.when(nrows <= lo)(functools.partial(clear, slice(lo, hi)))


def _moe_up_kernel(vt_ref, xs_ref, wg_ref, wu_ref, bg_ref, bu_ref, h_ref, wg_scr, wu_scr):
    def compute(rows, first):
        if first:
            wg_scr[...] = wg_ref[...].astype(BF16)
            wu_scr[...] = wu_ref[...].astype(BF16)
        x = xs_ref[rows, :]
        gate = jnp.minimum(_dot(x, wg_scr[...]) + bg_ref[...], SWIGLU_LIMIT)
        up = jnp.clip(_dot(x, wu_scr[...]) + bu_ref[...], -SWIGLU_LIMIT, SWIGLU_LIMIT)
        act = (up + 1.0) * (gate * _sigmoid(SWIGLU_ALPHA * gate))
        h_ref[rows, :] = act.astype(h_ref.dtype)

    def clear(rows):
        h_ref[rows, :] = jnp.zeros((rows.stop - rows.start, h_ref.shape[1]), h_ref.dtype)

    _visit_blocks(vt_ref[pl.program_id(0) * VT_W + 2], compute, clear)


def _moe_up(vt, xs, w_gate_up, b_gate_up, n_visits):
    d = w_gate_up.shape[1]
    nf = D_EXPERT // MOE_TF

    def fsel(v, f, vt):
        return jnp.where(vt[v * VT_W + 3] > 0, f, nf - 1)

    return pl.pallas_call(
        _moe_up_kernel,
        out_shape=jax.ShapeDtypeStruct((n_visits * MOE_TM, D_EXPERT), BF16),
        grid_spec=pltpu.PrefetchScalarGridSpec(
            num_scalar_prefetch=1,
            grid=(n_visits, nf),
            in_specs=[
                pl.BlockSpec((MOE_TM, d), lambda v, f, vt: (vt[v * VT_W + 1], 0)),
                pl.BlockSpec((None, d, MOE_TF), lambda v, f, vt: (vt[v * VT_W], 0, fsel(v, f, vt))),
                pl.BlockSpec((None, d, MOE_TF), lambda v, f, vt: (vt[v * VT_W], 0, nf + fsel(v, f, vt))),
                pl.BlockSpec((None, 1, MOE_TF), lambda v, f, vt: (vt[v * VT_W], 0, fsel(v, f, vt))),
                pl.BlockSpec((None, 1, MOE_TF), lambda v, f, vt: (vt[v * VT_W], 0, nf + fsel(v, f, vt))),
            ],
            out_specs=pl.BlockSpec((MOE_TM, MOE_TF), lambda v, f, vt: (v, f)),
            scratch_shapes=[pltpu.VMEM((d, MOE_TF), BF16), pltpu.VMEM((d, MOE_TF), BF16)],
        ),
        compiler_params=_params("arbitrary", "arbitrary"),
        name="moe_up",
    )(vt, xs, w_gate_up, w_gate_up, b_gate_up, b_gate_up)


def _moe_down_kernel(vt_ref, h_ref, wd_ref, bd_ref, y_ref, wd_scr):
    def compute(rows, first):
        if first:
            wd_scr[...] = wd_ref[...].astype(BF16)
        y_ref[rows, :] = _dot(h_ref[rows, :], wd_scr[...]) + bd_ref[...]

    def clear(rows):
        y_ref[rows, :] = jnp.zeros((rows.stop - rows.start, y_ref.shape[1]), y_ref.dtype)

    _visit_blocks(vt_ref[pl.program_id(0) * VT_W + 2], compute, clear)


def _moe_down(vt, h, w_down, b_down, n_visits):
    d = w_down.shape[2]
    nj = d // MOE_TN

    def jsel(v, j, vt):
        return jnp.where(vt[v * VT_W + 3] > 0, j, nj - 1)

    return pl.pallas_call(
        _moe_down_kernel,
        out_shape=jax.ShapeDtypeStruct((n_visits * MOE_TM, d), F32),
        grid_spec=pltpu.PrefetchScalarGridSpec(
            num_scalar_prefetch=1,
            grid=(n_visits, nj),
            in_specs=[
                pl.BlockSpec((MOE_TM, D_EXPERT), lambda v, j, vt: (vt[v * VT_W + 1], 0)),
                pl.BlockSpec((None, D_EXPERT, MOE_TN), lambda v, j, vt: (vt[v * VT_W], 0, jsel(v, j, vt))),
                pl.BlockSpec((None, 1, MOE_TN), lambda v, j, vt: (vt[v * VT_W], 0, jsel(v, j, vt))),
            ],
            out_specs=pl.BlockSpec((MOE_TM, MOE_TN), lambda v, j, vt: (v, j)),
            scratch_shapes=[pltpu.VMEM((D_EXPERT, MOE_TN), BF16)],
        ),
        compiler_params=_params("arbitrary", "arbitrary"),
        name="moe_down",
    )(vt, h, w_down, b_down)


def _combine_kernel(pos_ref, ys_ref, x_ref, tw_ref, o_ref, buf, sems):
    i = pl.program_id(0)

    def start(step, slot):
        def issue(r, carry):
            for k in range(TOP_K):
                src = pos_ref[(step * COMBINE_TM + r) * TOP_K + k]
                pltpu.make_async_copy(ys_ref.at[pl.ds(src, 1)], buf.at[slot, k, pl.ds(r, 1)], sems.at[slot]).start()
            return carry
        lax.fori_loop(0, COMBINE_TM, issue, 0, unroll=GATHER_UNROLL)

    def wait(slot):
        for k in range(TOP_K):
            pltpu.make_async_copy(ys_ref.at[pl.ds(0, COMBINE_TM)], buf.at[slot, k], sems.at[slot]).wait()

    @pl.when(i == 0)
    def _():
        start(0, 0)

    @pl.when(i + 1 < pl.num_programs(0))
    def _():
        start(i + 1, (i + 1) % 2)

    wait(i % 2)
    tw = tw_ref[...]
    acc = x_ref[...]
    for k in range(TOP_K):
        acc = acc + tw[:, k:k + 1] * buf[i % 2, k]
    o_ref[...] = acc


def _combine(pos_flat, ys, x1, tw):
    t, d = x1.shape
    return pl.pallas_call(
        _combine_kernel,
        out_shape=jax.ShapeDtypeStruct((t, d), F32),
        grid_spec=pltpu.PrefetchScalarGridSpec(
            num_scalar_prefetch=1,
            grid=(t // COMBINE_TM,),
            in_specs=[pl.BlockSpec(memory_space=pl.ANY),
                      pl.BlockSpec((COMBINE_TM, d), lambda i, pos: (i, 0)),
                      pl.BlockSpec((COMBINE_TM, LANES), lambda i, pos: (i, 0))],
            out_specs=pl.BlockSpec((COMBINE_TM, d), lambda i, pos: (i, 0)),
            scratch_shapes=[pltpu.VMEM((2, TOP_K, COMBINE_TM, d), F32), pltpu.SemaphoreType.DMA((2,))],
        ),
        compiler_params=_params("arbitrary"),
        name="moe_combine",
    )(pos_flat, ys, x1, tw)


def _alibi_slopes():
    n = N_NSA_HEADS + N_DIL_HEADS
    i = jnp.arange(1, n + 1, dtype=F32)
    return jnp.exp2(-8.0 * i / n)


def _attention_block(x2, batch, seq, attn_norm, w_in, pe_k, w_k1, w_k2, pe_v, w_v1, w_v2,
                     qn_nsa, kn_cmp, kn_slc, kn_win, qn_dil, kn_dil, on_nsa, on_dil, w_out):
    d = x2.shape[1]
    a_dim = Q_NSA_DIM + 6 * KV_NSA_DIM
    d_off = a_dim + GATE_DIM
    n_pad = N_PROJ - (a_dim + 3 * DIL_DIM + GATE_DIM)
    assert a_dim % PROJ_TN == 0
    wa = w_in[:, :a_dim].astype(BF16)
    wb = jnp.concatenate([w_in[:, d_off:].astype(BF16), w_in[:, a_dim:d_off].astype(BF16),
                          jnp.zeros((d, n_pad), BF16)], axis=1)
    ones_kv = jnp.ones((KV_NSA_DIM,), F32)
    ones_dil = jnp.ones((DIL_DIM,), F32)
    tail = jnp.ones((GATE_DIM + n_pad,), F32)
    q_scale = ATTN_SCALE * LOG2E
    col_gain = jnp.concatenate([
        jnp.tile(qn_nsa, N_NSA_HEADS) * q_scale, ones_kv, ones_kv, jnp.tile(kn_slc, N_NSA_KV), ones_kv,
        jnp.tile(kn_win, N_NSA_KV), ones_kv, jnp.tile(qn_dil, N_DIL_HEADS) * q_scale,
        jnp.tile(kn_dil, N_DIL_HEADS), ones_dil, tail])[None, :]
    col_flag = jnp.concatenate([
        jnp.ones((Q_NSA_DIM,), F32), 0 * ones_kv, 0 * ones_kv, ones_kv, 0 * ones_kv, ones_kv, 0 * ones_kv,
        ones_dil, ones_dil, 0 * ones_dil, 0 * tail])[None, :]
    proj = _in_proj(x2, attn_norm[None, :], wa, wb, col_gain, col_flag)

    n_chunks = seq // CMP_STRIDE
    kv = proj[:, CB_K_CMP * LANES:(CB_V_CMP + N_NSA_KV) * LANES]
    kv = kv.reshape(batch, n_chunks, CMP_STRIDE, 2, N_NSA_KV, HEAD_DIM).transpose(3, 0, 4, 1, 2, 5)
    chunks = kv.reshape(2, batch * N_NSA_KV, n_chunks, CMP_STRIDE * HEAD_DIM)
    assert CMP_BLOCK == 2 * CMP_STRIDE
    blocks = jnp.concatenate([chunks[:, :, :-1], chunks[:, :, 1:]], axis=-1)
    blocks = jnp.pad(blocks, ((0, 0), (0, 0), (0, 1), (0, 0)))
    pe = jnp.stack([pe_k.reshape(1, -1), pe_v.reshape(1, -1)])
    w1 = jnp.stack([w_k1, w_v1]).astype(BF16)
    w2 = jnp.stack([w_k2, w_v2]).astype(BF16)
    cmp_kv = _compress(blocks, pe, w1, w2, kn_cmp[None, :])

    slopes = _alibi_slopes()
    rel_tab = _rel_position_table(seq)
    gates = proj[:, CB_GATE * LANES:CB_GATE * LANES + GATE_DIM]
    gates_t = gates.reshape(-1, N_NSA_KV, 3 * NSA_REP).transpose(1, 0, 2)
    o_nsa = _nsa_mixer(slopes[0::2], proj, cmp_kv, gates_t, rel_tab, batch, seq)
    o_dil = _dil_mixer(slopes[1::2], proj, rel_tab, batch, seq)
    return _out_proj(o_nsa, o_dil, on_nsa[None, :], on_dil[None, :], w_out.astype(BF16), x2)


def _moe_block(x1, ffn_norm, w_router, b_router, w_gate_up, b_gate_up, w_down, b_down):
    t, d = x1.shape
    wr = jnp.pad(w_router, ((0, 0), (0, LANES - N_EXPERTS)))
    br = jnp.pad(b_router, (0, LANES - N_EXPERTS))[None, :]
    hn, idx, tw = _router(x1, ffn_norm[None, :], wr, br)
    rank, cnt = _rank(idx)
    n_worst = N_EXPERTS + (t * TOP_K) // MOE_TM
    pos, table = _positions(cnt, idx, rank, n_worst)
    pos_flat = pos[:, :TOP_K].reshape(-1)
    tok = _invert(pos[:, TOP_K:2 * TOP_K].reshape(-1))
    bias_gu, bias_d = b_gate_up[:, None, :], b_down[:, None, :]

    def experts(n_visits):
        vt = table[:n_visits, :VT_W].reshape(-1)
        xs = _dispatch(vt, tok, hn, n_visits)
        h = _moe_up(vt, xs, w_gate_up, bias_gu, n_visits)
        ys = _moe_down(vt, h, w_down, bias_d, n_visits)
        return _combine(pos_flat, ys, x1, tw)

    n_even = min(n_worst, N_EXPERTS + MOE_SPARE_VISITS)
    return lax.cond(table[0, VT_TOTAL] <= n_even, lambda: experts(n_even), lambda: experts(n_worst))


def kernel(x, attn_norm, w_in, cmp_pos_k, w_cmp_k1, w_cmp_k2, cmp_pos_v, w_cmp_v1, w_cmp_v2, q_norm_nsa, k_norm_cmp, k_norm_slc, k_norm_win, q_norm_dil, k_norm_dil, out_norm_nsa, out_norm_dil, w_out, ffn_norm, w_router, b_router, w_gate_up, b_gate_up, w_down, b_down):
    batch, seq, d = x.shape
    x2 = x.reshape(batch * seq, d)
    for layer in range(attn_norm.shape[0]):
        x2 = _attention_block(
            x2, batch, seq, attn_norm[layer], w_in[layer], cmp_pos_k[layer], w_cmp_k1[layer],
            w_cmp_k2[layer], cmp_pos_v[layer], w_cmp_v1[layer], w_cmp_v2[layer], q_norm_nsa[layer],
            k_norm_cmp[layer], k_norm_slc[layer], k_norm_win[layer], q_norm_dil[layer],
            k_norm_dil[layer], out_norm_nsa[layer], out_norm_dil[layer], w_out[layer])
        x2 = _moe_block(x2, ffn_norm[layer], w_router[layer], b_router[layer], w_gate_up[layer],
                        b_gate_up[layer], w_down[layer], b_down[layer])
    return x2.reshape(batch, seq, d)
```

```python
import functools

import numpy as np
import jax
import jax.numpy as jnp
from jax import lax
from jax.experimental import pallas as pl
from jax.experimental.pallas import tpu as pltpu

F32, BF16, I32 = jnp.float32, jnp.bfloat16, jnp.int32

HEAD_DIM = 128
N_NSA_HEADS = 16
N_NSA_KV = 4
NSA_REP = N_NSA_HEADS // N_NSA_KV
N_DIL_HEADS = 16
CMP_BLOCK = 32
CMP_STRIDE = 16
CMP_HIDDEN = 256
SLC_BLOCK = 64
SLC_TOP_N = 16
FORCED_SCORE = 1.0e4
WIN_SIZE = 512
DIL_PATTERNS = ((128, 1), (512, 4), (2048, 16))
N_EXPERTS = 32
TOP_K = 4
D_EXPERT = 1536
SWIGLU_ALPHA = 1.702
SWIGLU_LIMIT = 7.0
RMS_EPS = 1e-6
NEG_INF = -1e30
ATTN_SCALE = HEAD_DIM ** -0.5
LOG2E = 1.4426950408889634

Q_NSA_DIM = N_NSA_HEADS * HEAD_DIM
KV_NSA_DIM = N_NSA_KV * HEAD_DIM
GATE_DIM = N_NSA_HEADS * 3
DIL_DIM = N_DIL_HEADS * HEAD_DIM

LANES = 128
SUBLANES = 8
VMEM_LIMIT = 56 * 1024 * 1024

CB_Q_NSA = 0
CB_K_CMP = CB_Q_NSA + N_NSA_HEADS
CB_V_CMP = CB_K_CMP + N_NSA_KV
CB_K_SLC = CB_V_CMP + N_NSA_KV
CB_V_SLC = CB_K_SLC + N_NSA_KV
CB_K_WIN = CB_V_SLC + N_NSA_KV
CB_V_WIN = CB_K_WIN + N_NSA_KV
CB_Q_DIL = CB_V_WIN + N_NSA_KV
CB_K_DIL = CB_Q_DIL + N_DIL_HEADS
CB_V_DIL = CB_K_DIL + N_DIL_HEADS
CB_GATE = CB_V_DIL + N_DIL_HEADS
PROJ_TN = 512
N_PROJ = -(-(CB_GATE + 1) * LANES // PROJ_TN) * PROJ_TN

PROJ_TM = 1024
OUT_TM = 1024
NSA_TQ = 256
DIL_TQ = 512
ATT_TK = 256
ROUTER_TM = 256
RANK_TM = 512
POS_TM = 1024
MOE_TM = 1536
MOE_SUB = 256
MOE_HALF = MOE_SUB // 2
MOE_TF = 256
MOE_TN = 1024
INVERT_UNROLL = 16
GATHER_UNROLL = 8
DISPATCH_TM = 512
COMBINE_TM = 128
VT_W = 8
VT_TOTAL = 6
MOE_SPARE_VISITS = 8

POS_SHIFT = 8
POS_RADIX = 1 << POS_SHIFT
ALIBI_PIECES = 3
ALIBI_COLS = 2 * ALIBI_PIECES
SEL_COL0 = 64
MASK_BIG = 2.0 ** 100


def _dot(a, b, **kw):
    return jnp.dot(a, b, preferred_element_type=F32, **kw)


def _dot_nt(a, b, **kw):
    return lax.dot_general(a, b, (((1,), (1,)), ((), ())), preferred_element_type=F32, **kw)


def _rms_rows(x, gain):
    ms = jnp.mean(x * x, axis=-1, keepdims=True)
    return x * lax.rsqrt(ms + RMS_EPS) * gain


def _sigmoid(x):
    return 1.0 / (1.0 + jnp.exp(-x))


def _params(*sem):
    return pltpu.CompilerParams(dimension_semantics=sem, vmem_limit_bytes=VMEM_LIMIT)


def _in_proj_kernel(x_ref, g_ref, wa_ref, wb_ref, cg_ref, cf_ref, o_ref, h_scr, *, na_tiles):
    j = pl.program_id(1)

    @pl.when(j == 0)
    def _():
        for c in range(x_ref.shape[0] // LANES):
            rs = slice(c * LANES, (c + 1) * LANES)
            h_scr[rs, :] = _rms_rows(x_ref[rs, :], g_ref[...]).astype(BF16)

    def tile(w_ref):
        acc = _dot(h_scr[...], w_ref[...])
        for c in range(PROJ_TN // LANES):
            cs = slice(c * LANES, (c + 1) * LANES)
            a = acc[:, cs]
            r = lax.rsqrt(jnp.mean(a * a, axis=-1, keepdims=True) + RMS_EPS)
            y = a * jnp.where(cf_ref[:, cs] > 0, r, 1.0) * cg_ref[:, cs]
            o_ref[:, cs] = y.astype(o_ref.dtype)

    pl.when(j < na_tiles)(functools.partial(tile, wa_ref))
    pl.when(j >= na_tiles)(functools.partial(tile, wb_ref))


def _in_proj(x2, gain, wa, wb, col_gain, col_flag):
    t, d = x2.shape
    na_tiles = wa.shape[1] // PROJ_TN
    n = wa.shape[1] + wb.shape[1]
    return pl.pallas_call(
        functools.partial(_in_proj_kernel, na_tiles=na_tiles),
        out_shape=jax.ShapeDtypeStruct((t, n), BF16),
        grid=(t // PROJ_TM, n // PROJ_TN),
        in_specs=[
            pl.BlockSpec((PROJ_TM, d), lambda i, j: (i, 0), pipeline_mode=pl.Buffered(1)),
            pl.BlockSpec((1, d), lambda i, j: (0, 0)),
            pl.BlockSpec((d, PROJ_TN), lambda i, j: (0, jnp.minimum(j, na_tiles - 1))),
            pl.BlockSpec((d, PROJ_TN), lambda i, j: (0, jnp.maximum(j - na_tiles, 0))),
            pl.BlockSpec((1, PROJ_TN), lambda i, j: (0, j)),
            pl.BlockSpec((1, PROJ_TN), lambda i, j: (0, j)),
        ],
        out_specs=pl.BlockSpec((PROJ_TM, PROJ_TN), lambda i, j: (i, j)),
        scratch_shapes=[pltpu.VMEM((PROJ_TM, d), BF16)],
        compiler_params=_params("parallel", "arbitrary"),
        name="in_proj",
    )(x2, gain, wa, wb, col_gain, col_flag)


def _compress_kernel(a_ref, pe_ref, w1_ref, w2_ref, kn_ref, o_ref):
    a = (a_ref[...].astype(F32) + pe_ref[...]).astype(BF16)
    hid = _dot(a, w1_ref[...])
    hid = hid * _sigmoid(hid)
    out = _dot(hid.astype(BF16), w2_ref[...])
    normed = _rms_rows(out, kn_ref[...])
    o_ref[...] = jnp.where(pl.program_id(0) == 0, normed, out).astype(o_ref.dtype)


def _compress(blocks, pe, w1, w2, kn):
    two, bg, ncp, flat = blocks.shape
    return pl.pallas_call(
        _compress_kernel,
        out_shape=jax.ShapeDtypeStruct((two, bg, ncp, HEAD_DIM), BF16),
        grid=(two, bg),
        in_specs=[
            pl.BlockSpec((None, None, ncp, flat), lambda s, i: (s, i, 0, 0)),
            pl.BlockSpec((None, 1, flat), lambda s, i: (s, 0, 0)),
            pl.BlockSpec((None, flat, CMP_HIDDEN), lambda s, i: (s, 0, 0)),
            pl.BlockSpec((None, CMP_HIDDEN, HEAD_DIM), lambda s, i: (s, 0, 0)),
            pl.BlockSpec((1, HEAD_DIM), lambda s, i: (0, 0)),
        ],
        out_specs=pl.BlockSpec((None, None, ncp, HEAD_DIM), lambda s, i: (s, i, 0, 0)),
        compiler_params=_params("arbitrary", "arbitrary"),
        name="nsa_compress",
    )(blocks, pe, w1, w2, kn)


def _split_position(v):
    hi = np.floor_divide(v, POS_RADIX)
    return hi, v - POS_RADIX * hi


def _rel_position_table(seq):
    hi, lo = _split_position(np.arange(2 * seq) - seq)
    tab = np.zeros((2 * seq, LANES), np.float32)
    tab[:, 0:ALIBI_COLS:2] = hi[:, None]
    tab[:, 1:ALIBI_COLS:2] = lo[:, None]
    return jnp.asarray(tab, BF16)


def _block_onehot_table(seq):
    tab = np.zeros((seq, LANES), np.float32)
    tab[np.arange(seq), SEL_COL0 + np.arange(seq) // SLC_BLOCK] = 1.0
    return jnp.asarray(tab, BF16)


def _alibi_rows(slopes):
    rest = slopes.astype(F32) * LOG2E
    cols = []
    for _ in range(ALIBI_PIECES):
        piece = rest.astype(BF16).astype(F32)
        rest = rest - piece
        cols += [piece * POS_RADIX, piece]
    rows = jnp.concatenate([jnp.stack(cols, axis=1),
                            jnp.zeros((slopes.shape[0], LANES - ALIBI_COLS), F32)], axis=1)
    return jnp.broadcast_to(rows[:, None, :], (slopes.shape[0], SUBLANES, LANES))


def _tile_distance(u):
    return LANES * u + np.arange(LANES)[:, None] - np.arange(ATT_TK)[None, :]


def _mask_table(n, valid_fn):
    tabs = [np.where(valid_fn(_tile_distance(u)), 0.0, -MASK_BIG) for u in range(n)]
    tabs += [np.zeros((LANES, ATT_TK)), np.full((LANES, ATT_TK), -MASK_BIG)]
    return jnp.asarray(np.stack(tabs), F32)


def _dilation_table():
    reach = max(w for w, _ in DIL_PATTERNS)
    n = (reach + ATT_TK - 1) // LANES + 1
    tabs = []
    for u in range(n):
        d = _tile_distance(u)
        mult = sum(((d >= 0) & (d <= w) & (d % dil == 0)).astype(np.float32) for w, dil in DIL_PATTERNS)
        tabs.append(np.where(mult > 0, np.log2(np.maximum(mult, 1.0)), -MASK_BIG))
    tabs.append(np.full((LANES, ATT_TK), -MASK_BIG))
    return jnp.asarray(np.stack(tabs), F32)


def _softmax_tile(s_buf, p_buf, m_scr, l_scr, acc_scr, bias_for_block):
    for rb in range(s_buf.shape[0] // LANES):
        rs = slice(rb * LANES, (rb + 1) * LANES)
        s = s_buf[rs, :] + bias_for_block(rb)
        m_old = m_scr[rs, :]
        m_new = jnp.maximum(m_old, jnp.max(s, axis=-1, keepdims=True))
        alpha = jnp.exp2(m_old - m_new)
        p = [jnp.exp2(s[:, c * LANES:(c + 1) * LANES] - m_new) for c in range(ATT_TK // LANES)]
        total = functools.reduce(lambda a, b: a + b, p)
        l_scr[rs, :] = alpha * l_scr[rs, :] + jnp.sum(total, axis=-1, keepdims=True)
        acc_scr[rs, :] = alpha * acc_scr[rs, :]
        m_scr[rs, :] = m_new
        for c in range(ATT_TK // LANES):
            p_buf[rs, c * LANES:(c + 1) * LANES] = p[c].astype(BF16)


def _softmax_init(m_scr, l_scr, acc_scr):
    m_scr[...] = jnp.full(m_scr.shape, NEG_INF, F32)
    l_scr[...] = jnp.zeros(l_scr.shape, F32)
    acc_scr[...] = jnp.zeros(acc_scr.shape, F32)


def _attend(first, end, qa_scr, s_bufs, p_scr, m_scr, l_scr, acc_scr, load_k_aug, load_v, bias_for_tile):
    s_a, s_b = s_bufs
    last = end - 1

    def scores(kt, s_buf):
        s_buf[...] = _dot_nt(qa_scr[...], load_k_aug(jnp.minimum(kt, last)))

    def absorb(kt, s_buf):
        _softmax_tile(s_buf, p_scr, m_scr, l_scr, acc_scr, bias_for_tile(kt))
        acc_scr[...] += _dot(p_scr[...], load_v(kt))

    scores(first, s_a)

    def pair(j, carry):
        a = first + 2 * j
        scores(a + 1, s_b)
        absorb(a, s_a)
        scores(a + 2, s_a)
        absorb(a + 1, s_b)
        return carry

    lax.fori_loop(0, (end - first) // 2, pair, 0)

    @pl.when((end - first) % 2 == 1)
    def _():
        absorb(last, s_a)


def _nsa_kernel(q_ref, qx_ref, kc_ref, vc_ref, ks_ref, vs_ref, kw_ref, vw_ref, rel_ref, oh_ref,
                wtab_ref, ctab_ref, gt_ref, o_ref,
                qa_scr, s_a_scr, s_b_scr, p_scr, m_scr, l_scr, acc_scr, ob_scr, sc_scr, *, seq):
    s_bufs = (s_a_scr, s_b_scr)
    t0 = pl.program_id(2) * NSA_TQ
    n_slc = seq // SLC_BLOCK
    heads = [slice(r * NSA_TQ, (r + 1) * NSA_TQ) for r in range(NSA_REP)]
    alibi = [jnp.broadcast_to(qx_ref[r, 0:1, :], (NSA_TQ, LANES)) for r in range(NSA_REP)]
    for r in range(NSA_REP):
        qa_scr[heads[r], :HEAD_DIM] = q_ref[:, r * HEAD_DIM:(r + 1) * HEAD_DIM]
        qa_scr[heads[r], HEAD_DIM:] = alibi[r].astype(BF16)

    ncp = kc_ref.shape[0]
    n_i = lax.broadcasted_iota(I32, (ncp, LANES), 0)
    ln = lax.broadcasted_iota(I32, (ncp, LANES), 1)
    rel_end = n_i * CMP_STRIDE + (CMP_BLOCK - 1) - t0
    digits = jnp.where((ln & 1) == 0, rel_end >> POS_SHIFT, rel_end & (POS_RADIX - 1))
    feat = jnp.where(ln < ALIBI_COLS, digits, 0).astype(F32).astype(BF16)
    kc_aug = jnp.concatenate([kc_ref[...], feat], axis=1)
    row = lax.broadcasted_iota(I32, (NSA_TQ, 1), 0) + t0
    blk_end = lax.broadcasted_iota(I32, (NSA_TQ, ncp), 1) * CMP_STRIDE + (CMP_BLOCK - 1)
    valid_c = row >= blk_end
    vc = vc_ref[...]
    psum = jnp.zeros((NSA_TQ, ncp), F32)
    for r in range(NSA_REP):
        s = jnp.where(valid_c, _dot_nt(qa_scr[heads[r], :], kc_aug), NEG_INF)
        mc = jnp.max(s, axis=-1, keepdims=True)
        pc = jnp.where(valid_c, jnp.exp2(s - mc), 0.0)
        p = pc / jnp.maximum(jnp.sum(pc, axis=-1, keepdims=True), 1e-30)
        psum = psum + p
        ob_scr[0, heads[r], :] = _dot(p.astype(BF16), vc)

    per_slc = SLC_BLOCK // CMP_STRIDE
    back = CMP_BLOCK // CMP_STRIDE - 1
    jb = lax.broadcasted_iota(I32, (LANES, ncp), 0)
    nb = lax.broadcasted_iota(I32, (LANES, ncp), 1)
    overlap_t = jnp.where((nb >= per_slc * jb - back) & (nb < per_slc * (jb + 1)), 1.0, 0.0)
    imp_t = _dot_nt(overlap_t, psum, precision=lax.Precision.HIGHEST)
    j = lax.broadcasted_iota(I32, (LANES, NSA_TQ), 0)
    tq = lax.broadcasted_iota(I32, (LANES, NSA_TQ), 1) + t0
    cur = tq // SLC_BLOCK
    forced = (j == 0) | (j == cur) | (j == cur - 1)
    sc_scr[...] = jnp.where(forced, FORCED_SCORE, jnp.where(j * SLC_BLOCK <= tq, imp_t, -1.0))

    n_slab = n_slc // SUBLANES
    slabs = [sc_scr[SUBLANES * v:SUBLANES * (v + 1), :] for v in range(n_slab)]
    ranks = [jnp.zeros((SUBLANES, NSA_TQ), F32) for _ in range(n_slab)]
    sub = lax.broadcasted_iota(I32, (SUBLANES, NSA_TQ), 0)
    for i in range(n_slc):
        rival = jnp.broadcast_to(sc_scr[i:i + 1, :], (SUBLANES, NSA_TQ))
        for v in range(n_slab):
            wins_ties = jnp.where(rival >= slabs[v], 1.0, 0.0)
            loses_ties = jnp.where(rival > slabs[v], 1.0, 0.0)
            if SUBLANES * v > i:
                ranks[v] = ranks[v] + wins_ties
            elif SUBLANES * (v + 1) <= i:
                ranks[v] = ranks[v] + loses_ties
            else:
                ranks[v] = ranks[v] + jnp.where(sub + SUBLANES * v > i, wins_ties, loses_ties)
    n_top = min(SLC_TOP_N, n_slc)
    pen_rows = [jnp.zeros((SEL_COL0, NSA_TQ), F32)]
    pen_rows += [jnp.where(rk < n_top, 0.0, -MASK_BIG) for rk in ranks]
    if SEL_COL0 + n_slc < LANES:
        pen_rows.append(jnp.zeros((LANES - SEL_COL0 - n_slc, NSA_TQ), F32))
    penalty = jnp.concatenate(pen_rows, axis=0).T
    for r in range(NSA_REP):
        qa_scr[heads[r], HEAD_DIM:] = (alibi[r] + penalty).astype(BF16)

    def tile_rows(kt):
        return pl.ds(pl.multiple_of(kt * ATT_TK, ATT_TK), ATT_TK)

    def k_aug(kt, k_ref, with_blocks):
        extra = rel_ref[pl.ds(pl.multiple_of(kt * ATT_TK - t0 + seq, LANES), ATT_TK), :]
        if with_blocks:
            extra = extra + oh_ref[tile_rows(kt), :]
        return jnp.concatenate([k_ref[tile_rows(kt), :], extra], axis=1)

    q_blocks = NSA_TQ // LANES
    last_kt = (t0 + NSA_TQ - 1) // ATT_TK

    def tile_shift(kt, rb):
        return (t0 + (rb % q_blocks) * LANES - kt * ATT_TK) // LANES

    def masks(tab_ref, index):
        return lambda kt: (lambda rb: tab_ref[index(kt, rb)])

    unmasked = ctab_ref.shape[0] - 2
    _softmax_init(m_scr, l_scr, acc_scr)
    _attend(0, last_kt + 1, qa_scr, s_bufs, p_scr, m_scr, l_scr, acc_scr,
            lambda kt: k_aug(kt, ks_ref, True), lambda kt: vs_ref[tile_rows(kt), :],
            masks(ctab_ref, lambda kt, qb: jnp.where(kt == last_kt, tile_shift(kt, qb), unmasked)))
    ob_scr[1] = acc_scr[...] / l_scr[...]

    _softmax_init(m_scr, l_scr, acc_scr)
    first_kt = jnp.maximum(t0 - (WIN_SIZE - 1), 0) // ATT_TK
    _attend(first_kt, last_kt + 1, qa_scr, s_bufs, p_scr, m_scr, l_scr, acc_scr,
            lambda kt: k_aug(kt, kw_ref, False), lambda kt: vw_ref[tile_rows(kt), :],
            masks(wtab_ref, tile_shift))

    gate = _sigmoid(gt_ref[...].astype(F32))
    for r in range(NSA_REP):
        o_win = acc_scr[heads[r], :] / l_scr[heads[r], :]
        o = (gate[:, 3 * r:3 * r + 1] * ob_scr[0, heads[r], :] + gate[:, 3 * r + 1:3 * r + 2] * ob_scr[1, heads[r], :]
             + gate[:, 3 * r + 2:3 * r + 3] * o_win)
        o_ref[:, r * HEAD_DIM:(r + 1) * HEAD_DIM] = o.astype(o_ref.dtype)


def _nsa_mixer(slopes, proj, cmp_kv, gates_t, rel_tab, batch, seq):
    t = proj.shape[0]
    nq = seq // NSA_TQ
    ncp = cmp_kv.shape[2]
    n_slc = seq // SLC_BLOCK
    assert SEL_COL0 + n_slc <= LANES and n_slc % SUBLANES == 0 and seq % ATT_TK == 0
    rows = NSA_REP * NSA_TQ
    win_shifts = (WIN_SIZE - 1 + ATT_TK - 1 + NSA_TQ - LANES) // LANES + 1
    win_tab = _mask_table(win_shifts, lambda d: (d >= 0) & (d <= WIN_SIZE - 1))
    causal_tab = _mask_table(ATT_TK // LANES, lambda d: d >= 0)

    def seq_spec(cb):
        return pl.BlockSpec((seq, HEAD_DIM), lambda b, g, i: (b, cb + g))

    def cmp_spec(which):
        return pl.BlockSpec((None, None, ncp, HEAD_DIM), lambda b, g, i: (which, b * N_NSA_KV + g, 0, 0))

    def whole(a):
        return pl.BlockSpec(a.shape, lambda b, g, i: (0,) * a.ndim)

    qo_spec = pl.BlockSpec((NSA_TQ, NSA_REP * HEAD_DIM), lambda b, g, i: (b * nq + i, g))
    onehot = _block_onehot_table(seq)
    qx = _alibi_rows(slopes)
    return pl.pallas_call(
        functools.partial(_nsa_kernel, seq=seq),
        out_shape=jax.ShapeDtypeStruct((t, Q_NSA_DIM), BF16),
        grid=(batch, N_NSA_KV, nq),
        in_specs=[
            qo_spec,
            pl.BlockSpec((NSA_REP, SUBLANES, LANES), lambda b, g, i: (g, 0, 0)),
            cmp_spec(0), cmp_spec(1),
            seq_spec(CB_K_SLC), seq_spec(CB_V_SLC), seq_spec(CB_K_WIN), seq_spec(CB_V_WIN),
            whole(rel_tab), whole(onehot), whole(win_tab), whole(causal_tab),
            pl.BlockSpec((None, NSA_TQ, 3 * NSA_REP), lambda b, g, i: (g, b * nq + i, 0)),
        ],
        out_specs=qo_spec,
        scratch_shapes=[
            pltpu.VMEM((rows, HEAD_DIM + LANES), BF16),
            pltpu.VMEM((rows, ATT_TK), F32),
            pltpu.VMEM((rows, ATT_TK), F32),
            pltpu.VMEM((rows, ATT_TK), BF16),
            pltpu.VMEM((rows, LANES), F32),
            pltpu.VMEM((rows, LANES), F32),
            pltpu.VMEM((rows, HEAD_DIM), F32),
            pltpu.VMEM((2, rows, HEAD_DIM), F32),
            pltpu.VMEM((LANES, NSA_TQ), F32),
        ],
        compiler_params=_params("arbitrary", "arbitrary", "arbitrary"),
        name="nsa_mixer",
    )(proj, qx, cmp_kv, cmp_kv, proj, proj, proj, proj, rel_tab, onehot, win_tab, causal_tab, gates_t)


def _dil_kernel(q_ref, qx_ref, k_ref, v_ref, rel_ref, mtab_ref, o_ref,
                qa_scr, s_a_scr, s_b_scr, p_scr, m_scr, l_scr, acc_scr, *, seq):
    t0 = pl.program_id(2) * DIL_TQ
    qa_scr[:, :HEAD_DIM] = q_ref[...]
    qa_scr[:, HEAD_DIM:] = jnp.broadcast_to(qx_ref[0:1, :], (DIL_TQ, LANES)).astype(BF16)
    _softmax_init(m_scr, l_scr, acc_scr)
    masked = mtab_ref.shape[0] - 1

    def tile_rows(kt):
        return pl.ds(pl.multiple_of(kt * ATT_TK, ATT_TK), ATT_TK)

    def k_aug(kt):
        rel = rel_ref[pl.ds(pl.multiple_of(kt * ATT_TK - t0 + seq, LANES), ATT_TK), :]
        return jnp.concatenate([k_ref[tile_rows(kt), :], rel], axis=1)

    def multiplicity(kt):
        def for_block(rb):
            u = (t0 + rb * LANES - kt * ATT_TK) // LANES
            return mtab_ref[jnp.where((u < 0) | (u >= masked), masked, u)]
        return for_block

    reach = max(w for w, _ in DIL_PATTERNS)
    first_kt = jnp.maximum(t0 - reach, 0) // ATT_TK
    _attend(first_kt, (t0 + DIL_TQ - 1) // ATT_TK + 1, qa_scr, (s_a_scr, s_b_scr), p_scr,
            m_scr, l_scr, acc_scr, k_aug, lambda kt: v_ref[tile_rows(kt), :], multiplicity)
    o_ref[...] = (acc_scr[...] / l_scr[...]).astype(o_ref.dtype)


def _dil_mixer(slopes, proj, rel_tab, batch, seq):
    t = proj.shape[0]
    nq = seq // DIL_TQ
    mult_tab = _dilation_table()
    qx = _alibi_rows(slopes)

    def whole(a):
        return pl.BlockSpec(a.shape, lambda b, h, i: (0,) * a.ndim)

    return pl.pallas_call(
        functools.partial(_dil_kernel, seq=seq),
        out_shape=jax.ShapeDtypeStruct((t, DIL_DIM), BF16),
        grid=(batch, N_DIL_HEADS, nq),
        in_specs=[
            pl.BlockSpec((DIL_TQ, HEAD_DIM), lambda b, h, i: (b * nq + i, CB_Q_DIL + h)),
            pl.BlockSpec((None, SUBLANES, LANES), lambda b, h, i: (h, 0, 0)),
            pl.BlockSpec((seq, HEAD_DIM), lambda b, h, i: (b, CB_K_DIL + h)),
            pl.BlockSpec((seq, HEAD_DIM), lambda b, h, i: (b, CB_V_DIL + h)),
            whole(rel_tab), whole(mult_tab),
        ],
        out_specs=pl.BlockSpec((DIL_TQ, HEAD_DIM), lambda b, h, i: (b * nq + i, h)),
        scratch_shapes=[
            pltpu.VMEM((DIL_TQ, HEAD_DIM + LANES), BF16),
            pltpu.VMEM((DIL_TQ, ATT_TK), F32),
            pltpu.VMEM((DIL_TQ, ATT_TK), F32),
            pltpu.VMEM((DIL_TQ, ATT_TK), BF16),
            pltpu.VMEM((DIL_TQ, LANES), F32),
            pltpu.VMEM((DIL_TQ, LANES), F32),
            pltpu.VMEM((DIL_TQ, HEAD_DIM), F32),
        ],
        compiler_params=_params("arbitrary", "arbitrary", "arbitrary"),
        name="dil_mixer",
    )(proj, qx, proj, proj, rel_tab, mult_tab)


def _out_proj_kernel(a_ref, b_ref, ga_ref, gb_ref, w_ref, res_ref, o_ref, h_scr):
    @pl.when(pl.program_id(1) == 0)
    def _():
        na = a_ref.shape[1]
        h_scr[:, :na] = _rms_rows(a_ref[...].astype(F32), ga_ref[...]).astype(BF16)
        h_scr[:, na:] = _rms_rows(b_ref[...].astype(F32), gb_ref[...]).astype(BF16)

    o_ref[...] = res_ref[...] + _dot(h_scr[...], w_ref[...])


def _out_proj(o_nsa, o_dil, g_nsa, g_dil, w, resid):
    t, d = resid.shape
    na, nb = o_nsa.shape[1], o_dil.shape[1]
    return pl.pallas_call(
        _out_proj_kernel,
        out_shape=jax.ShapeDtypeStruct((t, d), F32),
        grid=(t // OUT_TM, d // PROJ_TN),
        in_specs=[
            pl.BlockSpec((OUT_TM, na), lambda i, j: (i, 0)),
            pl.BlockSpec((OUT_TM, nb), lambda i, j: (i, 0)),
            pl.BlockSpec((1, na), lambda i, j: (0, 0)),
            pl.BlockSpec((1, nb), lambda i, j: (0, 0)),
            pl.BlockSpec((na + nb, PROJ_TN), lambda i, j: (0, j)),
            pl.BlockSpec((OUT_TM, PROJ_TN), lambda i, j: (i, j)),
        ],
        out_specs=pl.BlockSpec((OUT_TM, PROJ_TN), lambda i, j: (i, j)),
        scratch_shapes=[pltpu.VMEM((OUT_TM, na + nb), BF16)],
        compiler_params=_params("parallel", "arbitrary"),
        name="out_proj",
    )(o_nsa, o_dil, g_nsa, g_dil, w, resid)


def _router_kernel(x_ref, g_ref, wr_ref, br_ref, hn_ref, idx_ref, tw_ref):
    h = _rms_rows(x_ref[...], g_ref[...])
    hn_ref[...] = h
    logits = _dot(h, wr_ref[...], precision=lax.Precision.HIGHEST) + br_ref[...]
    lane = lax.broadcasted_iota(I32, logits.shape, 1)
    lanef = lane.astype(F32)
    work = jnp.where(lane < N_EXPERTS, logits, -jnp.inf)
    vals, ids = [], []
    for _ in range(TOP_K):
        mx = jnp.max(work, axis=-1, keepdims=True)
        first = jnp.min(jnp.where(work == mx, lanef, float(LANES)), axis=-1, keepdims=True)
        vals.append(mx)
        ids.append(first)
        work = jnp.where(lanef == first, -jnp.inf, work)
    es = [jnp.exp(v - vals[0]) for v in vals]
    den = functools.reduce(lambda a, b: a + b, es)
    idx_out = jnp.zeros(logits.shape, F32)
    tw_out = jnp.zeros(logits.shape, F32)
    for k in range(TOP_K):
        idx_out = jnp.where(lane == k, ids[k], idx_out)
        tw_out = jnp.where(lane == k, es[k] / den, tw_out)
    idx_ref[...] = idx_out.astype(I32)
    tw_ref[...] = tw_out


def _router(x1, gain, wr, br):
    t, d = x1.shape
    return pl.pallas_call(
        _router_kernel,
        out_shape=(jax.ShapeDtypeStruct((t, d), F32),
                   jax.ShapeDtypeStruct((t, LANES), I32),
                   jax.ShapeDtypeStruct((t, LANES), F32)),
        grid=(t // ROUTER_TM,),
        in_specs=[
            pl.BlockSpec((ROUTER_TM, d), lambda i: (i, 0)),
            pl.BlockSpec((1, d), lambda i: (0, 0)),
            pl.BlockSpec((d, LANES), lambda i: (0, 0)),
            pl.BlockSpec((1, LANES), lambda i: (0, 0)),
        ],
        out_specs=(pl.BlockSpec((ROUTER_TM, d), lambda i: (i, 0)),
                   pl.BlockSpec((ROUTER_TM, LANES), lambda i: (i, 0)),
                   pl.BlockSpec((ROUTER_TM, LANES), lambda i: (i, 0))),
        compiler_params=_params("parallel"),
        name="moe_router",
    )(x1, gain, wr, br)


def _rank_kernel(idx_ref, rank_ref, cnt_ref, carry_scr):
    @pl.when(pl.program_id(0) == 0)
    def _():
        carry_scr[...] = jnp.zeros(carry_scr.shape, F32)

    idx = idx_ref[...]
    lane = lax.broadcasted_iota(I32, idx.shape, 1)
    hits = [lane == idx[:, k:k + 1] for k in range(TOP_K)]
    onehot = functools.reduce(lambda a, b: a + b, [jnp.where(h, 1.0, 0.0) for h in hits])
    ri = lax.broadcasted_iota(I32, (RANK_TM, RANK_TM), 0)
    ci = lax.broadcasted_iota(I32, (RANK_TM, RANK_TM), 1)
    before = jnp.where(ci < ri, 1.0, 0.0).astype(BF16)
    rank = _dot(before, onehot.astype(BF16)) + carry_scr[0:1, :]
    out = jnp.zeros(idx.shape, F32)
    for k in range(TOP_K):
        mine = jnp.sum(jnp.where(hits[k], rank, 0.0), axis=-1, keepdims=True)
        out = jnp.where(lane == k, mine, out)
    rank_ref[...] = out.astype(I32)
    carry = carry_scr[...] + jnp.sum(onehot, axis=0, keepdims=True)
    carry_scr[...] = carry
    cnt_ref[...] = carry


def _rank(idx):
    t = idx.shape[0]
    return pl.pallas_call(
        _rank_kernel,
        out_shape=(jax.ShapeDtypeStruct((t, LANES), I32), jax.ShapeDtypeStruct((SUBLANES, LANES), F32)),
        grid=(t // RANK_TM,),
        in_specs=[pl.BlockSpec((RANK_TM, LANES), lambda i: (i, 0))],
        out_specs=(pl.BlockSpec((RANK_TM, LANES), lambda i: (i, 0)),
                   pl.BlockSpec((SUBLANES, LANES), lambda i: (0, 0))),
        scratch_shapes=[pltpu.VMEM((SUBLANES, LANES), F32)],
        compiler_params=_params("arbitrary"),
        name="moe_rank",
    )(idx)


def _pos_kernel(cnt_ref, idx_ref, rank_ref, pos_ref, vt_ref):
    cnt = cnt_ref[0:1, :]
    lane1 = lax.broadcasted_iota(I32, (1, LANES), 1)
    tiles = jnp.where(lane1 < N_EXPERTS, jnp.floor((cnt + (MOE_TM - 0.5)) / MOE_TM), 0.0)
    tiles_before = jnp.zeros((1, LANES), F32)
    rows_before = jnp.zeros((1, LANES), F32)
    for e in range(N_EXPERTS):
        tiles_before = tiles_before + jnp.where(lane1 > e, tiles[:, e:e + 1], 0.0)
        rows_before = rows_before + jnp.where(lane1 > e, cnt[:, e:e + 1], 0.0)
    tiles_through = tiles_before + tiles

    idx = idx_ref[...]
    rank = rank_ref[...].astype(F32)
    lane = lax.broadcasted_iota(I32, idx.shape, 1)
    out = jnp.zeros(idx.shape, F32)
    for k in range(TOP_K):
        mine = lane == idx[:, k:k + 1]
        row0 = jnp.sum(jnp.where(mine, tiles_before * MOE_TM, 0.0), axis=-1, keepdims=True)
        slot0 = jnp.sum(jnp.where(mine, rows_before, 0.0), axis=-1, keepdims=True)
        out = jnp.where(lane == k, row0 + rank[:, k:k + 1], out)
        out = jnp.where(lane == TOP_K + k, slot0 + rank[:, k:k + 1], out)
    pos_ref[...] = out.astype(I32)

    nv = vt_ref.shape[0]
    vl = lax.broadcasted_iota(I32, (nv, LANES), 1)
    v = lax.broadcasted_iota(I32, (nv, LANES), 0).astype(F32)
    total = jnp.sum(tiles, axis=-1, keepdims=True)
    vv = jnp.minimum(v, total - 1.0)
    done = jnp.where((tiles_through <= vv) & (vl < N_EXPERTS), 1.0, 0.0)
    e_v = jnp.minimum(jnp.sum(done, axis=-1, keepdims=True), float(N_EXPERTS - 1))
    mine = vl.astype(F32) == e_v
    first_tile = jnp.sum(jnp.where(mine, tiles_before, 0.0), axis=-1, keepdims=True)
    e_cnt = jnp.sum(jnp.where(mine, cnt, 0.0), axis=-1, keepdims=True)
    e_slot = jnp.sum(jnp.where(mine, rows_before, 0.0), axis=-1, keepdims=True)
    vv1 = vv[:, 0:1]
    valid = v[:, 0:1] < total
    offset = (vv1 - first_tile) * MOE_TM
    held = jnp.where(valid, jnp.clip(e_cnt - offset, 0.0, float(MOE_TM)), 0.0)
    nrows = jnp.ceil(held * (1.0 / MOE_HALF)) * MOE_HALF
    cols = [e_v, vv1, nrows, jnp.where(valid, 1.0, 0.0), e_slot + offset, held, total]
    table = jnp.zeros((nv, LANES), F32)
    for c, val in enumerate(cols):
        table = jnp.where(vl == c, val, table)
    vt_ref[...] = table.astype(I32)


def _positions(cnt, idx, rank, n_visits):
    t = idx.shape[0]
    nvp = -(-n_visits // SUBLANES) * SUBLANES
    return pl.pallas_call(
        _pos_kernel,
        out_shape=(jax.ShapeDtypeStruct((t, LANES), I32), jax.ShapeDtypeStruct((nvp, LANES), I32)),
        grid=(t // POS_TM,),
        in_specs=[pl.BlockSpec((SUBLANES, LANES), lambda i: (0, 0)),
                  pl.BlockSpec((POS_TM, LANES), lambda i: (i, 0)),
                  pl.BlockSpec((POS_TM, LANES), lambda i: (i, 0))],
        out_specs=(pl.BlockSpec((POS_TM, LANES), lambda i: (i, 0)),
                   pl.BlockSpec((nvp, LANES), lambda i: (0, 0))),
        compiler_params=_params("arbitrary"),
        name="moe_positions",
    )(cnt, idx, rank)


def _invert_kernel(slot_ref, tok_ref):
    def body(g, carry):
        base = g * INVERT_UNROLL
        slots = [slot_ref[base + u] for u in range(INVERT_UNROLL)]
        for u in range(INVERT_UNROLL):
            tok_ref[slots[u]] = (base + u) >> top_k_shift
        return carry

    top_k_shift = TOP_K.bit_length() - 1
    assert TOP_K == 1 << top_k_shift and slot_ref.shape[0] % INVERT_UNROLL == 0
    lax.fori_loop(0, slot_ref.shape[0] // INVERT_UNROLL, body, 0)


def _invert(slot_flat):
    return pl.pallas_call(
        _invert_kernel,
        out_shape=jax.ShapeDtypeStruct(slot_flat.shape, I32),
        in_specs=[pl.BlockSpec(memory_space=pltpu.SMEM)],
        out_specs=pl.BlockSpec(memory_space=pltpu.SMEM),
        name="moe_invert",
    )(slot_flat)


def _dispatch_kernel(vt_ref, tok_ref, hn_ref, xs_ref, buf, sems):
    i = pl.program_id(0)
    per_visit = MOE_TM // DISPATCH_TM

    def held_rows(step):
        v = step // per_visit
        part = (step % per_visit) * DISPATCH_TM
        return jnp.clip(vt_ref[v * VT_W + 5] - part, 0, DISPATCH_TM), vt_ref[v * VT_W + 4] + part

    def row_copy(src_row, r, slot):
        return pltpu.make_async_copy(hn_ref.at[pl.ds(src_row, 1)], buf.at[slot, pl.ds(r, 1)], sems.at[slot])

    def start(step, slot):
        n, base = held_rows(step)
        issue = lambda r, c: (row_copy(tok_ref[base + r], r, slot).start(), c)[1]

        @pl.when(n == DISPATCH_TM)
        def _():
            lax.fori_loop(0, DISPATCH_TM, issue, 0, unroll=GATHER_UNROLL)

        @pl.when(n < DISPATCH_TM)
        def _():
            lax.fori_loop(0, n, issue, 0)

    def wait(step, slot):
        n, _ = held_rows(step)

        @pl.when(n == DISPATCH_TM)
        def _():
            pltpu.make_async_copy(hn_ref.at[pl.ds(0, DISPATCH_TM)], buf.at[slot], sems.at[slot]).wait()

        @pl.when(n < DISPATCH_TM)
        def _():
            lax.fori_loop(0, n, lambda r, c: (row_copy(0, r, slot).wait(), c)[1], 0)

    @pl.when(i == 0)
    def _():
        buf[...] = jnp.zeros(buf.shape, buf.dtype)
        start(0, 0)

    @pl.when(i + 1 < pl.num_programs(0))
    def _():
        start(i + 1, (i + 1) % 2)

    wait(i, i % 2)
    n, _ = held_rows(i)
    rowid = lax.broadcasted_iota(I32, (DISPATCH_TM, 1), 0)
    xs_ref[...] = jnp.where(rowid < n, buf[i % 2], 0.0).astype(xs_ref.dtype)


def _dispatch(vt, tok, hn, n_visits):
    d = hn.shape[1]
    return pl.pallas_call(
        _dispatch_kernel,
        out_shape=jax.ShapeDtypeStruct((n_visits * MOE_TM, d), BF16),
        grid_spec=pltpu.PrefetchScalarGridSpec(
            num_scalar_prefetch=2,
            grid=(n_visits * MOE_TM // DISPATCH_TM,),
            in_specs=[pl.BlockSpec(memory_space=pl.ANY)],
            out_specs=pl.BlockSpec((DISPATCH_TM, d), lambda i, vt, tok: (i, 0)),
            scratch_shapes=[pltpu.VMEM((2, DISPATCH_TM, d), F32), pltpu.SemaphoreType.DMA((2,))],
        ),
        compiler_params=_params("arbitrary"),
        name="moe_dispatch",
    )(vt, tok, hn)


def _visit_blocks(nrows, compute, clear):
    for sb in range(MOE_TM // MOE_SUB):
        lo, mid, hi = sb * MOE_SUB, sb * MOE_SUB + MOE_HALF, (sb + 1) * MOE_SUB
        pl.when(nrows >= hi)(functools.partial(compute, slice(lo, hi), sb == 0))

        @pl.when(nrows == mid)
        def _(lo=lo, mid=mid, hi=hi, sb=sb):
            compute(slice(lo, mid), sb == 0)
            clear(slice(mid, hi))

        pl.when(nrows <= lo)(functools.partial(clear, slice(lo, hi)))


def _moe_up_kernel(vt_ref, xs_ref, wg_ref, wu_ref, bg_ref, bu_ref, h_ref, wg_scr, wu_scr):
    def compute(rows, first):
        if first:
            wg_scr[...] = wg_ref[...].astype(BF16)
            wu_scr[...] = wu_ref[...].astype(BF16)
        x = xs_ref[rows, :]
        gate = jnp.minimum(_dot(x, wg_scr[...]) + bg_ref[...], SWIGLU_LIMIT)
        up = jnp.clip(_dot(x, wu_scr[...]) + bu_ref[...], -SWIGLU_LIMIT, SWIGLU_LIMIT)
        act = (up + 1.0) * (gate * _sigmoid(SWIGLU_ALPHA * gate))
        h_ref[rows, :] = act.astype(h_ref.dtype)

    def clear(rows):
        h_ref[rows, :] = jnp.zeros((rows.stop - rows.start, h_ref.shape[1]), h_ref.dtype)

    _visit_blocks(vt_ref[pl.program_id(0) * VT_W + 2], compute, clear)


def _moe_up(vt, xs, w_gate_up, b_gate_up, n_visits):
    d = w_gate_up.shape[1]
    nf = D_EXPERT // MOE_TF

    def fsel(v, f, vt):
        return jnp.where(vt[v * VT_W + 3] > 0, f, nf - 1)

    return pl.pallas_call(
        _moe_up_kernel,
        out_shape=jax.ShapeDtypeStruct((n_visits * MOE_TM, D_EXPERT), BF16),
        grid_spec=pltpu.PrefetchScalarGridSpec(
            num_scalar_prefetch=1,
            grid=(n_visits, nf),
            in_specs=[
                pl.BlockSpec((MOE_TM, d), lambda v, f, vt: (vt[v * VT_W + 1], 0)),
                pl.BlockSpec((None, d, MOE_TF), lambda v, f, vt: (vt[v * VT_W], 0, fsel(v, f, vt))),
                pl.BlockSpec((None, d, MOE_TF), lambda v, f, vt: (vt[v * VT_W], 0, nf + fsel(v, f, vt))),
                pl.BlockSpec((None, 1, MOE_TF), lambda v, f, vt: (vt[v * VT_W], 0, fsel(v, f, vt))),
                pl.BlockSpec((None, 1, MOE_TF), lambda v, f, vt: (vt[v * VT_W], 0, nf + fsel(v, f, vt))),
            ],
            out_specs=pl.BlockSpec((MOE_TM, MOE_TF), lambda v, f, vt: (v, f)),
            scratch_shapes=[pltpu.VMEM((d, MOE_TF), BF16), pltpu.VMEM((d, MOE_TF), BF16)],
        ),
        compiler_params=_params("arbitrary", "arbitrary"),
        name="moe_up",
    )(vt, xs, w_gate_up, w_gate_up, b_gate_up, b_gate_up)


def _moe_down_kernel(vt_ref, h_ref, wd_ref, bd_ref, y_ref, wd_scr):
    def compute(rows, first):
        if first:
            wd_scr[...] = wd_ref[...].astype(BF16)
        y_ref[rows, :] = _dot(h_ref[rows, :], wd_scr[...]) + bd_ref[...]

    def clear(rows):
        y_ref[rows, :] = jnp.zeros((rows.stop - rows.start, y_ref.shape[1]), y_ref.dtype)

    _visit_blocks(vt_ref[pl.program_id(0) * VT_W + 2], compute, clear)


def _moe_down(vt, h, w_down, b_down, n_visits):
    d = w_down.shape[2]
    nj = d // MOE_TN

    def jsel(v, j, vt):
        return jnp.where(vt[v * VT_W + 3] > 0, j, nj - 1)

    return pl.pallas_call(
        _moe_down_kernel,
        out_shape=jax.ShapeDtypeStruct((n_visits * MOE_TM, d), F32),
        grid_spec=pltpu.PrefetchScalarGridSpec(
            num_scalar_prefetch=1,
            grid=(n_visits, nj),
            in_specs=[
                pl.BlockSpec((MOE_TM, D_EXPERT), lambda v, j, vt: (vt[v * VT_W + 1], 0)),
                pl.BlockSpec((None, D_EXPERT, MOE_TN), lambda v, j, vt: (vt[v * VT_W], 0, jsel(v, j, vt))),
                pl.BlockSpec((None, 1, MOE_TN), lambda v, j, vt: (vt[v * VT_W], 0, jsel(v, j, vt))),
            ],
            out_specs=pl.BlockSpec((MOE_TM, MOE_TN), lambda v, j, vt: (v, j)),
            scratch_shapes=[pltpu.VMEM((D_EXPERT, MOE_TN), BF16)],
        ),
        compiler_params=_params("arbitrary", "arbitrary"),
        name="moe_down",
    )(vt, h, w_down, b_down)


def _combine_kernel(pos_ref, ys_ref, x_ref, tw_ref, o_ref, buf, sems):
    i = pl.program_id(0)

    def start(step, slot):
        def issue(r, carry):
            for k in range(TOP_K):
                src = pos_ref[(step * COMBINE_TM + r) * TOP_K + k]
                pltpu.make_async_copy(ys_ref.at[pl.ds(src, 1)], buf.at[slot, k, pl.ds(r, 1)], sems.at[slot]).start()
            return carry
        lax.fori_loop(0, COMBINE_TM, issue, 0, unroll=GATHER_UNROLL)

    def wait(slot):
        for k in range(TOP_K):
            pltpu.make_async_copy(ys_ref.at[pl.ds(0, COMBINE_TM)], buf.at[slot, k], sems.at[slot]).wait()

    @pl.when(i == 0)
    def _():
        start(0, 0)

    @pl.when(i + 1 < pl.num_programs(0))
    def _():
        start(i + 1, (i + 1) % 2)

    wait(i % 2)
    tw = tw_ref[...]
    acc = x_ref[...]
    for k in range(TOP_K):
        acc = acc + tw[:, k:k + 1] * buf[i % 2, k]
    o_ref[...] = acc


def _combine(pos_flat, ys, x1, tw):
    t, d = x1.shape
    return pl.pallas_call(
        _combine_kernel,
        out_shape=jax.ShapeDtypeStruct((t, d), F32),
        grid_spec=pltpu.PrefetchScalarGridSpec(
            num_scalar_prefetch=1,
            grid=(t // COMBINE_TM,),
            in_specs=[pl.BlockSpec(memory_space=pl.ANY),
                      pl.BlockSpec((COMBINE_TM, d), lambda i, pos: (i, 0)),
                      pl.BlockSpec((COMBINE_TM, LANES), lambda i, pos: (i, 0))],
            out_specs=pl.BlockSpec((COMBINE_TM, d), lambda i, pos: (i, 0)),
            scratch_shapes=[pltpu.VMEM((2, TOP_K, COMBINE_TM, d), F32), pltpu.SemaphoreType.DMA((2,))],
        ),
        compiler_params=_params("arbitrary"),
        name="moe_combine",
    )(pos_flat, ys, x1, tw)


def _alibi_slopes():
    n = N_NSA_HEADS + N_DIL_HEADS
    i = jnp.arange(1, n + 1, dtype=F32)
    return jnp.exp2(-8.0 * i / n)


def _attention_block(x2, batch, seq, attn_norm, w_in, pe_k, w_k1, w_k2, pe_v, w_v1, w_v2,
                     qn_nsa, kn_cmp, kn_slc, kn_win, qn_dil, kn_dil, on_nsa, on_dil, w_out):
    d = x2.shape[1]
    a_dim = Q_NSA_DIM + 6 * KV_NSA_DIM
    d_off = a_dim + GATE_DIM
    n_pad = N_PROJ - (a_dim + 3 * DIL_DIM + GATE_DIM)
    assert a_dim % PROJ_TN == 0
    wa = w_in[:, :a_dim].astype(BF16)
    wb = jnp.concatenate([w_in[:, d_off:].astype(BF16), w_in[:, a_dim:d_off].astype(BF16),
                          jnp.zeros((d, n_pad), BF16)], axis=1)
    ones_kv = jnp.ones((KV_NSA_DIM,), F32)
    ones_dil = jnp.ones((DIL_DIM,), F32)
    tail = jnp.ones((GATE_DIM + n_pad,), F32)
    q_scale = ATTN_SCALE * LOG2E
    col_gain = jnp.concatenate([
        jnp.tile(qn_nsa, N_NSA_HEADS) * q_scale, ones_kv, ones_kv, jnp.tile(kn_slc, N_NSA_KV), ones_kv,
        jnp.tile(kn_win, N_NSA_KV), ones_kv, jnp.tile(qn_dil, N_DIL_HEADS) * q_scale,
        jnp.tile(kn_dil, N_DIL_HEADS), ones_dil, tail])[None, :]
    col_flag = jnp.concatenate([
        jnp.ones((Q_NSA_DIM,), F32), 0 * ones_kv, 0 * ones_kv, ones_kv, 0 * ones_kv, ones_kv, 0 * ones_kv,
        ones_dil, ones_dil, 0 * ones_dil, 0 * tail])[None, :]
    proj = _in_proj(x2, attn_norm[None, :], wa, wb, col_gain, col_flag)

    n_chunks = seq // CMP_STRIDE
    kv = proj[:, CB_K_CMP * LANES:(CB_V_CMP + N_NSA_KV) * LANES]
    kv = kv.reshape(batch, n_chunks, CMP_STRIDE, 2, N_NSA_KV, HEAD_DIM).transpose(3, 0, 4, 1, 2, 5)
    chunks = kv.reshape(2, batch * N_NSA_KV, n_chunks, CMP_STRIDE * HEAD_DIM)
    assert CMP_BLOCK == 2 * CMP_STRIDE
    blocks = jnp.concatenate([chunks[:, :, :-1], chunks[:, :, 1:]], axis=-1)
    blocks = jnp.pad(blocks, ((0, 0), (0, 0), (0, 1), (0, 0)))
    pe = jnp.stack([pe_k.reshape(1, -1), pe_v.reshape(1, -1)])
    w1 = jnp.stack([w_k1, w_v1]).astype(BF16)
    w2 = jnp.stack([w_k2, w_v2]).astype(BF16)
    cmp_kv = _compress(blocks, pe, w1, w2, kn_cmp[None, :])

    slopes = _alibi_slopes()
    rel_tab = _rel_position_table(seq)
    gates = proj[:, CB_GATE * LANES:CB_GATE * LANES + GATE_DIM]
    gates_t = gates.reshape(-1, N_NSA_KV, 3 * NSA_REP).transpose(1, 0, 2)
    o_nsa = _nsa_mixer(slopes[0::2], proj, cmp_kv, gates_t, rel_tab, batch, seq)
    o_dil = _dil_mixer(slopes[1::2], proj, rel_tab, batch, seq)
    return _out_proj(o_nsa, o_dil, on_nsa[None, :], on_dil[None, :], w_out.astype(BF16), x2)


def _moe_block(x1, ffn_norm, w_router, b_router, w_gate_up, b_gate_up, w_down, b_down):
    t, d = x1.shape
    wr = jnp.pad(w_router, ((0, 0), (0, LANES - N_EXPERTS)))
    br = jnp.pad(b_router, (0, LANES - N_EXPERTS))[None, :]
    hn, idx, tw = _router(x1, ffn_norm[None, :], wr, br)
    rank, cnt = _rank(idx)
    n_worst = N_EXPERTS + (t * TOP_K) // MOE_TM
    pos, table = _positions(cnt, idx, rank, n_worst)
    pos_flat = pos[:, :TOP_K].reshape(-1)
    tok = _invert(pos[:, TOP_K:2 * TOP_K].reshape(-1))
    bias_gu, bias_d = b_gate_up[:, None, :], b_down[:, None, :]

    def experts(n_visits):
        vt = table[:n_visits, :VT_W].reshape(-1)
        xs = _dispatch(vt, tok, hn, n_visits)
        h = _moe_up(vt, xs, w_gate_up, bias_gu, n_visits)
        ys = _moe_down(vt, h, w_down, bias_d, n_visits)
        return _combine(pos_flat, ys, x1, tw)

    n_even = min(n_worst, N_EXPERTS + MOE_SPARE_VISITS)
    return lax.cond(table[0, VT_TOTAL] <= n_even, lambda: experts(n_even), lambda: experts(n_worst))


def kernel(x, attn_norm, w_in, cmp_pos_k, w_cmp_k1, w_cmp_k2, cmp_pos_v, w_cmp_v1, w_cmp_v2, q_norm_nsa, k_norm_cmp, k_norm_slc, k_norm_win, q_norm_dil, k_norm_dil, out_norm_nsa, out_norm_dil, w_out, ffn_norm, w_router, b_router, w_gate_up, b_gate_up, w_down, b_down):
    batch, seq, d = x.shape
    x2 = x.reshape(batch * seq, d)
    for layer in range(attn_norm.shape[0]):
        x2 = _attention_block(
            x2, batch, seq, attn_norm[layer], w_in[layer], cmp_pos_k[layer], w_cmp_k1[layer],
            w_cmp_k2[layer], cmp_pos_v[layer], w_cmp_v1[layer], w_cmp_v2[layer], q_norm_nsa[layer],
            k_norm_cmp[layer], k_norm_slc[layer], k_norm_win[layer], q_norm_dil[layer],
            k_norm_dil[layer], out_norm_nsa[layer], out_norm_dil[layer], w_out[layer])
        x2 = _moe_block(x2, ffn_norm[layer], w_router[layer], b_router[layer], w_gate_up[layer],
                        b_gate_up[layer], w_down[layer], b_down[layer])
    return x2.reshape(batch, seq, d)
```

```python
import functools

import numpy as np
import jax
import jax.numpy as jnp
from jax import lax
from jax.experimental import pallas as pl
from jax.experimental.pallas import tpu as pltpu

F32, BF16, I32 = jnp.float32, jnp.bfloat16, jnp.int32

HEAD_DIM = 128
N_NSA_HEADS = 16
N_NSA_KV = 4
NSA_REP = N_NSA_HEADS // N_NSA_KV
N_DIL_HEADS = 16
CMP_BLOCK = 32
CMP_STRIDE = 16
CMP_HIDDEN = 256
SLC_BLOCK = 64
SLC_TOP_N = 16
FORCED_SCORE = 1.0e4
WIN_SIZE = 512
DIL_PATTERNS = ((128, 1), (512, 4), (2048, 16))
N_EXPERTS = 32
TOP_K = 4
D_EXPERT = 1536
SWIGLU_ALPHA = 1.702
SWIGLU_LIMIT = 7.0
RMS_EPS = 1e-6
NEG_INF = -1e30
ATTN_SCALE = HEAD_DIM ** -0.5
LOG2E = 1.4426950408889634

Q_NSA_DIM = N_NSA_HEADS * HEAD_DIM
KV_NSA_DIM = N_NSA_KV * HEAD_DIM
GATE_DIM = N_NSA_HEADS * 3
DIL_DIM = N_DIL_HEADS * HEAD_DIM

LANES = 128
SUBLANES = 8
VMEM_LIMIT = 56 * 1024 * 1024

CB_Q_NSA = 0
CB_K_CMP = CB_Q_NSA + N_NSA_HEADS
CB_V_CMP = CB_K_CMP + N_NSA_KV
CB_K_SLC = CB_V_CMP + N_NSA_KV
CB_V_SLC = CB_K_SLC + N_NSA_KV
CB_K_WIN = CB_V_SLC + N_NSA_KV
CB_V_WIN = CB_K_WIN + N_NSA_KV
CB_Q_DIL = CB_V_WIN + N_NSA_KV
CB_K_DIL = CB_Q_DIL + N_DIL_HEADS
CB_V_DIL = CB_K_DIL + N_DIL_HEADS
CB_GATE = CB_V_DIL + N_DIL_HEADS
PROJ_TN = 512
N_PROJ = -(-(CB_GATE + 1) * LANES // PROJ_TN) * PROJ_TN

PROJ_TM = 1024
OUT_TM = 1024
NSA_TQ = 256
DIL_TQ = 512
ATT_TK = 256
ROUTER_TM = 256
RANK_TM = 512
POS_TM = 1024
MOE_TM = 1536
MOE_SUB = 256
MOE_HALF = MOE_SUB // 2
MOE_TF = 256
MOE_TN = 1024
INVERT_UNROLL = 16
GATHER_UNROLL = 8
DISPATCH_TM = 512
COMBINE_TM = 128
VT_W = 8
VT_TOTAL = 6
MOE_SPARE_VISITS = 8

POS_SHIFT = 8
POS_RADIX = 1 << POS_SHIFT
ALIBI_PIECES = 3
ALIBI_COLS = 2 * ALIBI_PIECES
SEL_COL0 = 64
MASK_BIG = 2.0 ** 100


def _dot(a, b, **kw):
    return jnp.dot(a, b, preferred_element_type=F32, **kw)


def _dot_nt(a, b, **kw):
    return lax.dot_general(a, b, (((1,), (1,)), ((), ())), preferred_element_type=F32, **kw)


def _rms_rows(x, gain):
    ms = jnp.mean(x * x, axis=-1, keepdims=True)
    return x * lax.rsqrt(ms + RMS_EPS) * gain


def _sigmoid(x):
    return 1.0 / (1.0 + jnp.exp(-x))


def _params(*sem):
    return pltpu.CompilerParams(dimension_semantics=sem, vmem_limit_bytes=VMEM_LIMIT)


def _in_proj_kernel(x_ref, g_ref, wa_ref, wb_ref, cg_ref, cf_ref, o_ref, h_scr, *, na_tiles):
    j = pl.program_id(1)

    @pl.when(j == 0)
    def _():
        for c in range(x_ref.shape[0] // LANES):
            rs = slice(c * LANES, (c + 1) * LANES)
            h_scr[rs, :] = _rms_rows(x_ref[rs, :], g_ref[...]).astype(BF16)

    def tile(w_ref):
        acc = _dot(h_scr[...], w_ref[...])
        for c in range(PROJ_TN // LANES):
            cs = slice(c * LANES, (c + 1) * LANES)
            a = acc[:, cs]
            r = lax.rsqrt(jnp.mean(a * a, axis=-1, keepdims=True) + RMS_EPS)
            y = a * jnp.where(cf_ref[:, cs] > 0, r, 1.0) * cg_ref[:, cs]
            o_ref[:, cs] = y.astype(o_ref.dtype)

    pl.when(j < na_tiles)(functools.partial(tile, wa_ref))
    pl.when(j >= na_tiles)(functools.partial(tile, wb_ref))


def _in_proj(x2, gain, wa, wb, col_gain, col_flag):
    t, d = x2.shape
    na_tiles = wa.shape[1] // PROJ_TN
    n = wa.shape[1] + wb.shape[1]
    return pl.pallas_call(
        functools.partial(_in_proj_kernel, na_tiles=na_tiles),
        out_shape=jax.ShapeDtypeStruct((t, n), BF16),
        grid=(t // PROJ_TM, n // PROJ_TN),
        in_specs=[
            pl.BlockSpec((PROJ_TM, d), lambda i, j: (i, 0), pipeline_mode=pl.Buffered(1)),
            pl.BlockSpec((1, d), lambda i, j: (0, 0)),
            pl.BlockSpec((d, PROJ_TN), lambda i, j: (0, jnp.minimum(j, na_tiles - 1))),
            pl.BlockSpec((d, PROJ_TN), lambda i, j: (0, jnp.maximum(j - na_tiles, 0))),
            pl.BlockSpec((1, PROJ_TN), lambda i, j: (0, j)),
            pl.BlockSpec((1, PROJ_TN), lambda i, j: (0, j)),
        ],
        out_specs=pl.BlockSpec((PROJ_TM, PROJ_TN), lambda i, j: (i, j)),
        scratch_shapes=[pltpu.VMEM((PROJ_TM, d), BF16)],
        compiler_params=_params("parallel", "arbitrary"),
        name="in_proj",
    )(x2, gain, wa, wb, col_gain, col_flag)


def _compress_kernel(a_ref, pe_ref, w1_ref, w2_ref, kn_ref, o_ref):
    a = (a_ref[...].astype(F32) + pe_ref[...]).astype(BF16)
    hid = _dot(a, w1_ref[...])
    hid = hid * _sigmoid(hid)
    out = _dot(hid.astype(BF16), w2_ref[...])
    normed = _rms_rows(out, kn_ref[...])
    o_ref[...] = jnp.where(pl.program_id(0) == 0, normed, out).astype(o_ref.dtype)


def _compress(blocks, pe, w1, w2, kn):
    two, bg, ncp, flat = blocks.shape
    return pl.pallas_call(
        _compress_kernel,
        out_shape=jax.ShapeDtypeStruct((two, bg, ncp, HEAD_DIM), BF16),
        grid=(two, bg),
        in_specs=[
            pl.BlockSpec((None, None, ncp, flat), lambda s, i: (s, i, 0, 0)),
            pl.BlockSpec((None, 1, flat), lambda s, i: (s, 0, 0)),
            pl.BlockSpec((None, flat, CMP_HIDDEN), lambda s, i: (s, 0, 0)),
            pl.BlockSpec((None, CMP_HIDDEN, HEAD_DIM), lambda s, i: (s, 0, 0)),
            pl.BlockSpec((1, HEAD_DIM), lambda s, i: (0, 0)),
        ],
        out_specs=pl.BlockSpec((None, None, ncp, HEAD_DIM), lambda s, i: (s, i, 0, 0)),
        compiler_params=_params("arbitrary", "arbitrary"),
        name="nsa_compress",
    )(blocks, pe, w1, w2, kn)


def _split_position(v):
    hi = np.floor_divide(v, POS_RADIX)
    return hi, v - POS_RADIX * hi


def _rel_position_table(seq):
    hi, lo = _split_position(np.arange(2 * seq) - seq)
    tab = np.zeros((2 * seq, LANES), np.float32)
    tab[:, 0:ALIBI_COLS:2] = hi[:, None]
    tab[:, 1:ALIBI_COLS:2] = lo[:, None]
    return jnp.asarray(tab, BF16)


def _block_onehot_table(seq):
    tab = np.zeros((seq, LANES), np.float32)
    tab[np.arange(seq), SEL_COL0 + np.arange(seq) // SLC_BLOCK] = 1.0
    return jnp.asarray(tab, BF16)


def _alibi_rows(slopes):
    rest = slopes.astype(F32) * LOG2E
    cols = []
    for _ in range(ALIBI_PIECES):
        piece = rest.astype(BF16).astype(F32)
        rest = rest - piece
        cols += [piece * POS_RADIX, piece]
    rows = jnp.concatenate([jnp.stack(cols, axis=1),
                            jnp.zeros((slopes.shape[0], LANES - ALIBI_COLS), F32)], axis=1)
    return jnp.broadcast_to(rows[:, None, :], (slopes.shape[0], SUBLANES, LANES))


def _tile_distance(u):
    return LANES * u + np.arange(LANES)[:, None] - np.arange(ATT_TK)[None, :]


def _mask_table(n, valid_fn):
    tabs = [np.where(valid_fn(_tile_distance(u)), 0.0, -MASK_BIG) for u in range(n)]
    tabs += [np.zeros((LANES, ATT_TK)), np.full((LANES, ATT_TK), -MASK_BIG)]
    return jnp.asarray(np.stack(tabs), F32)


def _dilation_table():
    reach = max(w for w, _ in DIL_PATTERNS)
    n = (reach + ATT_TK - 1) // LANES + 1
    tabs = []
    for u in range(n):
        d = _tile_distance(u)
        mult = sum(((d >= 0) & (d <= w) & (d % dil == 0)).astype(np.float32) for w, dil in DIL_PATTERNS)
        tabs.append(np.where(mult > 0, np.log2(np.maximum(mult, 1.0)), -MASK_BIG))
    tabs.append(np.full((LANES, ATT_TK), -MASK_BIG))
    return jnp.asarray(np.stack(tabs), F32)


def _softmax_tile(s_buf, p_buf, m_scr, l_scr, acc_scr, bias_for_block):
    for rb in range(s_buf.shape[0] // LANES):
        rs = slice(rb * LANES, (rb + 1) * LANES)
        s = s_buf[rs, :] + bias_for_block(rb)
        m_old = m_scr[rs, :]
        m_new = jnp.maximum(m_old, jnp.max(s, axis=-1, keepdims=True))
        alpha = jnp.exp2(m_old - m_new)
        p = [jnp.exp2(s[:, c * LANES:(c + 1) * LANES] - m_new) for c in range(ATT_TK // LANES)]
        total = functools.reduce(lambda a, b: a + b, p)
        l_scr[rs, :] = alpha * l_scr[rs, :] + jnp.sum(total, axis=-1, keepdims=True)
        acc_scr[rs, :] = alpha * acc_scr[rs, :]
        m_scr[rs, :] = m_new
        for c in range(ATT_TK // LANES):
            p_buf[rs, c * LANES:(c + 1) * LANES] = p[c].astype(BF16)


def _softmax_init(m_scr, l_scr, acc_scr):
    m_scr[...] = jnp.full(m_scr.shape, NEG_INF, F32)
    l_scr[...] = jnp.zeros(l_scr.shape, F32)
    acc_scr[...] = jnp.zeros(acc_scr.shape, F32)


def _attend(first, end, qa_scr, s_bufs, p_scr, m_scr, l_scr, acc_scr, load_k_aug, load_v, bias_for_tile):
    s_a, s_b = s_bufs
    last = end - 1

    def scores(kt, s_buf):
        s_buf[...] = _dot_nt(qa_scr[...], load_k_aug(jnp.minimum(kt, last)))

    def absorb(kt, s_buf):
        _softmax_tile(s_buf, p_scr, m_scr, l_scr, acc_scr, bias_for_tile(kt))
        acc_scr[...] += _dot(p_scr[...], load_v(kt))

    scores(first, s_a)

    def pair(j, carry):
        a = first + 2 * j
        scores(a + 1, s_b)
        absorb(a, s_a)
        scores(a + 2, s_a)
        absorb(a + 1, s_b)
        return carry

    lax.fori_loop(0, (end - first) // 2, pair, 0)

    @pl.when((end - first) % 2 == 1)
    def _():
        absorb(last, s_a)


def _nsa_kernel(q_ref, qx_ref, kc_ref, vc_ref, ks_ref, vs_ref, kw_ref, vw_ref, rel_ref, oh_ref,
                wtab_ref, ctab_ref, gt_ref, o_ref,
                qa_scr, s_a_scr, s_b_scr, p_scr, m_scr, l_scr, acc_scr, ob_scr, sc_scr, *, seq):
    s_bufs = (s_a_scr, s_b_scr)
    t0 = pl.program_id(2) * NSA_TQ
    n_slc = seq // SLC_BLOCK
    heads = [slice(r * NSA_TQ, (r + 1) * NSA_TQ) for r in range(NSA_REP)]
    alibi = [jnp.broadcast_to(qx_ref[r, 0:1, :], (NSA_TQ, LANES)) for r in range(NSA_REP)]
    for r in range(NSA_REP):
        qa_scr[heads[r], :HEAD_DIM] = q_ref[:, r * HEAD_DIM:(r + 1) * HEAD_DIM]
        qa_scr[heads[r], HEAD_DIM:] = alibi[r].astype(BF16)

    ncp = kc_ref.shape[0]
    n_i = lax.broadcasted_iota(I32, (ncp, LANES), 0)
    ln = lax.broadcasted_iota(I32, (ncp, LANES), 1)
    rel_end = n_i * CMP_STRIDE + (CMP_BLOCK - 1) - t0
    digits = jnp.where((ln & 1) == 0, rel_end >> POS_SHIFT, rel_end & (POS_RADIX - 1))
    feat = jnp.where(ln < ALIBI_COLS, digits, 0).astype(F32).astype(BF16)
    kc_aug = jnp.concatenate([kc_ref[...], feat], axis=1)
    row = lax.broadcasted_iota(I32, (NSA_TQ, 1), 0) + t0
    blk_end = lax.broadcasted_iota(I32, (NSA_TQ, ncp), 1) * CMP_STRIDE + (CMP_BLOCK - 1)
    valid_c = row >= blk_end
    s_a_scr[:, :ncp] = _dot_nt(qa_scr[...], kc_aug)
    psum = jnp.zeros((NSA_TQ, ncp), F32)
    for r in range(NSA_REP):
        s = jnp.where(valid_c, s_a_scr[heads[r], :ncp], NEG_INF)
        mc = jnp.max(s, axis=-1, keepdims=True)
        pc = jnp.where(valid_c, jnp.exp2(s - mc), 0.0)
        p = pc / jnp.maximum(jnp.sum(pc, axis=-1, keepdims=True), 1e-30)
        psum = psum + p
        p_scr[heads[r], :ncp] = p.astype(BF16)
    ob_scr[0] = _dot(p_scr[:, :ncp], vc_ref[...])

    n_top = min(SLC_TOP_N, n_slc)

    @pl.when(t0 + NSA_TQ > n_top * SLC_BLOCK)
    def _():
        per_slc = SLC_BLOCK // CMP_STRIDE
        back = CMP_BLOCK // CMP_STRIDE - 1
        jb = lax.broadcasted_iota(I32, (LANES, ncp), 0)
        nb = lax.broadcasted_iota(I32, (LANES, ncp), 1)
        overlap_t = jnp.where((nb >= per_slc * jb - back) & (nb < per_slc * (jb + 1)), 1.0, 0.0)
        imp_t = _dot_nt(overlap_t, psum, precision=lax.Precision.HIGHEST)
        j = lax.broadcasted_iota(I32, (LANES, NSA_TQ), 0)
        tq = lax.broadcasted_iota(I32, (LANES, NSA_TQ), 1) + t0
        cur = tq // SLC_BLOCK
        forced = (j == 0) | (j == cur) | (j == cur - 1)
        sc_scr[...] = jnp.where(forced, FORCED_SCORE, jnp.where(j * SLC_BLOCK <= tq, imp_t, -1.0))

        n_slab = n_slc // SUBLANES
        slabs = [sc_scr[SUBLANES * v:SUBLANES * (v + 1), :] for v in range(n_slab)]
        ranks = [jnp.zeros((SUBLANES, NSA_TQ), F32) for _ in range(n_slab)]
        sub = lax.broadcasted_iota(I32, (SUBLANES, NSA_TQ), 0)
        for i in range(n_slc):
            rival = jnp.broadcast_to(sc_scr[i:i + 1, :], (SUBLANES, NSA_TQ))
            for v in range(n_slab):
                wins_ties = jnp.where(rival >= slabs[v], 1.0, 0.0)
                loses_ties = jnp.where(rival > slabs[v], 1.0, 0.0)
                if SUBLANES * v > i:
                    ranks[v] = ranks[v] + wins_ties
                elif SUBLANES * (v + 1) <= i:
                    ranks[v] = ranks[v] + loses_ties
                else:
                    ranks[v] = ranks[v] + jnp.where(sub + SUBLANES * v > i, wins_ties, loses_ties)
        pen_rows = [jnp.zeros((SEL_COL0, NSA_TQ), F32)]
        pen_rows += [jnp.where(rk < n_top, 0.0, -MASK_BIG) for rk in ranks]
        if SEL_COL0 + n_slc < LANES:
            pen_rows.append(jnp.zeros((LANES - SEL_COL0 - n_slc, NSA_TQ), F32))
        penalty = jnp.concatenate(pen_rows, axis=0).T
        for r in range(NSA_REP):
            qa_scr[heads[r], HEAD_DIM:] = (alibi[r] + penalty).astype(BF16)

    def tile_rows(kt):
        return pl.ds(pl.multiple_of(kt * ATT_TK, ATT_TK), ATT_TK)

    def k_aug(kt, k_ref, with_blocks):
        extra = rel_ref[pl.ds(pl.multiple_of(kt * ATT_TK - t0 + seq, LANES), ATT_TK), :]
        if with_blocks:
            extra = extra + oh_ref[tile_rows(kt), :]
        return jnp.concatenate([k_ref[tile_rows(kt), :], extra], axis=1)

    q_blocks = NSA_TQ // LANES
    last_kt = (t0 + NSA_TQ - 1) // ATT_TK

    def tile_shift(kt, rb):
        return (t0 + (rb % q_blocks) * LANES - kt * ATT_TK) // LANES

    def masks(tab_ref, index):
        return lambda kt: (lambda rb: tab_ref[index(kt, rb)])

    unmasked = ctab_ref.shape[0] - 2
    _softmax_init(m_scr, l_scr, acc_scr)
    _attend(0, last_kt + 1, qa_scr, s_bufs, p_scr, m_scr, l_scr, acc_scr,
            lambda kt: k_aug(kt, ks_ref, True), lambda kt: vs_ref[tile_rows(kt), :],
            masks(ctab_ref, lambda kt, qb: jnp.where(kt == last_kt, tile_shift(kt, qb), unmasked)))
    ob_scr[1] = acc_scr[...] / l_scr[...]

    _softmax_init(m_scr, l_scr, acc_scr)
    first_kt = jnp.maximum(t0 - (WIN_SIZE - 1), 0) // ATT_TK
    _attend(first_kt, last_kt + 1, qa_scr, s_bufs, p_scr, m_scr, l_scr, acc_scr,
            lambda kt: k_aug(kt, kw_ref, False), lambda kt: vw_ref[tile_rows(kt), :],
            masks(wtab_ref, tile_shift))

    gate = _sigmoid(gt_ref[...].astype(F32))
    for r in range(NSA_REP):
        o_win = acc_scr[heads[r], :] / l_scr[heads[r], :]
        o = (gate[:, 3 * r:3 * r + 1] * ob_scr[0, heads[r], :] + gate[:, 3 * r + 1:3 * r + 2] * ob_scr[1, heads[r], :]
             + gate[:, 3 * r + 2:3 * r + 3] * o_win)
        o_ref[:, r * HEAD_DIM:(r + 1) * HEAD_DIM] = o.astype(o_ref.dtype)


def _nsa_mixer(slopes, proj, cmp_kv, gates_t, rel_tab, batch, seq):
    t = proj.shape[0]
    nq = seq // NSA_TQ
    ncp = cmp_kv.shape[2]
    n_slc = seq // SLC_BLOCK
    assert SEL_COL0 + n_slc <= LANES and n_slc % SUBLANES == 0 and seq % ATT_TK == 0
    assert ncp <= ATT_TK and NSA_TQ <= ATT_TK
    rows = NSA_REP * NSA_TQ
    win_shifts = (WIN_SIZE - 1 + ATT_TK - 1 + NSA_TQ - LANES) // LANES + 1
    win_tab = _mask_table(win_shifts, lambda d: (d >= 0) & (d <= WIN_SIZE - 1))
    causal_tab = _mask_table(ATT_TK // LANES, lambda d: d >= 0)

    def seq_spec(cb):
        return pl.BlockSpec((seq, HEAD_DIM), lambda b, g, i: (b, cb + g))

    def cmp_spec(which):
        return pl.BlockSpec((None, None, ncp, HEAD_DIM), lambda b, g, i: (which, b * N_NSA_KV + g, 0, 0))

    def whole(a):
        return pl.BlockSpec(a.shape, lambda b, g, i: (0,) * a.ndim)

    qo_spec = pl.BlockSpec((NSA_TQ, NSA_REP * HEAD_DIM), lambda b, g, i: (b * nq + i, g))
    onehot = _block_onehot_table(seq)
    qx = _alibi_rows(slopes)
    return pl.pallas_call(
        functools.partial(_nsa_kernel, seq=seq),
        out_shape=jax.ShapeDtypeStruct((t, Q_NSA_DIM), BF16),
        grid=(batch, N_NSA_KV, nq),
        in_specs=[
            qo_spec,
            pl.BlockSpec((NSA_REP, SUBLANES, LANES), lambda b, g, i: (g, 0, 0)),
            cmp_spec(0), cmp_spec(1),
            seq_spec(CB_K_SLC), seq_spec(CB_V_SLC), seq_spec(CB_K_WIN), seq_spec(CB_V_WIN),
            whole(rel_tab), whole(onehot), whole(win_tab), whole(causal_tab),
            pl.BlockSpec((None, NSA_TQ, 3 * NSA_REP), lambda b, g, i: (g, b * nq + i, 0)),
        ],
        out_specs=qo_spec,
        scratch_shapes=[
            pltpu.VMEM((rows, HEAD_DIM + LANES), BF16),
            pltpu.VMEM((rows, ATT_TK), F32),
            pltpu.VMEM((rows, ATT_TK), F32),
            pltpu.VMEM((rows, ATT_TK), BF16),
            pltpu.VMEM((rows, LANES), F32),
            pltpu.VMEM((rows, LANES), F32),
            pltpu.VMEM((rows, HEAD_DIM), F32),
            pltpu.VMEM((2, rows, HEAD_DIM), F32),
            pltpu.VMEM((LANES, NSA_TQ), F32),
        ],
        compiler_params=_params("arbitrary", "arbitrary", "arbitrary"),
        name="nsa_mixer",
    )(proj, qx, cmp_kv, cmp_kv, proj, proj, proj, proj, rel_tab, onehot, win_tab, causal_tab, gates_t)


def _dil_kernel(q_ref, qx_ref, k_ref, v_ref, rel_ref, mtab_ref, o_ref,
                qa_scr, s_a_scr, s_b_scr, p_scr, m_scr, l_scr, acc_scr, *, seq):
    t0 = pl.program_id(2) * DIL_TQ
    qa_scr[:, :HEAD_DIM] = q_ref[...]
    qa_scr[:, HEAD_DIM:] = jnp.broadcast_to(qx_ref[0:1, :], (DIL_TQ, LANES)).astype(BF16)
    _softmax_init(m_scr, l_scr, acc_scr)
    masked = mtab_ref.shape[0] - 1

    def tile_rows(kt):
        return pl.ds(pl.multiple_of(kt * ATT_TK, ATT_TK), ATT_TK)

    def k_aug(kt):
        rel = rel_ref[pl.ds(pl.multiple_of(kt * ATT_TK - t0 + seq, LANES), ATT_TK), :]
        return jnp.concatenate([k_ref[tile_rows(kt), :], rel], axis=1)

    def multiplicity(kt):
        def for_block(rb):
            u = (t0 + rb * LANES - kt * ATT_TK) // LANES
            return mtab_ref[jnp.where((u < 0) | (u >= masked), masked, u)]
        return for_block

    reach = max(w for w, _ in DIL_PATTERNS)
    first_kt = jnp.maximum(t0 - reach, 0) // ATT_TK
    _attend(first_kt, (t0 + DIL_TQ - 1) // ATT_TK + 1, qa_scr, (s_a_scr, s_b_scr), p_scr,
            m_scr, l_scr, acc_scr, k_aug, lambda kt: v_ref[tile_rows(kt), :], multiplicity)
    o_ref[...] = (acc_scr[...] / l_scr[...]).astype(o_ref.dtype)


def _dil_mixer(slopes, proj, rel_tab, batch, seq):
    t = proj.shape[0]
    nq = seq // DIL_TQ
    mult_tab = _dilation_table()
    qx = _alibi_rows(slopes)

    def whole(a):
        return pl.BlockSpec(a.shape, lambda b, h, i: (0,) * a.ndim)

    return pl.pallas_call(
        functools.partial(_dil_kernel, seq=seq),
        out_shape=jax.ShapeDtypeStruct((t, DIL_DIM), BF16),
        grid=(batch, N_DIL_HEADS, nq),
        in_specs=[
            pl.BlockSpec((DIL_TQ, HEAD_DIM), lambda b, h, i: (b * nq + i, CB_Q_DIL + h)),
            pl.BlockSpec((None, SUBLANES, LANES), lambda b, h, i: (h, 0, 0)),
            pl.BlockSpec((seq, HEAD_DIM), lambda b, h, i: (b, CB_K_DIL + h)),
            pl.BlockSpec((seq, HEAD_DIM), lambda b, h, i: (b, CB_V_DIL + h)),
            whole(rel_tab), whole(mult_tab),
        ],
        out_specs=pl.BlockSpec((DIL_TQ, HEAD_DIM), lambda b, h, i: (b * nq + i, h)),
        scratch_shapes=[
            pltpu.VMEM((DIL_TQ, HEAD_DIM + LANES), BF16),
            pltpu.VMEM((DIL_TQ, ATT_TK), F32),
            pltpu.VMEM((DIL_TQ, ATT_TK), F32),
            pltpu.VMEM((DIL_TQ, ATT_TK), BF16),
            pltpu.VMEM((DIL_TQ, LANES), F32),
            pltpu.VMEM((DIL_TQ, LANES), F32),
            pltpu.VMEM((DIL_TQ, HEAD_DIM), F32),
        ],
        compiler_params=_params("arbitrary", "arbitrary", "arbitrary"),
        name="dil_mixer",
    )(proj, qx, proj, proj, rel_tab, mult_tab)


def _out_proj_kernel(a_ref, b_ref, ga_ref, gb_ref, w_ref, res_ref, o_ref, h_scr):
    @pl.when(pl.program_id(1) == 0)
    def _():
        na = a_ref.shape[1]
        h_scr[:, :na] = _rms_rows(a_ref[...].astype(F32), ga_ref[...]).astype(BF16)
        h_scr[:, na:] = _rms_rows(b_ref[...].astype(F32), gb_ref[...]).astype(BF16)

    o_ref[...] = res_ref[...] + _dot(h_scr[...], w_ref[...])


def _out_proj(o_nsa, o_dil, g_nsa, g_dil, w, resid):
    t, d = resid.shape
    na, nb = o_nsa.shape[1], o_dil.shape[1]
    return pl.pallas_call(
        _out_proj_kernel,
        out_shape=jax.ShapeDtypeStruct((t, d), F32),
        grid=(t // OUT_TM, d // PROJ_TN),
        in_specs=[
            pl.BlockSpec((OUT_TM, na), lambda i, j: (i, 0)),
            pl.BlockSpec((OUT_TM, nb), lambda i, j: (i, 0)),
            pl.BlockSpec((1, na), lambda i, j: (0, 0)),
            pl.BlockSpec((1, nb), lambda i, j: (0, 0)),
            pl.BlockSpec((na + nb, PROJ_TN), lambda i, j: (0, j)),
            pl.BlockSpec((OUT_TM, PROJ_TN), lambda i, j: (i, j)),
        ],
        out_specs=pl.BlockSpec((OUT_TM, PROJ_TN), lambda i, j: (i, j)),
        scratch_shapes=[pltpu.VMEM((OUT_TM, na + nb), BF16)],
        compiler_params=_params("parallel", "arbitrary"),
        name="out_proj",
    )(o_nsa, o_dil, g_nsa, g_dil, w, resid)


def _router_kernel(x_ref, g_ref, wr_ref, br_ref, hn_ref, idx_ref, tw_ref):
    h = _rms_rows(x_ref[...], g_ref[...])
    hn_ref[...] = h
    logits = _dot(h, wr_ref[...], precision=lax.Precision.HIGHEST) + br_ref[...]
    lane = lax.broadcasted_iota(I32, logits.shape, 1)
    lanef = lane.astype(F32)
    work = jnp.where(lane < N_EXPERTS, logits, -jnp.inf)
    vals, ids = [], []
    for _ in range(TOP_K):
        mx = jnp.max(work, axis=-1, keepdims=True)
        first = jnp.min(jnp.where(work == mx, lanef, float(LANES)), axis=-1, keepdims=True)
        vals.append(mx)
        ids.append(first)
        work = jnp.where(lanef == first, -jnp.inf, work)
    es = [jnp.exp(v - vals[0]) for v in vals]
    den = functools.reduce(lambda a, b: a + b, es)
    idx_out = jnp.zeros(logits.shape, F32)
    tw_out = jnp.zeros(logits.shape, F32)
    for k in range(TOP_K):
        idx_out = jnp.where(lane == k, ids[k], idx_out)
        tw_out = jnp.where(lane == k, es[k] / den, tw_out)
    idx_ref[...] = idx_out.astype(I32)
    tw_ref[...] = tw_out


def _router(x1, gain, wr, br):
    t, d = x1.shape
    return pl.pallas_call(
        _router_kernel,
        out_shape=(jax.ShapeDtypeStruct((t, d), F32),
                   jax.ShapeDtypeStruct((t, LANES), I32),
                   jax.ShapeDtypeStruct((t, LANES), F32)),
        grid=(t // ROUTER_TM,),
        in_specs=[
            pl.BlockSpec((ROUTER_TM, d), lambda i: (i, 0)),
            pl.BlockSpec((1, d), lambda i: (0, 0)),
            pl.BlockSpec((d, LANES), lambda i: (0, 0)),
            pl.BlockSpec((1, LANES), lambda i: (0, 0)),
        ],
        out_specs=(pl.BlockSpec((ROUTER_TM, d), lambda i: (i, 0)),
                   pl.BlockSpec((ROUTER_TM, LANES), lambda i: (i, 0)),
                   pl.BlockSpec((ROUTER_TM, LANES), lambda i: (i, 0))),
        compiler_params=_params("parallel"),
        name="moe_router",
    )(x1, gain, wr, br)


def _rank_kernel(idx_ref, rank_ref, cnt_ref, carry_scr):
    @pl.when(pl.program_id(0) == 0)
    def _():
        carry_scr[...] = jnp.zeros(carry_scr.shape, F32)

    idx = idx_ref[...]
    lane = lax.broadcasted_iota(I32, idx.shape, 1)
    hits = [lane == idx[:, k:k + 1] for k in range(TOP_K)]
    onehot = functools.reduce(lambda a, b: a + b, [jnp.where(h, 1.0, 0.0) for h in hits])
    ri = lax.broadcasted_iota(I32, (RANK_TM, RANK_TM), 0)
    ci = lax.broadcasted_iota(I32, (RANK_TM, RANK_TM), 1)
    before = jnp.where(ci < ri, 1.0, 0.0).astype(BF16)
    rank = _dot(before, onehot.astype(BF16)) + carry_scr[0:1, :]
    out = jnp.zeros(idx.shape, F32)
    for k in range(TOP_K):
        mine = jnp.sum(jnp.where(hits[k], rank, 0.0), axis=-1, keepdims=True)
        out = jnp.where(lane == k, mine, out)
    rank_ref[...] = out.astype(I32)
    carry = carry_scr[...] + jnp.sum(onehot, axis=0, keepdims=True)
    carry_scr[...] = carry
    cnt_ref[...] = carry


def _rank(idx):
    t = idx.shape[0]
    return pl.pallas_call(
        _rank_kernel,
        out_shape=(jax.ShapeDtypeStruct((t, LANES), I32), jax.ShapeDtypeStruct((SUBLANES, LANES), F32)),
        grid=(t // RANK_TM,),
        in_specs=[pl.BlockSpec((RANK_TM, LANES), lambda i: (i, 0))],
        out_specs=(pl.BlockSpec((RANK_TM, LANES), lambda i: (i, 0)),
                   pl.BlockSpec((SUBLANES, LANES), lambda i: (0, 0))),
        scratch_shapes=[pltpu.VMEM((SUBLANES, LANES), F32)],
        compiler_params=_params("arbitrary"),
        name="moe_rank",
    )(idx)


def _pos_kernel(cnt_ref, idx_ref, rank_ref, pos_ref, vt_ref):
    cnt = cnt_ref[0:1, :]
    lane1 = lax.broadcasted_iota(I32, (1, LANES), 1)
    tiles = jnp.where(lane1 < N_EXPERTS, jnp.floor((cnt + (MOE_TM - 0.5)) / MOE_TM), 0.0)
    tiles_before = jnp.zeros((1, LANES), F32)
    rows_before = jnp.zeros((1, LANES), F32)
    for e in range(N_EXPERTS):
        tiles_before = tiles_before + jnp.where(lane1 > e, tiles[:, e:e + 1], 0.0)
        rows_before = rows_before + jnp.where(lane1 > e, cnt[:, e:e + 1], 0.0)
    tiles_through = tiles_before + tiles

    idx = idx_ref[...]
    rank = rank_ref[...].astype(F32)
    lane = lax.broadcasted_iota(I32, idx.shape, 1)
    out = jnp.zeros(idx.shape, F32)
    for k in range(TOP_K):
        mine = lane == idx[:, k:k + 1]
        row0 = jnp.sum(jnp.where(mine, tiles_before * MOE_TM, 0.0), axis=-1, keepdims=True)
        slot0 = jnp.sum(jnp.where(mine, rows_before, 0.0), axis=-1, keepdims=True)
        out = jnp.where(lane == k, row0 + rank[:, k:k + 1], out)
        out = jnp.where(lane == TOP_K + k, slot0 + rank[:, k:k + 1], out)
    pos_ref[...] = out.astype(I32)

    nv = vt_ref.shape[0]
    vl = lax.broadcasted_iota(I32, (nv, LANES), 1)
    v = lax.broadcasted_iota(I32, (nv, LANES), 0).astype(F32)
    total = jnp.sum(tiles, axis=-1, keepdims=True)
    vv = jnp.minimum(v, total - 1.0)
    done = jnp.where((tiles_through <= vv) & (vl < N_EXPERTS), 1.0, 0.0)
    e_v = jnp.minimum(jnp.sum(done, axis=-1, keepdims=True), float(N_EXPERTS - 1))
    mine = vl.astype(F32) == e_v
    first_tile = jnp.sum(jnp.where(mine, tiles_before, 0.0), axis=-1, keepdims=True)
    e_cnt = jnp.sum(jnp.where(mine, cnt, 0.0), axis=-1, keepdims=True)
    e_slot = jnp.sum(jnp.where(mine, rows_before, 0.0), axis=-1, keepdims=True)
    vv1 = vv[:, 0:1]
    valid = v[:, 0:1] < total
    offset = (vv1 - first_tile) * MOE_TM
    held = jnp.where(valid, jnp.clip(e_cnt - offset, 0.0, float(MOE_TM)), 0.0)
    nrows = jnp.ceil(held * (1.0 / MOE_HALF)) * MOE_HALF
    cols = [e_v, vv1, nrows, jnp.where(valid, 1.0, 0.0), e_slot + offset, held, total]
    table = jnp.zeros((nv, LANES), F32)
    for c, val in enumerate(cols):
        table = jnp.where(vl == c, val, table)
    vt_ref[...] = table.astype(I32)


def _positions(cnt, idx, rank, n_visits):
    t = idx.shape[0]
    nvp = -(-n_visits // SUBLANES) * SUBLANES
    return pl.pallas_call(
        _pos_kernel,
        out_shape=(jax.ShapeDtypeStruct((t, LANES), I32), jax.ShapeDtypeStruct((nvp, LANES), I32)),
        grid=(t // POS_TM,),
        in_specs=[pl.BlockSpec((SUBLANES, LANES), lambda i: (0, 0)),
                  pl.BlockSpec((POS_TM, LANES), lambda i: (i, 0)),
                  pl.BlockSpec((POS_TM, LANES), lambda i: (i, 0))],
        out_specs=(pl.BlockSpec((POS_TM, LANES), lambda i: (i, 0)),
                   pl.BlockSpec((nvp, LANES), lambda i: (0, 0))),
        compiler_params=_params("arbitrary"),
        name="moe_positions",
    )(cnt, idx, rank)


def _invert_kernel(slot_ref, tok_ref):
    def body(g, carry):
        base = g * INVERT_UNROLL
        slots = [slot_ref[base + u] for u in range(INVERT_UNROLL)]
        for u in range(INVERT_UNROLL):
            tok_ref[slots[u]] = (base + u) >> top_k_shift
        return carry

    top_k_shift = TOP_K.bit_length() - 1
    assert TOP_K == 1 << top_k_shift and slot_ref.shape[0] % INVERT_UNROLL == 0
    lax.fori_loop(0, slot_ref.shape[0] // INVERT_UNROLL, body, 0)


def _invert(slot_flat):
    return pl.pallas_call(
        _invert_kernel,
        out_shape=jax.ShapeDtypeStruct(slot_flat.shape, I32),
        in_specs=[pl.BlockSpec(memory_space=pltpu.SMEM)],
        out_specs=pl.BlockSpec(memory_space=pltpu.SMEM),
        name="moe_invert",
    )(slot_flat)


def _dispatch_kernel(vt_ref, tok_ref, hn_ref, xs_ref, buf, sems):
    i = pl.program_id(0)
    per_visit = MOE_TM // DISPATCH_TM

    def held_rows(step):
        v = step // per_visit
        part = (step % per_visit) * DISPATCH_TM
        return jnp.clip(vt_ref[v * VT_W + 5] - part, 0, DISPATCH_TM), vt_ref[v * VT_W + 4] + part

    def row_copy(src_row, r, slot):
        return pltpu.make_async_copy(hn_ref.at[pl.ds(src_row, 1)], buf.at[slot, pl.ds(r, 1)], sems.at[slot])

    def start(step, slot):
        n, base = held_rows(step)
        issue = lambda r, c: (row_copy(tok_ref[base + r], r, slot).start(), c)[1]

        @pl.when(n == DISPATCH_TM)
        def _():
            lax.fori_loop(0, DISPATCH_TM, issue, 0, unroll=GATHER_UNROLL)

        @pl.when(n < DISPATCH_TM)
        def _():
            lax.fori_loop(0, n, issue, 0)

    def wait(step, slot):
        n, _ = held_rows(step)

        @pl.when(n == DISPATCH_TM)
        def _():
            pltpu.make_async_copy(hn_ref.at[pl.ds(0, DISPATCH_TM)], buf.at[slot], sems.at[slot]).wait()

        @pl.when(n < DISPATCH_TM)
        def _():
            lax.fori_loop(0, n, lambda r, c: (row_copy(0, r, slot).wait(), c)[1], 0)

    @pl.when(i == 0)
    def _():
        buf[...] = jnp.zeros(buf.shape, buf.dtype)
        start(0, 0)

    @pl.when(i + 1 < pl.num_programs(0))
    def _():
        start(i + 1, (i + 1) % 2)

    wait(i, i % 2)
    n, _ = held_rows(i)
    rowid = lax.broadcasted_iota(I32, (DISPATCH_TM, 1), 0)
    xs_ref[...] = jnp.where(rowid < n, buf[i % 2], 0.0).astype(xs_ref.dtype)


def _dispatch(vt, tok, hn, n_visits):
    d = hn.shape[1]
    return pl.pallas_call(
        _dispatch_kernel,
        out_shape=jax.ShapeDtypeStruct((n_visits * MOE_TM, d), BF16),
        grid_spec=pltpu.PrefetchScalarGridSpec(
            num_scalar_prefetch=2,
            grid=(n_visits * MOE_TM // DISPATCH_TM,),
            in_specs=[pl.BlockSpec(memory_space=pl.ANY)],
            out_specs=pl.BlockSpec((DISPATCH_TM, d), lambda i, vt, tok: (i, 0)),
            scratch_shapes=[pltpu.VMEM((2, DISPATCH_TM, d), F32), pltpu.SemaphoreType.DMA((2,))],
        ),
        compiler_params=_params("arbitrary"),
        name="moe_dispatch",
    )(vt, tok, hn)


def _visit_blocks(nrows, compute, clear):
    for sb in range(MOE_TM // MOE_SUB):
        lo, mid, hi = sb * MOE_SUB, sb * MOE_SUB + MOE_HALF, (sb + 1) * MOE_SUB
        pl.when(nrows >= hi)(functools.partial(compute, slice(lo, hi), sb == 0))

        @pl.when(nrows == mid)
        def _(lo=lo, mid=mid, hi=hi, sb=sb):
            compute(slice(lo, mid), sb == 0)
            clear(slice(mid, hi))

        pl.when(nrows <= lo)(functools.partial(clear, slice(lo, hi)))


def _moe_up_kernel(vt_ref, xs_ref, wg_ref, wu_ref, bg_ref, bu_ref, h_ref, wg_scr, wu_scr):
    def compute(rows, first):
        if first:
            wg_scr[...] = wg_ref[...].astype(BF16)
            wu_scr[...] = wu_ref[...].astype(BF16)
        x = xs_ref[rows, :]
        gate = jnp.minimum(_dot(x, wg_scr[...]) + bg_ref[...], SWIGLU_LIMIT)
        up = jnp.clip(_dot(x, wu_scr[...]) + bu_ref[...], -SWIGLU_LIMIT, SWIGLU_LIMIT)
        act = (up + 1.0) * (gate * _sigmoid(SWIGLU_ALPHA * gate))
        h_ref[rows, :] = act.astype(h_ref.dtype)

    def clear(rows):
        h_ref[rows, :] = jnp.zeros((rows.stop - rows.start, h_ref.shape[1]), h_ref.dtype)

    _visit_blocks(vt_ref[pl.program_id(0) * VT_W + 2], compute, clear)


def _moe_up(vt, xs, w_gate_up, b_gate_up, n_visits):
    d = w_gate_up.shape[1]
    nf = D_EXPERT // MOE_TF

    def fsel(v, f, vt):
        return jnp.where(vt[v * VT_W + 3] > 0, f, nf - 1)

    return pl.pallas_call(
        _moe_up_kernel,
        out_shape=jax.ShapeDtypeStruct((n_visits * MOE_TM, D_EXPERT), BF16),
        grid_spec=pltpu.PrefetchScalarGridSpec(
            num_scalar_prefetch=1,
            grid=(n_visits, nf),
            in_specs=[
                pl.BlockSpec((MOE_TM, d), lambda v, f, vt: (vt[v * VT_W + 1], 0)),
                pl.BlockSpec((None, d, MOE_TF), lambda v, f, vt: (vt[v * VT_W], 0, fsel(v, f, vt))),
                pl.BlockSpec((None, d, MOE_TF), lambda v, f, vt: (vt[v * VT_W], 0, nf + fsel(v, f, vt))),
                pl.BlockSpec((None, 1, MOE_TF), lambda v, f, vt: (vt[v * VT_W], 0, fsel(v, f, vt))),
                pl.BlockSpec((None, 1, MOE_TF), lambda v, f, vt: (vt[v * VT_W], 0, nf + fsel(v, f, vt))),
            ],
            out_specs=pl.BlockSpec((MOE_TM, MOE_TF), lambda v, f, vt: (v, f)),
            scratch_shapes=[pltpu.VMEM((d, MOE_TF), BF16), pltpu.VMEM((d, MOE_TF), BF16)],
        ),
        compiler_params=_params("arbitrary", "arbitrary"),
        name="moe_up",
    )(vt, xs, w_gate_up, w_gate_up, b_gate_up, b_gate_up)


def _moe_down_kernel(vt_ref, h_ref, wd_ref, bd_ref, y_ref, wd_scr):
    def compute(rows, first):
        if first:
            wd_scr[...] = wd_ref[...].astype(BF16)
        y_ref[rows, :] = _dot(h_ref[rows, :], wd_scr[...]) + bd_ref[...]

    def clear(rows):
        y_ref[rows, :] = jnp.zeros((rows.stop - rows.start, y_ref.shape[1]), y_ref.dtype)

    _visit_blocks(vt_ref[pl.program_id(0) * VT_W + 2], compute, clear)


def _moe_down(vt, h, w_down, b_down, n_visits):
    d = w_down.shape[2]
    nj = d // MOE_TN

    def jsel(v, j, vt):
        return jnp.where(vt[v * VT_W + 3] > 0, j, nj - 1)

    return pl.pallas_call(
        _moe_down_kernel,
        out_shape=jax.ShapeDtypeStruct((n_visits * MOE_TM, d), F32),
        grid_spec=pltpu.PrefetchScalarGridSpec(
            num_scalar_prefetch=1,
            grid=(n_visits, nj),
            in_specs=[
                pl.BlockSpec((MOE_TM, D_EXPERT), lambda v, j, vt: (vt[v * VT_W + 1], 0)),
                pl.BlockSpec((None, D_EXPERT, MOE_TN), lambda v, j, vt: (vt[v * VT_W], 0, jsel(v, j, vt))),
                pl.BlockSpec((None, 1, MOE_TN), lambda v, j, vt: (vt[v * VT_W], 0, jsel(v, j, vt))),
            ],
            out_specs=pl.BlockSpec((MOE_TM, MOE_TN), lambda v, j, vt: (v, j)),
            scratch_shapes=[pltpu.VMEM((D_EXPERT, MOE_TN), BF16)],
        ),
        compiler_params=_params("arbitrary", "arbitrary"),
        name="moe_down",
    )(vt, h, w_down, b_down)


def _combine_kernel(pos_ref, ys_ref, x_ref, tw_ref, o_ref, buf, sems):
    i = pl.program_id(0)

    def start(step, slot):
        def issue(r, carry):
            for k in range(TOP_K):
                src = pos_ref[(step * COMBINE_TM + r) * TOP_K + k]
                pltpu.make_async_copy(ys_ref.at[pl.ds(src, 1)], buf.at[slot, k, pl.ds(r, 1)], sems.at[slot]).start()
            return carry
        lax.fori_loop(0, COMBINE_TM, issue, 0, unroll=GATHER_UNROLL)

    def wait(slot):
        for k in range(TOP_K):
            pltpu.make_async_copy(ys_ref.at[pl.ds(0, COMBINE_TM)], buf.at[slot, k], sems.at[slot]).wait()

    @pl.when(i == 0)
    def _():
        start(0, 0)

    @pl.when(i + 1 < pl.num_programs(0))
    def _():
        start(i + 1, (i + 1) % 2)

    wait(i % 2)
    tw = tw_ref[...]
    acc = x_ref[...]
    for k in range(TOP_K):
        acc = acc + tw[:, k:k + 1] * buf[i % 2, k]
    o_ref[...] = acc


def _combine(pos_flat, ys, x1, tw):
    t, d = x1.shape
    return pl.pallas_call(
        _combine_kernel,
        out_shape=jax.ShapeDtypeStruct((t, d), F32),
        grid_spec=pltpu.PrefetchScalarGridSpec(
            num_scalar_prefetch=1,
            grid=(t // COMBINE_TM,),
            in_specs=[pl.BlockSpec(memory_space=pl.ANY),
                      pl.BlockSpec((COMBINE_TM, d), lambda i, pos: (i, 0)),
                      pl.BlockSpec((COMBINE_TM, LANES), lambda i, pos: (i, 0))],
            out_specs=pl.BlockSpec((COMBINE_TM, d), lambda i, pos: (i, 0)),
            scratch_shapes=[pltpu.VMEM((2, TOP_K, COMBINE_TM, d), F32), pltpu.SemaphoreType.DMA((2,))],
        ),
        compiler_params=_params("arbitrary"),
        name="moe_combine",
    )(pos_flat, ys, x1, tw)


def _alibi_slopes():
    n = N_NSA_HEADS + N_DIL_HEADS
    i = jnp.arange(1, n + 1, dtype=F32)
    return jnp.exp2(-8.0 * i / n)


def _attention_block(x2, batch, seq, attn_norm, w_in, pe_k, w_k1, w_k2, pe_v, w_v1, w_v2,
                     qn_nsa, kn_cmp, kn_slc, kn_win, qn_dil, kn_dil, on_nsa, on_dil, w_out):
    d = x2.shape[1]
    a_dim = Q_NSA_DIM + 6 * KV_NSA_DIM
    d_off = a_dim + GATE_DIM
    n_pad = N_PROJ - (a_dim + 3 * DIL_DIM + GATE_DIM)
    assert a_dim % PROJ_TN == 0
    wa = w_in[:, :a_dim].astype(BF16)
    wb = jnp.concatenate([w_in[:, d_off:].astype(BF16), w_in[:, a_dim:d_off].astype(BF16),
                          jnp.zeros((d, n_pad), BF16)], axis=1)
    ones_kv = jnp.ones((KV_NSA_DIM,), F32)
    ones_dil = jnp.ones((DIL_DIM,), F32)
    tail = jnp.ones((GATE_DIM + n_pad,), F32)
    q_scale = ATTN_SCALE * LOG2E
    col_gain = jnp.concatenate([
        jnp.tile(qn_nsa, N_NSA_HEADS) * q_scale, ones_kv, ones_kv, jnp.tile(kn_slc, N_NSA_KV), ones_kv,
        jnp.tile(kn_win, N_NSA_KV), ones_kv, jnp.tile(qn_dil, N_DIL_HEADS) * q_scale,
        jnp.tile(kn_dil, N_DIL_HEADS), ones_dil, tail])[None, :]
    col_flag = jnp.concatenate([
        jnp.ones((Q_NSA_DIM,), F32), 0 * ones_kv, 0 * ones_kv, ones_kv, 0 * ones_kv, ones_kv, 0 * ones_kv,
        ones_dil, ones_dil, 0 * ones_dil, 0 * tail])[None, :]
    proj = _in_proj(x2, attn_norm[None, :], wa, wb, col_gain, col_flag)

    n_chunks = seq // CMP_STRIDE
    kv = proj[:, CB_K_CMP * LANES:(CB_V_CMP + N_NSA_KV) * LANES]
    kv = kv.reshape(batch, n_chunks, CMP_STRIDE, 2, N_NSA_KV, HEAD_DIM).transpose(3, 0, 4, 1, 2, 5)
    chunks = kv.reshape(2, batch * N_NSA_KV, n_chunks, CMP_STRIDE * HEAD_DIM)
    assert CMP_BLOCK == 2 * CMP_STRIDE
    blocks = jnp.concatenate([chunks[:, :, :-1], chunks[:, :, 1:]], axis=-1)
    blocks = jnp.pad(blocks, ((0, 0), (0, 0), (0, 1), (0, 0)))
    pe = jnp.stack([pe_k.reshape(1, -1), pe_v.reshape(1, -1)])
    w1 = jnp.stack([w_k1, w_v1]).astype(BF16)
    w2 = jnp.stack([w_k2, w_v2]).astype(BF16)
    cmp_kv = _compress(blocks, pe, w1, w2, kn_cmp[None, :])

    slopes = _alibi_slopes()
    rel_tab = _rel_position_table(seq)
    gates = proj[:, CB_GATE * LANES:CB_GATE * LANES + GATE_DIM]
    gates_t = gates.reshape(-1, N_NSA_KV, 3 * NSA_REP).transpose(1, 0, 2)
    o_nsa = _nsa_mixer(slopes[0::2], proj, cmp_kv, gates_t, rel_tab, batch, seq)
    o_dil = _dil_mixer(slopes[1::2], proj, rel_tab, batch, seq)
    return _out_proj(o_nsa, o_dil, on_nsa[None, :], on_dil[None, :], w_out.astype(BF16), x2)


def _moe_block(x1, ffn_norm, w_router, b_router, w_gate_up, b_gate_up, w_down, b_down):
    t, d = x1.shape
    wr = jnp.pad(w_router, ((0, 0), (0, LANES - N_EXPERTS)))
    br = jnp.pad(b_router, (0, LANES - N_EXPERTS))[None, :]
    hn, idx, tw = _router(x1, ffn_norm[None, :], wr, br)
    rank, cnt = _rank(idx)
    n_worst = N_EXPERTS + (t * TOP_K) // MOE_TM
    pos, table = _positions(cnt, idx, rank, n_worst)
    pos_flat = pos[:, :TOP_K].reshape(-1)
    tok = _invert(pos[:, TOP_K:2 * TOP_K].reshape(-1))
    bias_gu, bias_d = b_gate_up[:, None, :], b_down[:, None, :]

    def experts(n_visits):
        vt = table[:n_visits, :VT_W].reshape(-1)
        xs = _dispatch(vt, tok, hn, n_visits)
        h = _moe_up(vt, xs, w_gate_up, bias_gu, n_visits)
        ys = _moe_down(vt, h, w_down, bias_d, n_visits)
        return _combine(pos_flat, ys, x1, tw)

    n_even = min(n_worst, N_EXPERTS + MOE_SPARE_VISITS)
    return lax.cond(table[0, VT_TOTAL] <= n_even, lambda: experts(n_even), lambda: experts(n_worst))


def kernel(x, attn_norm, w_in, cmp_pos_k, w_cmp_k1, w_cmp_k2, cmp_pos_v, w_cmp_v1, w_cmp_v2, q_norm_nsa, k_norm_cmp, k_norm_slc, k_norm_win, q_norm_dil, k_norm_dil, out_norm_nsa, out_norm_dil, w_out, ffn_norm, w_router, b_router, w_gate_up, b_gate_up, w_down, b_down):
    batch, seq, d = x.shape
    x2 = x.reshape(batch * seq, d)
    for layer in range(attn_norm.shape[0]):
        x2 = _attention_block(
            x2, batch, seq, attn_norm[layer], w_in[layer], cmp_pos_k[layer], w_cmp_k1[layer],
            w_cmp_k2[layer], cmp_pos_v[layer], w_cmp_v1[layer], w_cmp_v2[layer], q_norm_nsa[layer],
            k_norm_cmp[layer], k_norm_slc[layer], k_norm_win[layer], q_norm_dil[layer],
            k_norm_dil[layer], out_norm_nsa[layer], out_norm_dil[layer], w_out[layer])
        x2 = _moe_block(x2, ffn_norm[layer], w_router[layer], b_router[layer], w_gate_up[layer],
                        b_gate_up[layer], w_down[layer], b_down[layer])
    return x2.reshape(batch, seq, d)
```
